```python
import math
import jax
import jax.numpy as jnp
from jax import lax
import numpy as np

D_MODEL = 2048
BATCH = 1
SEQ = 8192
DEPTH = 4
DEC_BATCH = 16
DEC_SEQ = 16
PAST_LEN = 1024

CHUNK = 64
Q_BLOCK = 128
N_EVEN = (DEPTH + 1) // 2
N_ODD = DEPTH // 2
EPS = 1e-6
D_RNN = D_MODEL // 2
LRU_BLOCKS = 16
LRU_BW = D_RNN // LRU_BLOCKS
CONV_W = 4
LRU_C = 8.0
B_HEADS = 8
B_DH = 64
B_W = B_HEADS * 2 * B_DH
EVEN_IN = 2 * D_RNN + 3 * B_W
MIX_W = D_RNN + B_W
N_BUCKETS = 32
MAX_DIST = 512
GMLP_CHUNK = 128
D_C = D_MODEL
C_GROUPS = 16
C_GW = D_C // C_GROUPS
D_FF = 4 * D_MODEL

kernel_name = 'hybrid_rglru_diffattn_gmlp_stream_step'


def rms_norm(x, w):
    xf = x.astype(jnp.float32)
    y = xf * lax.rsqrt(jnp.mean(xf * xf, axis=-1, keepdims=True) + EPS)
    return (y * w.astype(jnp.float32)).astype(x.dtype)


def rel_bucket(rel):
    nb = N_BUCKETS // 2
    max_exact = nb // 2
    n = jnp.abs(rel)
    nf = jnp.maximum(n, 1).astype(jnp.float32)
    large = max_exact + (jnp.log(nf / max_exact) / math.log(MAX_DIST / max_exact)
                         * (nb - max_exact)).astype(jnp.int32)
    large = jnp.minimum(large, nb - 1)
    return jnp.where(rel > 0, nb, 0) + jnp.where(n < max_exact, n, large)


def rel_bias(bias_table, qpos, kpos):
    b = bias_table[rel_bucket(kpos[None, :] - qpos[:, None])]
    return jnp.transpose(b, (2, 0, 1)).astype(jnp.float32)


def causal_conv(xr, prev, w, b):
    T = xr.shape[1]
    xp = jnp.concatenate([prev.astype(xr.dtype), xr], axis=1)
    y = b + sum(xp[:, j:j + T] * w[j] for j in range(CONV_W))
    return y, xp[:, -(CONV_W - 1):]


def rg_lru(xc, h0, wa, ba, wx, bx, lru_lambda):
    B, T, _ = xc.shape
    xf = xc.astype(jnp.float32)
    xg = xf.reshape(B, T, LRU_BLOCKS, LRU_BW)
    r = jax.nn.sigmoid(jnp.einsum('btgi,gij->btgj', xg, wa.astype(jnp.float32)).reshape(B, T, D_RNN)
                       + ba.astype(jnp.float32))
    i = jax.nn.sigmoid(jnp.einsum('btgi,gij->btgj', xg, wx.astype(jnp.float32)).reshape(B, T, D_RNN)
                       + bx.astype(jnp.float32))
    log_a = -LRU_C * r * jax.nn.softplus(-lru_lambda.astype(jnp.float32))
    a = jnp.exp(log_a)
    u = jnp.sqrt(-jnp.expm1(2.0 * log_a)) * (i * xf)
    u = u.at[:, 0].add(a[:, 0] * h0.astype(jnp.float32))

    def combine(left, right):
        a_l, b_l = left
        a_r, b_r = right
        return a_l * a_r, a_r * b_l + b_r

    _, h = lax.associative_scan(combine, (a, u), axis=1)
    return h, h[:, -1]


def diff_attention(q, k, v, bias, mask, lam, subln_w, lam_init):
    q = q.astype(jnp.float32)
    k = k.astype(jnp.float32)
    v = v.astype(jnp.float32)
    scale = B_DH ** -0.5
    s1 = jnp.einsum('bqhd,bkhd->bhqk', q[..., :B_DH], k[..., :B_DH]) * scale + bias
    s2 = jnp.einsum('bqhd,bkhd->bhqk', q[..., B_DH:], k[..., B_DH:]) * scale + bias
    if mask is not None:
        s1 = jnp.where(mask, s1, -1e30)
        s2 = jnp.where(mask, s2, -1e30)
    p = jax.nn.softmax(s1, axis=-1) - lam * jax.nn.softmax(s2, axis=-1)
    o = jnp.einsum('bhqk,bkhd->bqhd', p, v)
    return rms_norm(o, subln_w) * (1.0 - lam_init)


def prompt_diff_attention(q, k, v, lam, subln_w, lam_init, bias_table):
    B, S, H, E = q.shape
    nb = S // Q_BLOCK
    kf = k.astype(jnp.float32)
    vf = v.astype(jnp.float32)
    qb = jnp.transpose(q.reshape(B, nb, Q_BLOCK, H, E), (1, 0, 2, 3, 4))
    kpos = jnp.arange(S)

    def block(args):
        bi, qi = args
        qpos = bi * Q_BLOCK + jnp.arange(Q_BLOCK)
        bias = rel_bias(bias_table, qpos, kpos)[None]
        mask = (kpos[None, :] // CHUNK) <= (qpos[:, None] // CHUNK)
        return diff_attention(qi, kf, vf, bias, mask, lam, subln_w, lam_init)

    o = lax.map(block, (jnp.arange(nb), qb))
    return jnp.transpose(o, (1, 0, 2, 3, 4)).reshape(B, S, H, E)


def even_mixer(h, conv_prev, h_prev, k_cache, v_cache, lam_init, w_in, w_out, conv_w, conv_b,
               wa, ba, wx, bx, lru_lambda, lam_vecs, subln_w, bias_table):
    B, T, _ = h.shape
    proj = h @ w_in
    g = proj[..., :D_RNN]
    xr = proj[..., D_RNN:2 * D_RNN]
    qkv = proj[..., 2 * D_RNN:].reshape(B, T, 3, B_HEADS, 2 * B_DH)
    q, k, v = qkv[:, :, 0], qkv[:, :, 1], qkv[:, :, 2]
    if conv_prev is None:
        conv_prev = jnp.zeros((B, CONV_W - 1, D_RNN), xr.dtype)
    if h_prev is None:
        h_prev = jnp.zeros((B, D_RNN), jnp.float32)
    xc, conv_new = causal_conv(xr, conv_prev, conv_w, conv_b)
    hs, h_last = rg_lru(xc, h_prev, wa, ba, wx, bx, lru_lambda)
    ya = (jax.nn.gelu(g.astype(jnp.float32)) * hs).astype(h.dtype)
    lv = lam_vecs.astype(jnp.float32)
    lam = jnp.exp(jnp.sum(lv[0] * lv[1])) - jnp.exp(jnp.sum(lv[2] * lv[3])) + lam_init
    if k_cache is None:
        yb = prompt_diff_attention(q, k, v, lam, subln_w, lam_init, bias_table)
    else:
        P = k_cache.shape[1]
        kk = jnp.concatenate([k_cache.astype(k.dtype), k], axis=1)
        vv = jnp.concatenate([v_cache.astype(v.dtype), v], axis=1)
        qpos = P + jnp.arange(T)
        kpos = jnp.arange(P + T)
        bias = rel_bias(bias_table, qpos, kpos)[None]
        yb = diff_attention(q, kk, vv, bias, None, lam, subln_w, lam_init)
    yb = yb.reshape(B, T, B_W).astype(h.dtype)
    y = jnp.concatenate([ya, yb], axis=-1) @ w_out
    return y, conv_new, h_last.astype(h.dtype), k, v


def odd_mixer(h, w_in, w_out, vnorm_w, w_s, b_s):
    B, T, _ = h.shape
    z = jax.nn.gelu(h @ w_in)
    u = z[..., :D_C]
    v = rms_norm(z[..., D_C:], vnorm_w)
    L = min(T, GMLP_CHUNK)
    n = T // L
    tri = jnp.tril(jnp.ones((GMLP_CHUNK, GMLP_CHUNK), w_s.dtype))
    ws = (w_s * tri)[:, :L, :L]
    vg = v.reshape(B, n, L, C_GROUPS, C_GW)
    mix = jnp.einsum('gts,bnsgc->bntgc', ws, vg) + jnp.transpose(b_s[:, :L])[None, None, :, :, None]
    y = (u * mix.reshape(B, T, D_C)) @ w_out
    return y, v


def sq_relu_mlp(h, w_up, w_down):
    return jnp.square(jax.nn.relu(h @ w_up)) @ w_down


def setup_inputs(seed: int = 0) -> dict:
    key = jax.random.key(seed)
    ks = jax.random.split(key, 32)
    f32 = jnp.float32
    nrm = lambda k, s, sc: jax.random.normal(k, s, f32) * sc
    a = jax.random.uniform(ks[12], (N_EVEN, D_RNN), f32, minval=0.9, maxval=0.999)
    return {
        'x_prompt': nrm(ks[0], (BATCH, SEQ, D_MODEL), 1.0),
        'x_sample': nrm(ks[1], (DEC_BATCH, DEC_SEQ, D_MODEL), 1.0),
        'state_conv': nrm(ks[2], (N_EVEN, DEC_BATCH, CONV_W - 1, D_RNN), 1.0),
        'state_lru': nrm(ks[3], (N_EVEN, DEC_BATCH, D_RNN), 0.5),
        'cache_k': nrm(ks[4], (N_EVEN, DEC_BATCH, PAST_LEN, B_HEADS, 2 * B_DH), 1.0),
        'cache_v': nrm(ks[5], (N_EVEN, DEC_BATCH, PAST_LEN, B_HEADS, 2 * B_DH), 1.0),
        'w_even_in': nrm(ks[6], (N_EVEN, D_MODEL, EVEN_IN), D_MODEL ** -0.5),
        'w_even_out': nrm(ks[7], (N_EVEN, MIX_W, D_MODEL), MIX_W ** -0.5),
        'conv_w': nrm(ks[8], (N_EVEN, CONV_W, D_RNN), CONV_W ** -0.5),
        'conv_b': nrm(ks[9], (N_EVEN, D_RNN), 0.01),
        'lru_wa': nrm(ks[10], (N_EVEN, LRU_BLOCKS, LRU_BW, LRU_BW), LRU_BW ** -0.5),
        'lru_ba': nrm(ks[11], (N_EVEN, D_RNN), 0.01),
        'lru_wx': nrm(ks[13], (N_EVEN, LRU_BLOCKS, LRU_BW, LRU_BW), LRU_BW ** -0.5),
        'lru_bx': nrm(ks[14], (N_EVEN, D_RNN), 0.01),
        'lru_lambda': jnp.log(a) - jnp.log1p(-a),
        'lam_vecs': nrm(ks[15], (N_EVEN, 4, B_DH), 0.1),
        'subln_w': 1.0 + nrm(ks[16], (N_EVEN, 2 * B_DH), 0.01),
        'rel_bias_table': nrm(ks[17], (N_BUCKETS, B_HEADS), 0.1),
        'w_odd_in': nrm(ks[18], (N_ODD, D_MODEL, 2 * D_C), D_MODEL ** -0.5),
        'w_odd_out': nrm(ks[19], (N_ODD, D_C, D_MODEL), D_C ** -0.5),
        'gmlp_vnorm_w': 1.0 + nrm(ks[20], (N_ODD, D_C), 0.01),
        'gmlp_ws': nrm(ks[21], (N_ODD, C_GROUPS, GMLP_CHUNK, GMLP_CHUNK), GMLP_CHUNK ** -0.5),
        'gmlp_bs': 1.0 + nrm(ks[22], (N_ODD, C_GROUPS, GMLP_CHUNK), 0.1),
        'norm_mix_w': 1.0 + nrm(ks[23], (DEPTH, D_MODEL), 0.01),
        'norm_ffn_w': 1.0 + nrm(ks[24], (DEPTH, D_MODEL), 0.01),
        'norm_final_w': 1.0 + nrm(ks[25], (D_MODEL,), 0.01),
        'w_ff_up': nrm(ks[26], (DEPTH, D_MODEL, D_FF), D_MODEL ** -0.5),
        'w_ff_down': nrm(ks[27], (DEPTH, D_FF, D_MODEL), D_FF ** -0.5),
    }


def reference(x_prompt, x_sample, state_conv, state_lru, cache_k, cache_v,
              w_even_in, w_even_out, conv_w, conv_b, lru_wa, lru_ba, lru_wx, lru_bx,
              lru_lambda, lam_vecs, subln_w, rel_bias_table, w_odd_in, w_odd_out,
              gmlp_vnorm_w, gmlp_ws, gmlp_bs, norm_mix_w, norm_ffn_w, norm_final_w,
              w_ff_up, w_ff_down):
    yp, ys = x_prompt, x_sample
    p_conv, p_lru, p_k, p_v = [], [], [], []
    s_conv, s_lru, s_k, s_v, s_gv = [], [], [], [], []
    for l in range(DEPTH):
        if l % 2 == 0:
            e = l // 2
            lam_init = 0.8 - 0.6 * math.exp(-0.3 * l)
            ew = (w_even_in[e], w_even_out[e], conv_w[e], conv_b[e], lru_wa[e], lru_ba[e],
                  lru_wx[e], lru_bx[e], lru_lambda[e], lam_vecs[e], subln_w[e], rel_bias_table)
            mp, c, hl, k, v = even_mixer(rms_norm(yp, norm_mix_w[l]), None, None, None, None,
                                         lam_init, *ew)
            p_conv.append(c); p_lru.append(hl); p_k.append(k); p_v.append(v)
            ms, c, hl, k, v = even_mixer(rms_norm(ys, norm_mix_w[l]), state_conv[e], state_lru[e],
                                         cache_k[e], cache_v[e], lam_init, *ew)
            s_conv.append(c); s_lru.append(hl); s_k.append(k); s_v.append(v)
        else:
            o = l // 2
            cw = (w_odd_in[o], w_odd_out[o], gmlp_vnorm_w[o], gmlp_ws[o], gmlp_bs[o])
            mp, _ = odd_mixer(rms_norm(yp, norm_mix_w[l]), *cw)
            ms, gv = odd_mixer(rms_norm(ys, norm_mix_w[l]), *cw)
            s_gv.append(gv)
        yp = yp + mp
        ys = ys + ms
        yp = yp + sq_relu_mlp(rms_norm(yp, norm_ffn_w[l]), w_ff_up[l], w_ff_down[l])
        ys = ys + sq_relu_mlp(rms_norm(ys, norm_ffn_w[l]), w_ff_up[l], w_ff_down[l])
    yp = rms_norm(yp, norm_final_w)
    ys = rms_norm(ys, norm_final_w)
    return (yp, ys, jnp.stack(p_conv), jnp.stack(p_lru), jnp.stack(p_k), jnp.stack(p_v),
            jnp.stack(s_conv), jnp.stack(s_lru), jnp.stack(s_k), jnp.stack(s_v), jnp.stack(s_gv))
```

```python
import functools
import math

import jax
import jax.numpy as jnp
from jax import lax
from jax.experimental import pallas as pl
from jax.experimental.pallas import tpu as pltpu

D_MODEL = 2048
DEPTH = 4
CHUNK = 64
EPS = 1e-6
D_RNN = D_MODEL // 2
LRU_BLOCKS = 16
LRU_BW = D_RNN // LRU_BLOCKS
CONV_W = 4
LRU_C = 8.0
B_HEADS = 8
B_DH = 64
B_E = 2 * B_DH
B_W = B_HEADS * B_E
N_BUCKETS = 32
MAX_DIST = 512
GMLP_CHUNK = 128
D_C = D_MODEL
C_GROUPS = 16
C_GW = D_C // C_GROUPS
D_FF = 4 * D_MODEL

F32 = jnp.float32
BF16 = jnp.bfloat16
MASK_VALUE = -1e30
MIB = 1024 * 1024

_NB = N_BUCKETS // 2
_MAX_EXACT = _NB // 2
_BUCKET_THRESHOLDS = tuple(
    int(math.ceil(_MAX_EXACT * (MAX_DIST / _MAX_EXACT) ** (k / (_NB - _MAX_EXACT)) - 1e-9))
    for k in range(1, _NB - _MAX_EXACT))
_FAR_BUCKET = _NB - 1
_FAR_DIST = _BUCKET_THRESHOLDS[-1]


def _dot(a, b):
    return jnp.dot(a, b, preferred_element_type=F32)


def _dot_nt(a, b):
    return lax.dot_general(a, b, (((1,), (1,)), ((), ())), preferred_element_type=F32)


def _rms_rows(x, w):
    ms = jnp.mean(x * x, axis=-1, keepdims=True)
    return x * lax.rsqrt(ms + EPS) * w


def _gelu_tanh(x):
    c = math.sqrt(2.0 / math.pi)
    return x * (0.5 * (1.0 + jnp.tanh(c * (x + 0.044715 * (x * x * x)))))


def _sigmoid(x):
    return 1.0 / (1.0 + jnp.exp(-x))


def _params(sem, vmem_mib):
    return pltpu.CompilerParams(dimension_semantics=sem, vmem_limit_bytes=vmem_mib * MIB)


def _norm_matmul_kernel(x_ref, nw_ref, w_ref, *refs, segs, act, tm, row_chunk):
    out_refs = refs[:len(segs)]
    xn_ref = refs[len(segs)]
    n = pl.program_id(1)

    @pl.when(n == 0)
    def _():
        def body(c, carry):
            r = pl.multiple_of(c * row_chunk, row_chunk)
            x = x_ref[pl.ds(r, row_chunk), :]
            xn_ref[pl.ds(r, row_chunk), :] = _rms_rows(x, nw_ref[...]).astype(BF16)
            return carry
        lax.fori_loop(0, tm // row_chunk, body, 0)

    acc = _dot(xn_ref[...], w_ref[...])
    if act == "gelu":
        acc = _gelu_tanh(acc)
    for (lo, hi, scale), o_ref in zip(segs, out_refs):
        def store(o_ref=o_ref, scale=scale):
            val = acc if scale == 1.0 else acc * scale
            o_ref[...] = val.astype(o_ref.dtype)
        if len(segs) == 1:
            store()
        else:
            pl.when((n >= lo) & (n < hi))(store)


def _norm_matmul(x, nw, w, seg_defs, *, act=None, tm, tn=512):
    m, d = x.shape
    n_total = w.shape[1]
    assert m % tm == 0 and n_total % tn == 0
    segs, out_shapes, out_specs = [], [], []
    lo = 0
    for cols, dtype, scale in seg_defs:
        assert cols % tn == 0
        nb = cols // tn
        segs.append((lo, lo + nb, scale))
        out_shapes.append(jax.ShapeDtypeStruct((m, cols), dtype))
        out_specs.append(pl.BlockSpec(
            (tm, tn), lambda i, j, lo=lo, nb=nb: (i, jnp.clip(j - lo, 0, nb - 1))))
        lo += nb
    kern = functools.partial(_norm_matmul_kernel, segs=tuple(segs), act=act, tm=tm,
                             row_chunk=min(tm, 64))
    return pl.pallas_call(
        kern,
        out_shape=out_shapes,
        grid=(m // tm, n_total // tn),
        in_specs=[pl.BlockSpec((tm, d), lambda i, j: (i, 0)),
                  pl.BlockSpec((1, d), lambda i, j: (0, 0)),
                  pl.BlockSpec((d, tn), lambda i, j: (0, j))],
        out_specs=out_specs,
        scratch_shapes=[pltpu.VMEM((tm, d), BF16)],
        compiler_params=_params(("arbitrary", "arbitrary"), 52),
        name="norm_matmul",
    )(x, nw.reshape(1, d), w)


def _matmul_res_kernel(*refs, n_in):
    a_refs = refs[:n_in]
    w_refs = refs[n_in:2 * n_in]
    res_ref = refs[2 * n_in]
    o_ref = refs[2 * n_in + 1]
    acc = res_ref[...]
    for a_ref, w_ref in zip(a_refs, w_refs):
        acc = acc + _dot(a_ref[...], w_ref[...])
    o_ref[...] = acc


def _matmul_res(a_list, w, res, *, tm, tn=1024):
    m, n_total = res.shape
    n_in = len(a_list)
    kb = w.shape[0] // n_in
    assert all(a.shape == (m, kb) for a in a_list) and m % tm == 0 and n_total % tn == 0
    in_specs = [pl.BlockSpec((tm, kb), lambda i, j: (i, 0)) for _ in a_list]
    in_specs += [pl.BlockSpec((kb, tn), lambda i, j, b=b: (b, j)) for b in range(n_in)]
    in_specs += [pl.BlockSpec((tm, tn), lambda i, j: (i, j))]
    return pl.pallas_call(
        functools.partial(_matmul_res_kernel, n_in=n_in),
        out_shape=jax.ShapeDtypeStruct((m, n_total), F32),
        grid=(m // tm, n_total // tn),
        in_specs=in_specs,
        out_specs=pl.BlockSpec((tm, tn), lambda i, j: (i, j)),
        compiler_params=_params(("arbitrary", "arbitrary"), 48),
        name="matmul_res",
    )(*a_list, *([w] * n_in), res)


def _ffn_kernel(x_ref, nw_ref, wu_ref, wd_ref, fw_ref, o_ref, xn_ref, *, tm, row_chunk, col_chunk,
                final):
    f = pl.program_id(1)

    @pl.when(f == 0)
    def _():
        def body(c, carry):
            r = pl.multiple_of(c * row_chunk, row_chunk)
            x = x_ref[pl.ds(r, row_chunk), :]
            xn_ref[pl.ds(r, row_chunk), :] = _rms_rows(x, nw_ref[...]).astype(BF16)
            o_ref[pl.ds(r, row_chunk), :] = x
            return carry
        lax.fori_loop(0, tm // row_chunk, body, 0)

    h = _dot(xn_ref[...], wu_ref[...])
    h = jnp.maximum(h, 0.0)
    h = (h * h).astype(BF16)
    d = o_ref.shape[1]
    for c0 in range(0, d, col_chunk):
        o_ref[:, c0:c0 + col_chunk] += _dot(h, wd_ref[:, c0:c0 + col_chunk])

    if final:
        @pl.when(f == pl.num_programs(1) - 1)
        def _():
            def body(c, carry):
                r = pl.multiple_of(c * row_chunk, row_chunk)
                o_ref[pl.ds(r, row_chunk), :] = _rms_rows(o_ref[pl.ds(r, row_chunk), :], fw_ref[...])
                return carry
            lax.fori_loop(0, tm // row_chunk, body, 0)


def _ffn(x, nw, wu, wd, fw, *, tm, tf=512, final=False):
    m, d = x.shape
    dff = wu.shape[1]
    assert m % tm == 0 and dff % tf == 0
    kern = functools.partial(_ffn_kernel, tm=tm, row_chunk=min(tm, 64), col_chunk=512, final=final)
    return pl.pallas_call(
        kern,
        out_shape=jax.ShapeDtypeStruct((m, d), F32),
        grid=(m // tm, dff // tf),
        in_specs=[pl.BlockSpec((tm, d), lambda i, f: (i, 0)),
                  pl.BlockSpec((1, d), lambda i, f: (0, 0)),
                  pl.BlockSpec((d, tf), lambda i, f: (0, f)),
                  pl.BlockSpec((tf, d), lambda i, f: (f, 0)),
                  pl.BlockSpec((1, d), lambda i, f: (0, 0))],
        out_specs=pl.BlockSpec((tm, d), lambda i, f: (i, 0)),
        scratch_shapes=[pltpu.VMEM((tm, d), BF16)],
        compiler_params=_params(("arbitrary", "arbitrary"), 56),
        name="ffn",
    )(x, nw.reshape(1, d), wu, wd, fw.reshape(1, d))


_XP_PAD = 8


def _scan_group(a, u, hb, row):
    for dist in (1, 2, 4):
        keep = row >= dist
        a_s = jnp.where(keep, pltpu.roll(a, dist, 0), 1.0)
        u_s = jnp.where(keep, pltpu.roll(u, dist, 0), 0.0)
        u = a * u_s + u
        a = a * a_s
    return a * hb + u


def _mixer_a_kernel(g_ref, xr_ref, cprev_ref, hprev_ref, cw_ref, cb_ref, wa_ref, ba_ref,
                    wx_ref, bx_ref, lam_ref, ya_ref, conv_ref, hlast_ref, xp_ref, h_ref,
                    *, tt, rc):
    t = pl.program_id(1)
    hist = CONV_W - 1

    @pl.when(t == 0)
    def _():
        xp_ref[_XP_PAD - hist:_XP_PAD, :] = cprev_ref[0]
        h_ref[...] = hprev_ref[0]

    @pl.when(t > 0)
    def _():
        xp_ref[_XP_PAD - hist:_XP_PAD, :] = xp_ref[_XP_PAD + tt - hist:_XP_PAD + tt, :]

    xp_ref[_XP_PAD:_XP_PAD + tt, :] = xr_ref[...]

    z = -lam_ref[...]
    softplus = jnp.maximum(z, 0.0) + jnp.log1p(jnp.exp(-jnp.abs(z)))
    row = lax.broadcasted_iota(jnp.int32, (8, D_RNN), 0)
    nblk = D_RNN // 256
    hb = jnp.broadcast_to(h_ref[...], (8, D_RNN))

    for c in range(tt // rc):
        r0 = c * rc
        xc = cb_ref[...]
        for j in range(CONV_W):
            s = _XP_PAD - hist + j + r0
            xc = xc + xp_ref[s:s + rc, :] * cw_ref[j:j + 1, :]
        xcb = xc.astype(BF16)
        r_lin = jnp.concatenate(
            [_dot(xcb[:, 256 * b:256 * (b + 1)], wa_ref[b]) for b in range(nblk)], axis=1)
        i_lin = jnp.concatenate(
            [_dot(xcb[:, 256 * b:256 * (b + 1)], wx_ref[b]) for b in range(nblk)], axis=1)
        r = _sigmoid(r_lin + ba_ref[...])
        i = _sigmoid(i_lin + bx_ref[...])
        log_a = (-LRU_C) * r * softplus
        a = jnp.exp(log_a)
        th = jnp.tanh(log_a)
        u = jnp.sqrt((-2.0 * th) / (1.0 - th)) * (i * xc)
        hs = []
        for k in range(rc // 8):
            hrows = _scan_group(a[8 * k:8 * k + 8], u[8 * k:8 * k + 8], hb, row)
            hb = jnp.broadcast_to(hrows[7:8, :], (8, D_RNN))
            hs.append(hrows)
        hs = jnp.concatenate(hs, axis=0)
        g = g_ref[r0:r0 + rc, :]
        ya_ref[r0:r0 + rc, :] = (_gelu_tanh(g) * hs).astype(ya_ref.dtype)

    h_ref[...] = hb[0:1, :]
    conv_ref[0] = xr_ref[tt - hist:tt, :]
    hlast_ref[0] = hb[0:1, :]


def _mixer_a(gx, conv_prev, h_prev, p, *, batch, seq, tt):
    assert seq % tt == 0 and tt % 8 == 0
    rc = min(tt, 64)
    nt = seq // tt
    kern = functools.partial(_mixer_a_kernel, tt=tt, rc=rc)
    row_blk = lambda b, t: (b * nt + t, 0)
    vec = lambda b, t: (0, 0)
    return pl.pallas_call(
        kern,
        out_shape=[jax.ShapeDtypeStruct((batch * seq, D_RNN), BF16),
                   jax.ShapeDtypeStruct((batch, CONV_W - 1, D_RNN), F32),
                   jax.ShapeDtypeStruct((batch, 1, D_RNN), F32)],
        grid=(batch, nt),
        in_specs=[pl.BlockSpec((tt, D_RNN), row_blk),
                  pl.BlockSpec((tt, D_RNN), lambda b, t: (b * nt + t, 1)),
                  pl.BlockSpec((1, CONV_W - 1, D_RNN), lambda b, t: (b, 0, 0)),
                  pl.BlockSpec((1, 1, D_RNN), lambda b, t: (b, 0, 0)),
                  pl.BlockSpec((CONV_W, D_RNN), vec),
                  pl.BlockSpec((1, D_RNN), vec),
                  pl.BlockSpec((D_RNN // 256, 256, 256), lambda b, t: (0, 0, 0)),
                  pl.BlockSpec((1, D_RNN), vec),
                  pl.BlockSpec((D_RNN // 256, 256, 256), lambda b, t: (0, 0, 0)),
                  pl.BlockSpec((1, D_RNN), vec),
                  pl.BlockSpec((1, D_RNN), vec)],
        out_specs=[pl.BlockSpec((tt, D_RNN), row_blk),
                   pl.BlockSpec((1, CONV_W - 1, D_RNN), lambda b, t: (b, 0, 0)),
                   pl.BlockSpec((1, 1, D_RNN), lambda b, t: (b, 0, 0))],
        scratch_shapes=[pltpu.VMEM((_XP_PAD + tt, D_RNN), F32),
                        pltpu.VMEM((1, D_RNN), F32)],
        compiler_params=_params(("arbitrary", "arbitrary"), 40),
        name="mixer_a",
    )(gx, gx, conv_prev, h_prev.reshape(batch, 1, D_RNN), p["conv_w"], p["conv_b"],
      p["wa_bd"], p["ba"], p["wx_bd"], p["bx"], p["lam"])


def _rel_bias(d, tab_ref, h):
    n = jnp.abs(d)
    large = jnp.full(d.shape, _MAX_EXACT, jnp.int32)
    for thr in _BUCKET_THRESHOLDS:
        large = large + (n >= thr).astype(jnp.int32)
    bucket = jnp.where(d > 0, _NB, 0) + jnp.where(n < _MAX_EXACT, n, large)
    val = jnp.full(d.shape, tab_ref[0, h], F32)
    for b in range(1, N_BUCKETS):
        val = jnp.where(bucket == b, tab_ref[b, h], val)
    return val


def _split_q(q):
    lane = lax.broadcasted_iota(jnp.int32, q.shape, 1)
    zero = jnp.zeros_like(q)
    return jnp.where(lane < B_DH, q, zero), jnp.where(lane >= B_DH, q, zero)


def _diff_lambda(lamv_ref, lam_init):
    lv = lamv_ref[...]
    s01 = jnp.sum(lv[0:1, :] * lv[1:2, :], axis=-1, keepdims=True)
    s23 = jnp.sum(lv[2:3, :] * lv[3:4, :], axis=-1, keepdims=True)
    return jnp.exp(s01) - jnp.exp(s23) + lam_init


def _diff_finish(o1, l1, o2, l2, lam, sub, lam_init):
    o = o1 / l1 - lam * (o2 / l2)
    return _rms_rows(o, sub) * (1.0 - lam_init)


def _attn_prompt_kernel(tab_ref, lamv_ref, sub_ref, q_ref, k_ref, v_ref, o_ref,
                        b0_ref, b1_ref, m1_ref, l1_ref, m2_ref, l2_ref, acc1_ref, acc2_ref,
                        *, tq, lam_init):
    h = pl.program_id(0)
    qi = pl.program_id(1)

    @pl.when(qi == 0)
    def _():
        ri = lax.broadcasted_iota(jnp.int32, (tq, tq), 0)
        ci = lax.broadcasted_iota(jnp.int32, (tq, tq), 1)
        d = ci - ri
        visible = (ci // CHUNK) <= (ri // CHUNK)
        b0_ref[...] = jnp.where(visible, _rel_bias(d, tab_ref, h), MASK_VALUE)
        b1_ref[...] = _rel_bias(d - tq, tab_ref, h)

    q1, q2 = _split_q(q_ref[...])
    m1_ref[...] = jnp.full(m1_ref.shape, MASK_VALUE, F32)
    m2_ref[...] = jnp.full(m2_ref.shape, MASK_VALUE, F32)
    l1_ref[...] = jnp.zeros(l1_ref.shape, F32)
    l2_ref[...] = jnp.zeros(l2_ref.shape, F32)
    acc1_ref[...] = jnp.zeros(acc1_ref.shape, F32)
    acc2_ref[...] = jnp.zeros(acc2_ref.shape, F32)

    def branch(qz, s_bias, kb, vb, m_ref, l_ref, acc_ref):
        s = _dot_nt(qz, kb) + s_bias
        m_old = m_ref[...]
        m_new = jnp.maximum(m_old, jnp.max(s, axis=-1, keepdims=True))
        alpha = jnp.exp(m_old - m_new)
        e = jnp.exp(s - m_new)
        l_ref[...] = alpha * l_ref[...] + jnp.sum(e, axis=-1, keepdims=True)
        acc_ref[...] = alpha * acc_ref[...] + _dot(e.astype(BF16), vb)
        m_ref[...] = m_new

    def step(ki, s_bias):
        r = pl.multiple_of(ki * tq, tq)
        kb = k_ref[pl.ds(r, tq), :].astype(BF16)
        vb = v_ref[pl.ds(r, tq), :].astype(BF16)
        branch(q1, s_bias, kb, vb, m1_ref, l1_ref, acc1_ref)
        branch(q2, s_bias, kb, vb, m2_ref, l2_ref, acc2_ref)

    far_bias = tab_ref[_FAR_BUCKET, h]

    def far_body(ki, carry):
        step(ki, far_bias)
        return carry
    lax.fori_loop(0, jnp.maximum(qi - 1, 0), far_body, 0)

    @pl.when(qi >= 1)
    def _():
        step(qi - 1, b1_ref[...])

    step(qi, b0_ref[...])

    lam = _diff_lambda(lamv_ref, lam_init)
    o = _diff_finish(acc1_ref[...], l1_ref[...], acc2_ref[...], l2_ref[...], lam, sub_ref[...],
                     lam_init)
    o_ref[...] = o.astype(o_ref.dtype)


def _attn_prompt(q, k, v, p, lam_init, *, tq=512):
    s = q.shape[0]
    assert s % tq == 0 and tq % CHUNK == 0 and tq + 1 >= _FAR_DIST
    kern = functools.partial(_attn_prompt_kernel, tq=tq, lam_init=lam_init)
    stat = pltpu.VMEM((tq, 1), F32)
    return pl.pallas_call(
        kern,
        out_shape=jax.ShapeDtypeStruct((s, B_W), BF16),
        grid=(B_HEADS, s // tq),
        in_specs=[pl.BlockSpec(memory_space=pltpu.SMEM),
                  pl.BlockSpec((4, B_DH), lambda h, i: (0, 0)),
                  pl.BlockSpec((1, B_E), lambda h, i: (0, 0)),
                  pl.BlockSpec((tq, B_E), lambda h, i: (i, h)),
                  pl.BlockSpec((s, B_E), lambda h, i: (0, h)),
                  pl.BlockSpec((s, B_E), lambda h, i: (0, h))],
        out_specs=pl.BlockSpec((tq, B_E), lambda h, i: (i, h)),
        scratch_shapes=[pltpu.VMEM((tq, tq), F32), pltpu.VMEM((tq, tq), F32),
                        stat, stat, stat, stat,
                        pltpu.VMEM((tq, B_E), F32), pltpu.VMEM((tq, B_E), F32)],
        compiler_params=_params(("arbitrary", "arbitrary"), 48),
        name="attn_prompt",
    )(p["bias_table"], p["lam_vecs"], p["subln_w"], q, k, v)


def _attn_sample_kernel(tab_ref, lamv_ref, sub_ref, q_ref, kn_ref, vn_ref, kc_ref, vc_ref, o_ref,
                        *, t_new, past, lam_init):
    h = pl.program_id(1)
    q1, q2 = _split_q(q_ref[...])
    kc = kc_ref[...].astype(BF16)
    vc = vc_ref[...].astype(BF16)
    kn = kn_ref[...].astype(BF16)
    vn = vn_ref[...].astype(BF16)

    qpos = past + lax.broadcasted_iota(jnp.int32, (t_new, past), 0)
    bias_c = _rel_bias(lax.broadcasted_iota(jnp.int32, (t_new, past), 1) - qpos, tab_ref, h)
    bias_n = _rel_bias(lax.broadcasted_iota(jnp.int32, (t_new, t_new), 1)
                       - lax.broadcasted_iota(jnp.int32, (t_new, t_new), 0), tab_ref, h)

    def branch(qz):
        sc = _dot_nt(qz, kc) + bias_c
        sn = _dot_nt(qz, kn) + bias_n
        m = jnp.maximum(jnp.max(sc, axis=-1, keepdims=True), jnp.max(sn, axis=-1, keepdims=True))
        ec = jnp.exp(sc - m)
        en = jnp.exp(sn - m)
        l = jnp.sum(ec, axis=-1, keepdims=True) + jnp.sum(en, axis=-1, keepdims=True)
        return _dot(ec.astype(BF16), vc) + _dot(en.astype(BF16), vn), l

    o1, l1 = branch(q1)
    o2, l2 = branch(q2)
    lam = _diff_lambda(lamv_ref, lam_init)
    o_ref[...] = _diff_finish(o1, l1, o2, l2, lam, sub_ref[...], lam_init).astype(o_ref.dtype)


def _attn_sample(q, k_new, v_new, k_cache, v_cache, p, lam_init, *, batch, t_new):
    past = k_cache.shape[1]
    kern = functools.partial(_attn_sample_kernel, t_new=t_new, past=past, lam_init=lam_init)
    new_blk = pl.BlockSpec((t_new, B_E), lambda b, h: (b, h))
    cache_blk = pl.BlockSpec((None, past, B_E), lambda b, h: (b, 0, h))
    return pl.pallas_call(
        kern,
        out_shape=jax.ShapeDtypeStruct((batch * t_new, B_W), BF16),
        grid=(batch, B_HEADS),
        in_specs=[pl.BlockSpec(memory_space=pltpu.SMEM),
                  pl.BlockSpec((4, B_DH), lambda b, h: (0, 0)),
                  pl.BlockSpec((1, B_E), lambda b, h: (0, 0)),
                  new_blk, new_blk, new_blk, cache_blk, cache_blk],
        out_specs=new_blk,
        compiler_params=_params(("arbitrary", "arbitrary"), 32),
        name="attn_sample",
    )(p["bias_table"], p["lam_vecs"], p["subln_w"], q, k_new, v_new,
      k_cache.reshape(batch, past, B_W), v_cache.reshape(batch, past, B_W))


def _gating_kernel(u_ref, v_ref, vnw_ref, ws_ref, bs_ref, *refs, chunk, n_chunks, emit_v):
    if emit_v:
        o_ref, vout_ref, wsm_ref = refs
    else:
        o_ref, wsm_ref = refs
        vout_ref = None

    @pl.when(pl.program_id(0) == 0)
    def _():
        ri = lax.broadcasted_iota(jnp.int32, (chunk, chunk), 0)
        ci = lax.broadcasted_iota(jnp.int32, (chunk, chunk), 1)
        for g in range(C_GROUPS):
            wsm_ref[g] = jnp.where(ci <= ri, ws_ref[g], 0.0).astype(BF16)

    for c in range(n_chunks):
        r0 = c * chunk
        vn = _rms_rows(v_ref[r0:r0 + chunk, :], vnw_ref[...])
        if emit_v:
            vout_ref[r0:r0 + chunk, :] = vn
        vnb = vn.astype(BF16)
        for g in range(C_GROUPS):
            cols = slice(g * C_GW, (g + 1) * C_GW)
            mix = _dot(wsm_ref[g], vnb[:, cols]) + bs_ref[:, g:g + 1]
            o_ref[r0:r0 + chunk, cols] = (u_ref[r0:r0 + chunk, cols] * mix).astype(o_ref.dtype)


def _gating(z, vnw, ws, bs_t, *, chunk, n_chunks, emit_v):
    m = z.shape[0]
    rt = chunk * n_chunks
    assert m % rt == 0
    kern = functools.partial(_gating_kernel, chunk=chunk, n_chunks=n_chunks, emit_v=emit_v)
    out_shape = [jax.ShapeDtypeStruct((m, D_C), BF16)]
    out_specs = [pl.BlockSpec((rt, D_C), lambda i: (i, 0))]
    if emit_v:
        out_shape.append(jax.ShapeDtypeStruct((m, D_C), F32))
        out_specs.append(pl.BlockSpec((rt, D_C), lambda i: (i, 0)))
    res = pl.pallas_call(
        kern,
        out_shape=out_shape,
        grid=(m // rt,),
        in_specs=[pl.BlockSpec((rt, D_C), lambda i: (i, 0)),
                  pl.BlockSpec((rt, D_C), lambda i: (i, 1)),
                  pl.BlockSpec((1, D_C), lambda i: (0, 0)),
                  pl.BlockSpec((C_GROUPS, chunk, chunk), lambda i: (0, 0, 0)),
                  pl.BlockSpec((chunk, C_GROUPS), lambda i: (0, 0))],
        out_specs=out_specs,
        scratch_shapes=[pltpu.VMEM((C_GROUPS, chunk, chunk), BF16)],
        compiler_params=_params(("arbitrary",), 40),
        name="gating",
    )(z, z, vnw.reshape(1, D_C), ws, bs_t)
    return res if emit_v else (res[0], None)


def _block_diag(w):
    per = 256 // LRU_BW
    w4 = w.reshape(LRU_BLOCKS // per, per, LRU_BW, LRU_BW)
    eye = jnp.eye(per, dtype=w.dtype)
    bd = jnp.einsum("gaij,ab->gaibj", w4, eye)
    return bd.reshape(LRU_BLOCKS // per, 256, 256).astype(BF16)


def kernel(x_prompt, x_sample, state_conv, state_lru, cache_k, cache_v, w_even_in, w_even_out,
           conv_w, conv_b, lru_wa, lru_ba, lru_wx, lru_bx, lru_lambda, lam_vecs, subln_w,
           rel_bias_table, w_odd_in, w_odd_out, gmlp_vnorm_w, gmlp_ws, gmlp_bs, norm_mix_w,
           norm_ffn_w, norm_final_w, w_ff_up, w_ff_down):
    batch_p, seq_p, _ = x_prompt.shape
    batch_s, seq_s, _ = x_sample.shape
    assert batch_p == 1
    yp = x_prompt.reshape(batch_p * seq_p, D_MODEL)
    ys = x_sample.reshape(batch_s * seq_s, D_MODEL)
    tm_p = 1024
    tm_s = batch_s * seq_s
    scale = B_DH ** -0.5

    w_even_in_b = w_even_in.astype(BF16)
    w_even_out_b = w_even_out.astype(BF16)
    w_odd_in_b = w_odd_in.astype(BF16)
    w_odd_out_b = w_odd_out.astype(BF16)
    w_up_b = w_ff_up.astype(BF16)
    w_down_b = w_ff_down.astype(BF16)

    even_segs = [(2 * D_RNN, F32, 1.0), (B_W, BF16, scale), (B_W, F32, 1.0), (B_W, F32, 1.0)]
    zeros_conv = jnp.zeros((batch_p, CONV_W - 1, D_RNN), F32)
    zeros_h = jnp.zeros((batch_p, D_RNN), F32)

    p_conv, p_lru, p_k, p_v = [], [], [], []
    s_conv, s_lru, s_k, s_v, s_gv = [], [], [], [], []
    for l in range(DEPTH):
        if l % 2 == 0:
            e = l // 2
            lam_init = 0.8 - 0.6 * math.exp(-0.3 * l)
            pa = dict(conv_w=conv_w[e], conv_b=conv_b[e].reshape(1, D_RNN),
                      wa_bd=_block_diag(lru_wa[e]), ba=lru_ba[e].reshape(1, D_RNN),
                      wx_bd=_block_diag(lru_wx[e]), bx=lru_bx[e].reshape(1, D_RNN),
                      lam=lru_lambda[e].reshape(1, D_RNN))
            pb = dict(bias_table=rel_bias_table, lam_vecs=lam_vecs[e],
                      subln_w=subln_w[e].reshape(1, B_E))
            gx, q, k, v = _norm_matmul(yp, norm_mix_w[l], w_even_in_b[e], even_segs, tm=tm_p)
            ya, c_new, h_new = _mixer_a(gx, zeros_conv, zeros_h, pa, batch=batch_p, seq=seq_p,
                                        tt=256)
            yb = _attn_prompt(q, k, v, pb, lam_init)
            yp = _matmul_res([ya, yb], w_even_out_b[e], yp, tm=tm_p)
            p_conv.append(c_new)
            p_lru.append(h_new.reshape(batch_p, D_RNN))
            p_k.append(k.reshape(batch_p, seq_p, B_HEADS, B_E))
            p_v.append(v.reshape(batch_p, seq_p, B_HEADS, B_E))
            gx, q, k, v = _norm_matmul(ys, norm_mix_w[l], w_even_in_b[e], even_segs, tm=tm_s)
            ya, c_new, h_new = _mixer_a(gx, state_conv[e], state_lru[e], pa, batch=batch_s,
                                        seq=seq_s, tt=seq_s)
            yb = _attn_sample(q, k, v, cache_k[e], cache_v[e], pb, lam_init, batch=batch_s,
                              t_new=seq_s)
            ys = _matmul_res([ya, yb], w_even_out_b[e], ys, tm=tm_s)
            s_conv.append(c_new)
            s_lru.append(h_new.reshape(batch_s, D_RNN))
            s_k.append(k.reshape(batch_s, seq_s, B_HEADS, B_E))
            s_v.append(v.reshape(batch_s, seq_s, B_HEADS, B_E))
        else:
            o = l // 2
            (z,) = _norm_matmul(yp, norm_mix_w[l], w_odd_in_b[o], [(2 * D_C, F32, 1.0)],
                                act="gelu", tm=tm_p)
            gated, _ = _gating(z, gmlp_vnorm_w[o], gmlp_ws[o], gmlp_bs[o].T, chunk=GMLP_CHUNK,
                               n_chunks=4, emit_v=False)
            yp = _matmul_res([gated], w_odd_out_b[o], yp, tm=tm_p)
            (z,) = _norm_matmul(ys, norm_mix_w[l], w_odd_in_b[o], [(2 * D_C, F32, 1.0)],
                                act="gelu", tm=tm_s)
            gated, gv = _gating(z, gmlp_vnorm_w[o], gmlp_ws[o][:, :seq_s, :seq_s],
                                gmlp_bs[o][:, :seq_s].T, chunk=seq_s, n_chunks=1, emit_v=True)
            ys = _matmul_res([gated], w_odd_out_b[o], ys, tm=tm_s)
            s_gv.append(gv.reshape(batch_s, seq_s, D_C))
        last = l == DEPTH - 1
        yp = _ffn(yp, norm_ffn_w[l], w_up_b[l], w_down_b[l], norm_final_w, tm=tm_p, final=last)
        ys = _ffn(ys, norm_ffn_w[l], w_up_b[l], w_down_b[l], norm_final_w, tm=tm_s, final=last)

    return (yp.reshape(batch_p, seq_p, D_MODEL), ys.reshape(batch_s, seq_s, D_MODEL),
            jnp.stack(p_conv), jnp.stack(p_lru), jnp.stack(p_k), jnp.stack(p_v),
            jnp.stack(s_conv), jnp.stack(s_lru), jnp.stack(s_k), jnp.stack(s_v),
            jnp.stack(s_gv))
```

```python
import functools
import math

import jax
import jax.numpy as jnp
from jax import lax
from jax.experimental import pallas as pl
from jax.experimental.pallas import tpu as pltpu

D_MODEL = 2048
DEPTH = 4
CHUNK = 64
EPS = 1e-6
D_RNN = D_MODEL // 2
LRU_BLOCKS = 16
LRU_BW = D_RNN // LRU_BLOCKS
CONV_W = 4
LRU_C = 8.0
B_HEADS = 8
B_DH = 64
B_E = 2 * B_DH
B_W = B_HEADS * B_E
N_BUCKETS = 32
MAX_DIST = 512
GMLP_CHUNK = 128
D_C = D_MODEL
C_GROUPS = 16
C_GW = D_C // C_GROUPS
D_FF = 4 * D_MODEL

F32 = jnp.float32
BF16 = jnp.bfloat16
MASK_VALUE = -1e30
MIB = 1024 * 1024

_NB = N_BUCKETS // 2
_MAX_EXACT = _NB // 2
_BUCKET_THRESHOLDS = tuple(
    int(math.ceil(_MAX_EXACT * (MAX_DIST / _MAX_EXACT) ** (k / (_NB - _MAX_EXACT)) - 1e-9))
    for k in range(1, _NB - _MAX_EXACT))
_FAR_BUCKET = _NB - 1
_FAR_DIST = _BUCKET_THRESHOLDS[-1]


def _dot(a, b):
    return jnp.dot(a, b, preferred_element_type=F32)


def _dot_nt(a, b):
    return lax.dot_general(a, b, (((1,), (1,)), ((), ())), preferred_element_type=F32)


def _rms_rows(x, w):
    ms = jnp.mean(x * x, axis=-1, keepdims=True)
    return x * lax.rsqrt(ms + EPS) * w


def _gelu_tanh(x):
    c = math.sqrt(2.0 / math.pi)
    return x * (0.5 * (1.0 + jnp.tanh(c * (x + 0.044715 * (x * x * x)))))


def _sigmoid(x):
    return 1.0 / (1.0 + jnp.exp(-x))


def _params(sem, vmem_mib):
    return pltpu.CompilerParams(dimension_semantics=sem, vmem_limit_bytes=vmem_mib * MIB)


def _norm_matmul_kernel(x_ref, nw_ref, w_ref, *refs, segs, act, tm, row_chunk):
    out_refs = refs[:len(segs)]
    xn_ref = refs[len(segs)]
    n = pl.program_id(1)

    @pl.when(n == 0)
    def _():
        def body(c, carry):
            r = pl.multiple_of(c * row_chunk, row_chunk)
            x = x_ref[pl.ds(r, row_chunk), :]
            xn_ref[pl.ds(r, row_chunk), :] = _rms_rows(x, nw_ref[...]).astype(BF16)
            return carry
        lax.fori_loop(0, tm // row_chunk, body, 0)

    acc = _dot(xn_ref[...], w_ref[...])
    if act == "gelu":
        acc = _gelu_tanh(acc)
    for (lo, hi, scale), o_ref in zip(segs, out_refs):
        def store(o_ref=o_ref, scale=scale):
            val = acc if scale == 1.0 else acc * scale
            o_ref[...] = val.astype(o_ref.dtype)
        if len(segs) == 1:
            store()
        else:
            pl.when((n >= lo) & (n < hi))(store)


def _norm_matmul(x, nw, w, seg_defs, *, act=None, tm, tn=512):
    m, d = x.shape
    n_total = w.shape[1]
    assert m % tm == 0 and n_total % tn == 0
    segs, out_shapes, out_specs = [], [], []
    for first, cols, dtype, scale in seg_defs:
        assert cols % tn == 0 and first % tn == 0
        lo, nb = first // tn, cols // tn
        segs.append((lo, lo + nb, scale))
        out_shapes.append(jax.ShapeDtypeStruct((m, cols), dtype))
        out_specs.append(pl.BlockSpec(
            (tm, tn), lambda i, j, lo=lo, nb=nb: (i, jnp.clip(j - lo, 0, nb - 1))))
    kern = functools.partial(_norm_matmul_kernel, segs=tuple(segs), act=act, tm=tm,
                             row_chunk=min(tm, 64))
    return pl.pallas_call(
        kern,
        out_shape=out_shapes,
        grid=(m // tm, n_total // tn),
        in_specs=[pl.BlockSpec((tm, d), lambda i, j: (i, 0)),
                  pl.BlockSpec((1, d), lambda i, j: (0, 0)),
                  pl.BlockSpec((d, tn), lambda i, j: (0, j))],
        out_specs=out_specs,
        scratch_shapes=[pltpu.VMEM((tm, d), BF16)],
        compiler_params=_params(("arbitrary", "arbitrary"), 52),
        name="norm_matmul",
    )(x, nw.reshape(1, d), w)


def _matmul_res_kernel(*refs, n_in):
    a_refs = refs[:n_in]
    w_refs = refs[n_in:2 * n_in]
    res_ref = refs[2 * n_in]
    o_ref = refs[2 * n_in + 1]
    acc = res_ref[...]
    for a_ref, w_ref in zip(a_refs, w_refs):
        acc = acc + _dot(a_ref[...], w_ref[...])
    o_ref[...] = acc


def _matmul_res(a_list, w, res, *, tm, tn=1024):
    m, n_total = res.shape
    n_in = len(a_list)
    kb = w.shape[0] // n_in
    assert all(a.shape == (m, kb) for a in a_list) and m % tm == 0 and n_total % tn == 0
    in_specs = [pl.BlockSpec((tm, kb), lambda i, j: (i, 0)) for _ in a_list]
    in_specs += [pl.BlockSpec((kb, tn), lambda i, j, b=b: (b, j)) for b in range(n_in)]
    in_specs += [pl.BlockSpec((tm, tn), lambda i, j: (i, j))]
    return pl.pallas_call(
        functools.partial(_matmul_res_kernel, n_in=n_in),
        out_shape=jax.ShapeDtypeStruct((m, n_total), F32),
        grid=(m // tm, n_total // tn),
        in_specs=in_specs,
        out_specs=pl.BlockSpec((tm, tn), lambda i, j: (i, j)),
        compiler_params=_params(("arbitrary", "arbitrary"), 48),
        name="matmul_res",
    )(*a_list, *([w] * n_in), res)


def _ffn_kernel(x_ref, nw_ref, wu_ref, wd_ref, fw_ref, o_ref, xn_ref, *, tm, row_chunk, col_chunk,
                final):
    f = pl.program_id(1)

    @pl.when(f == 0)
    def _():
        def body(c, carry):
            r = pl.multiple_of(c * row_chunk, row_chunk)
            x = x_ref[pl.ds(r, row_chunk), :]
            xn_ref[pl.ds(r, row_chunk), :] = _rms_rows(x, nw_ref[...]).astype(BF16)
            o_ref[pl.ds(r, row_chunk), :] = x
            return carry
        lax.fori_loop(0, tm // row_chunk, body, 0)

    h = _dot(xn_ref[...], wu_ref[...])
    h = jnp.maximum(h, 0.0)
    h = (h * h).astype(BF16)
    d = o_ref.shape[1]
    for c0 in range(0, d, col_chunk):
        o_ref[:, c0:c0 + col_chunk] += _dot(h, wd_ref[:, c0:c0 + col_chunk])

    if final:
        @pl.when(f == pl.num_programs(1) - 1)
        def _():
            def body(c, carry):
                r = pl.multiple_of(c * row_chunk, row_chunk)
                o_ref[pl.ds(r, row_chunk), :] = _rms_rows(o_ref[pl.ds(r, row_chunk), :], fw_ref[...])
                return carry
            lax.fori_loop(0, tm // row_chunk, body, 0)


def _ffn(x, nw, wu, wd, fw, *, tm, tf=512, final=False):
    m, d = x.shape
    dff = wu.shape[1]
    assert m % tm == 0 and dff % tf == 0
    kern = functools.partial(_ffn_kernel, tm=tm, row_chunk=min(tm, 64), col_chunk=512, final=final)
    return pl.pallas_call(
        kern,
        out_shape=jax.ShapeDtypeStruct((m, d), F32),
        grid=(m // tm, dff // tf),
        in_specs=[pl.BlockSpec((tm, d), lambda i, f: (i, 0)),
                  pl.BlockSpec((1, d), lambda i, f: (0, 0)),
                  pl.BlockSpec((d, tf), lambda i, f: (0, f)),
                  pl.BlockSpec((tf, d), lambda i, f: (f, 0)),
                  pl.BlockSpec((1, d), lambda i, f: (0, 0))],
        out_specs=pl.BlockSpec((tm, d), lambda i, f: (i, 0)),
        scratch_shapes=[pltpu.VMEM((tm, d), BF16)],
        compiler_params=_params(("arbitrary", "arbitrary"), 56),
        name="ffn",
    )(x, nw.reshape(1, d), wu, wd, fw.reshape(1, d))


_XP_PAD = 8


def _scan_group(a, u, hb, row):
    for dist in (1, 2, 4):
        keep = row >= dist
        a_s = jnp.where(keep, pltpu.roll(a, dist, 0), 1.0)
        u_s = jnp.where(keep, pltpu.roll(u, dist, 0), 0.0)
        u = a * u_s + u
        a = a * a_s
    return a * hb + u


def _mixer_a_kernel(g_ref, xr_ref, cprev_ref, hprev_ref, cw_ref, cb_ref, wa_ref, ba_ref,
                    wx_ref, bx_ref, lam_ref, ya_ref, conv_ref, hlast_ref, xp_ref, h_ref,
                    *, tt, rc):
    t = pl.program_id(1)
    hist = CONV_W - 1

    @pl.when(t == 0)
    def _():
        xp_ref[_XP_PAD - hist:_XP_PAD, :] = cprev_ref[0]
        h_ref[...] = hprev_ref[0]

    @pl.when(t > 0)
    def _():
        xp_ref[_XP_PAD - hist:_XP_PAD, :] = xp_ref[_XP_PAD + tt - hist:_XP_PAD + tt, :]

    xp_ref[_XP_PAD:_XP_PAD + tt, :] = xr_ref[...]

    z = -lam_ref[...]
    softplus = jnp.maximum(z, 0.0) + jnp.log1p(jnp.exp(-jnp.abs(z)))
    row = lax.broadcasted_iota(jnp.int32, (8, D_RNN), 0)
    nblk = D_RNN // 256
    hb = jnp.broadcast_to(h_ref[...], (8, D_RNN))

    for c in range(tt // rc):
        r0 = c * rc
        xc = cb_ref[...]
        for j in range(CONV_W):
            s = _XP_PAD - hist + j + r0
            xc = xc + xp_ref[s:s + rc, :] * cw_ref[j:j + 1, :]
        xcb = xc.astype(BF16)
        r_lin = jnp.concatenate(
            [_dot(xcb[:, 256 * b:256 * (b + 1)], wa_ref[b]) for b in range(nblk)], axis=1)
        i_lin = jnp.concatenate(
            [_dot(xcb[:, 256 * b:256 * (b + 1)], wx_ref[b]) for b in range(nblk)], axis=1)
        r = _sigmoid(r_lin + ba_ref[...])
        i = _sigmoid(i_lin + bx_ref[...])
        log_a = (-LRU_C) * r * softplus
        a = jnp.exp(log_a)
        th = jnp.tanh(log_a)
        u = jnp.sqrt((-2.0 * th) / (1.0 - th)) * (i * xc)
        hs = []
        for k in range(rc // 8):
            hrows = _scan_group(a[8 * k:8 * k + 8], u[8 * k:8 * k + 8], hb, row)
            hb = jnp.broadcast_to(hrows[7:8, :], (8, D_RNN))
            hs.append(hrows)
        hs = jnp.concatenate(hs, axis=0)
        g = g_ref[r0:r0 + rc, :]
        ya_ref[r0:r0 + rc, :] = (_gelu_tanh(g) * hs).astype(ya_ref.dtype)

    h_ref[...] = hb[0:1, :]
    conv_ref[0] = xr_ref[tt - hist:tt, :]
    hlast_ref[0] = hb[0:1, :]


def _mixer_a(gx, conv_prev, h_prev, p, *, batch, seq, tt):
    assert seq % tt == 0 and tt % 8 == 0
    rc = min(tt, 64)
    nt = seq // tt
    kern = functools.partial(_mixer_a_kernel, tt=tt, rc=rc)
    row_blk = lambda b, t: (b * nt + t, 0)
    vec = lambda b, t: (0, 0)
    return pl.pallas_call(
        kern,
        out_shape=[jax.ShapeDtypeStruct((batch * seq, D_RNN), BF16),
                   jax.ShapeDtypeStruct((batch, CONV_W - 1, D_RNN), F32),
                   jax.ShapeDtypeStruct((batch, 1, D_RNN), F32)],
        grid=(batch, nt),
        in_specs=[pl.BlockSpec((tt, D_RNN), row_blk),
                  pl.BlockSpec((tt, D_RNN), lambda b, t: (b * nt + t, 1)),
                  pl.BlockSpec((1, CONV_W - 1, D_RNN), lambda b, t: (b, 0, 0)),
                  pl.BlockSpec((1, 1, D_RNN), lambda b, t: (b, 0, 0)),
                  pl.BlockSpec((CONV_W, D_RNN), vec),
                  pl.BlockSpec((1, D_RNN), vec),
                  pl.BlockSpec((D_RNN // 256, 256, 256), lambda b, t: (0, 0, 0)),
                  pl.BlockSpec((1, D_RNN), vec),
                  pl.BlockSpec((D_RNN // 256, 256, 256), lambda b, t: (0, 0, 0)),
                  pl.BlockSpec((1, D_RNN), vec),
                  pl.BlockSpec((1, D_RNN), vec)],
        out_specs=[pl.BlockSpec((tt, D_RNN), row_blk),
                   pl.BlockSpec((1, CONV_W - 1, D_RNN), lambda b, t: (b, 0, 0)),
                   pl.BlockSpec((1, 1, D_RNN), lambda b, t: (b, 0, 0))],
        scratch_shapes=[pltpu.VMEM((_XP_PAD + tt, D_RNN), F32),
                        pltpu.VMEM((1, D_RNN), F32)],
        compiler_params=_params(("arbitrary", "arbitrary"), 40),
        name="mixer_a",
    )(gx, gx, conv_prev, h_prev.reshape(batch, 1, D_RNN), p["conv_w"], p["conv_b"],
      p["wa_bd"], p["ba"], p["wx_bd"], p["bx"], p["lam"])


def _rel_bias(d, tab_ref, h):
    n = jnp.abs(d)
    large = jnp.full(d.shape, _MAX_EXACT, jnp.int32)
    for thr in _BUCKET_THRESHOLDS:
        large = large + (n >= thr).astype(jnp.int32)
    bucket = jnp.where(d > 0, _NB, 0) + jnp.where(n < _MAX_EXACT, n, large)
    val = jnp.full(d.shape, tab_ref[0, h], F32)
    for b in range(1, N_BUCKETS):
        val = jnp.where(bucket == b, tab_ref[b, h], val)
    return val


def _split_q(q):
    lane = lax.broadcasted_iota(jnp.int32, q.shape, 1)
    zero = jnp.zeros_like(q)
    return jnp.where(lane < B_DH, q, zero), jnp.where(lane >= B_DH, q, zero)


def _diff_lambda(lamv_ref, lam_init):
    lv = lamv_ref[...]
    s01 = jnp.sum(lv[0:1, :] * lv[1:2, :], axis=-1, keepdims=True)
    s23 = jnp.sum(lv[2:3, :] * lv[3:4, :], axis=-1, keepdims=True)
    return jnp.exp(s01) - jnp.exp(s23) + lam_init


def _diff_finish(o1, l1, o2, l2, lam, sub, lam_init):
    o = o1 / l1 - lam * (o2 / l2)
    return _rms_rows(o, sub) * (1.0 - lam_init)


def _attn_prompt_kernel(tab_ref, lamv_ref, subc_ref, q_ref, k_ref, vt_ref, o_ref,
                        b0_ref, b1_ref, qz_ref, s_ref, m_ref, l_ref, acc_ref, *, tq, lam_init):
    h = pl.program_id(0)
    qi = pl.program_id(1)
    far_bias = tab_ref[_FAR_BUCKET, h]

    @pl.when(qi == 0)
    def _():
        ki_ = lax.broadcasted_iota(jnp.int32, (tq, tq), 0)
        qi_ = lax.broadcasted_iota(jnp.int32, (tq, tq), 1)
        d = ki_ - qi_
        visible = (ki_ // CHUNK) <= (qi_ // CHUNK)
        b0_ref[...] = jnp.where(visible, _rel_bias(d, tab_ref, h) - far_bias, MASK_VALUE)
        b1_ref[...] = _rel_bias(d - tq, tab_ref, h) - far_bias

    qt = q_ref[...].astype(F32).T
    row = lax.broadcasted_iota(jnp.int32, qt.shape, 0)
    qz_ref[:, 0:tq] = jnp.where(row < B_DH, qt, 0.0).astype(BF16)
    qz_ref[:, tq:2 * tq] = jnp.where(row >= B_DH, qt, 0.0).astype(BF16)
    m_ref[...] = jnp.full(m_ref.shape, MASK_VALUE, F32)
    l_ref[...] = jnp.zeros(l_ref.shape, F32)
    acc_ref[...] = jnp.zeros(acc_ref.shape, F32)

    def scores(ki, br):
        r = pl.multiple_of(ki * tq, tq)
        s_ref[br] = _dot(k_ref[pl.ds(r, tq), :], qz_ref[:, br * tq:(br + 1) * tq])

    def softmax_pv(ki, br, bias_ref):
        cols = slice(br * tq, (br + 1) * tq)
        r = pl.multiple_of(ki * tq, tq)
        s = s_ref[br]
        if bias_ref is not None:
            s = s + bias_ref[...]
        m_old = m_ref[:, cols]
        m_new = jnp.maximum(m_old, jnp.max(s, axis=0, keepdims=True))
        alpha = jnp.exp(m_old - m_new)
        e = jnp.exp(s - m_new)
        l_ref[:, cols] = alpha * l_ref[:, cols] + jnp.sum(e, axis=0, keepdims=True)
        acc_ref[:, cols] = (alpha * acc_ref[:, cols]
                            + _dot(vt_ref[:, pl.ds(r, tq)], e.astype(BF16)))
        m_ref[:, cols] = m_new

    def iteration(ki, bias_ref, prefetch):
        scores(ki, 1)
        softmax_pv(ki, 0, bias_ref)
        if prefetch:
            scores(ki + 1, 0)
        softmax_pv(ki, 1, bias_ref)

    scores(0, 0)

    def far_body(ki, carry):
        iteration(ki, None, True)
        return carry
    lax.fori_loop(0, jnp.maximum(qi - 1, 0), far_body, 0)

    @pl.when(qi >= 1)
    def _():
        iteration(qi - 1, b1_ref, True)

    iteration(qi, b0_ref, False)

    lam = _diff_lambda(lamv_ref, lam_init)
    o = (acc_ref[:, 0:tq] / l_ref[:, 0:tq]
         - lam * (acc_ref[:, tq:2 * tq] / l_ref[:, tq:2 * tq]))
    ms = jnp.mean(o * o, axis=0, keepdims=True)
    o = o * lax.rsqrt(ms + EPS) * subc_ref[...] * (1.0 - lam_init)
    o_ref[...] = o.T.astype(o_ref.dtype)


def _attn_prompt(q, k, vt, p, lam_init, *, tq=512):
    s = q.shape[0]
    assert s % tq == 0 and tq % CHUNK == 0 and tq + 1 >= _FAR_DIST
    kern = functools.partial(_attn_prompt_kernel, tq=tq, lam_init=lam_init)
    return pl.pallas_call(
        kern,
        out_shape=jax.ShapeDtypeStruct((s, B_W), BF16),
        grid=(B_HEADS, s // tq),
        in_specs=[pl.BlockSpec(memory_space=pltpu.SMEM),
                  pl.BlockSpec((4, B_DH), lambda h, i: (0, 0)),
                  pl.BlockSpec((B_E, 1), lambda h, i: (0, 0)),
                  pl.BlockSpec((tq, B_E), lambda h, i: (i, h)),
                  pl.BlockSpec((s, B_E), lambda h, i: (0, h)),
                  pl.BlockSpec((B_E, s), lambda h, i: (h, 0))],
        out_specs=pl.BlockSpec((tq, B_E), lambda h, i: (i, h)),
        scratch_shapes=[pltpu.VMEM((tq, tq), F32), pltpu.VMEM((tq, tq), F32),
                        pltpu.VMEM((B_E, 2 * tq), BF16),
                        pltpu.VMEM((2, tq, tq), F32),
                        pltpu.VMEM((1, 2 * tq), F32), pltpu.VMEM((1, 2 * tq), F32),
                        pltpu.VMEM((B_E, 2 * tq), F32)],
        compiler_params=_params(("arbitrary", "arbitrary"), 48),
        name="attn_prompt",
    )(p["bias_table"], p["lam_vecs"], p["subln_w"].reshape(B_E, 1), q, k, vt)


def _attn_sample_kernel(tab_ref, lamv_ref, sub_ref, q_ref, kn_ref, vn_ref, kc_ref, vc_ref, o_ref,
                        *, t_new, past, lam_init):
    h = pl.program_id(1)
    q1, q2 = _split_q(q_ref[...])
    kc = kc_ref[...].astype(BF16)
    vc = vc_ref[...].astype(BF16)
    kn = kn_ref[...].astype(BF16)
    vn = vn_ref[...].astype(BF16)

    qpos = past + lax.broadcasted_iota(jnp.int32, (t_new, past), 0)
    bias_c = _rel_bias(lax.broadcasted_iota(jnp.int32, (t_new, past), 1) - qpos, tab_ref, h)
    bias_n = _rel_bias(lax.broadcasted_iota(jnp.int32, (t_new, t_new), 1)
                       - lax.broadcasted_iota(jnp.int32, (t_new, t_new), 0), tab_ref, h)

    def branch(qz):
        sc = _dot_nt(qz, kc) + bias_c
        sn = _dot_nt(qz, kn) + bias_n
        m = jnp.maximum(jnp.max(sc, axis=-1, keepdims=True), jnp.max(sn, axis=-1, keepdims=True))
        ec = jnp.exp(sc - m)
        en = jnp.exp(sn - m)
        l = jnp.sum(ec, axis=-1, keepdims=True) + jnp.sum(en, axis=-1, keepdims=True)
        return _dot(ec.astype(BF16), vc) + _dot(en.astype(BF16), vn), l

    o1, l1 = branch(q1)
    o2, l2 = branch(q2)
    lam = _diff_lambda(lamv_ref, lam_init)
    o_ref[...] = _diff_finish(o1, l1, o2, l2, lam, sub_ref[...], lam_init).astype(o_ref.dtype)


def _attn_sample(q, k_new, v_new, k_cache, v_cache, p, lam_init, *, batch, t_new):
    past = k_cache.shape[1]
    kern = functools.partial(_attn_sample_kernel, t_new=t_new, past=past, lam_init=lam_init)
    new_blk = pl.BlockSpec((t_new, B_E), lambda b, h: (b, h))
    cache_blk = pl.BlockSpec((None, past, B_E), lambda b, h: (b, 0, h))
    return pl.pallas_call(
        kern,
        out_shape=jax.ShapeDtypeStruct((batch * t_new, B_W), BF16),
        grid=(batch, B_HEADS),
        in_specs=[pl.BlockSpec(memory_space=pltpu.SMEM),
                  pl.BlockSpec((4, B_DH), lambda b, h: (0, 0)),
                  pl.BlockSpec((1, B_E), lambda b, h: (0, 0)),
                  new_blk, new_blk, new_blk, cache_blk, cache_blk],
        out_specs=new_blk,
        compiler_params=_params(("arbitrary", "arbitrary"), 32),
        name="attn_sample",
    )(p["bias_table"], p["lam_vecs"], p["subln_w"], q, k_new, v_new,
      k_cache.reshape(batch, past, B_W), v_cache.reshape(batch, past, B_W))


def _gating_kernel(u_ref, v_ref, vnw_ref, ws_ref, bs_ref, *refs, chunk, n_chunks, emit_v):
    if emit_v:
        o_ref, vout_ref, wsm_ref = refs
    else:
        o_ref, wsm_ref = refs
        vout_ref = None

    @pl.when(pl.program_id(0) == 0)
    def _():
        ri = lax.broadcasted_iota(jnp.int32, (chunk, chunk), 0)
        ci = lax.broadcasted_iota(jnp.int32, (chunk, chunk), 1)
        for g in range(C_GROUPS):
            wsm_ref[g] = jnp.where(ci <= ri, ws_ref[g], 0.0).astype(BF16)

    for c in range(n_chunks):
        r0 = c * chunk
        vn = _rms_rows(v_ref[r0:r0 + chunk, :], vnw_ref[...])
        if emit_v:
            vout_ref[r0:r0 + chunk, :] = vn
        vnb = vn.astype(BF16)
        for g in range(C_GROUPS):
            cols = slice(g * C_GW, (g + 1) * C_GW)
            mix = _dot(wsm_ref[g], vnb[:, cols]) + bs_ref[:, g:g + 1]
            o_ref[r0:r0 + chunk, cols] = (u_ref[r0:r0 + chunk, cols] * mix).astype(o_ref.dtype)


def _gating(z, vnw, ws, bs_t, *, chunk, n_chunks, emit_v):
    m = z.shape[0]
    rt = chunk * n_chunks
    assert m % rt == 0
    kern = functools.partial(_gating_kernel, chunk=chunk, n_chunks=n_chunks, emit_v=emit_v)
    out_shape = [jax.ShapeDtypeStruct((m, D_C), BF16)]
    out_specs = [pl.BlockSpec((rt, D_C), lambda i: (i, 0))]
    if emit_v:
        out_shape.append(jax.ShapeDtypeStruct((m, D_C), F32))
        out_specs.append(pl.BlockSpec((rt, D_C), lambda i: (i, 0)))
    res = pl.pallas_call(
        kern,
        out_shape=out_shape,
        grid=(m // rt,),
        in_specs=[pl.BlockSpec((rt, D_C), lambda i: (i, 0)),
                  pl.BlockSpec((rt, D_C), lambda i: (i, 1)),
                  pl.BlockSpec((1, D_C), lambda i: (0, 0)),
                  pl.BlockSpec((C_GROUPS, chunk, chunk), lambda i: (0, 0, 0)),
                  pl.BlockSpec((chunk, C_GROUPS), lambda i: (0, 0))],
        out_specs=out_specs,
        scratch_shapes=[pltpu.VMEM((C_GROUPS, chunk, chunk), BF16)],
        compiler_params=_params(("arbitrary",), 40),
        name="gating",
    )(z, z, vnw.reshape(1, D_C), ws, bs_t)
    return res if emit_v else (res[0], None)


def _block_diag(w):
    per = 256 // LRU_BW
    w4 = w.reshape(LRU_BLOCKS // per, per, LRU_BW, LRU_BW)
    eye = jnp.eye(per, dtype=w.dtype)
    bd = jnp.einsum("gaij,ab->gaibj", w4, eye)
    return bd.reshape(LRU_BLOCKS // per, 256, 256).astype(BF16)


def kernel(x_prompt, x_sample, state_conv, state_lru, cache_k, cache_v, w_even_in, w_even_out,
           conv_w, conv_b, lru_wa, lru_ba, lru_wx, lru_bx, lru_lambda, lam_vecs, subln_w,
           rel_bias_table, w_odd_in, w_odd_out, gmlp_vnorm_w, gmlp_ws, gmlp_bs, norm_mix_w,
           norm_ffn_w, norm_final_w, w_ff_up, w_ff_down):
    batch_p, seq_p, _ = x_prompt.shape
    batch_s, seq_s, _ = x_sample.shape
    assert batch_p == 1
    yp = x_prompt.reshape(batch_p * seq_p, D_MODEL)
    ys = x_sample.reshape(batch_s * seq_s, D_MODEL)
    tm_p = 1024
    tm_s = batch_s * seq_s
    scale = B_DH ** -0.5

    w_even_in_b = w_even_in.astype(BF16)
    w_even_out_b = w_even_out.astype(BF16)
    w_odd_in_b = w_odd_in.astype(BF16)
    w_odd_out_b = w_odd_out.astype(BF16)
    w_up_b = w_ff_up.astype(BF16)
    w_down_b = w_ff_down.astype(BF16)

    q0, k0, v0 = 2 * D_RNN, 2 * D_RNN + B_W, 2 * D_RNN + 2 * B_W
    even_segs = [(0, 2 * D_RNN, F32, 1.0), (q0, B_W, BF16, scale), (k0, B_W, F32, 1.0),
                 (v0, B_W, F32, 1.0)]
    even_segs_p = even_segs + [(k0, B_W, BF16, 1.0)]
    odd_segs = [(0, 2 * D_C, F32, 1.0)]
    zeros_conv = jnp.zeros((batch_p, CONV_W - 1, D_RNN), F32)
    zeros_h = jnp.zeros((batch_p, D_RNN), F32)

    p_conv, p_lru, p_k, p_v = [], [], [], []
    s_conv, s_lru, s_k, s_v, s_gv = [], [], [], [], []
    for l in range(DEPTH):
        if l % 2 == 0:
            e = l // 2
            lam_init = 0.8 - 0.6 * math.exp(-0.3 * l)
            pa = dict(conv_w=conv_w[e], conv_b=conv_b[e].reshape(1, D_RNN),
                      wa_bd=_block_diag(lru_wa[e]), ba=lru_ba[e].reshape(1, D_RNN),
                      wx_bd=_block_diag(lru_wx[e]), bx=lru_bx[e].reshape(1, D_RNN),
                      lam=lru_lambda[e].reshape(1, D_RNN))
            pb = dict(bias_table=rel_bias_table, lam_vecs=lam_vecs[e],
                      subln_w=subln_w[e].reshape(1, B_E))
            gx, q, k, v, k_b = _norm_matmul(yp, norm_mix_w[l], w_even_in_b[e], even_segs_p,
                                            tm=tm_p)
            ya, c_new, h_new = _mixer_a(gx, zeros_conv, zeros_h, pa, batch=batch_p, seq=seq_p,
                                        tt=256)
            yb = _attn_prompt(q, k_b, v.T.astype(BF16), pb, lam_init)
            yp = _matmul_res([ya, yb], w_even_out_b[e], yp, tm=tm_p)
            p_conv.append(c_new)
            p_lru.append(h_new.reshape(batch_p, D_RNN))
            p_k.append(k.reshape(batch_p, seq_p, B_HEADS, B_E))
            p_v.append(v.reshape(batch_p, seq_p, B_HEADS, B_E))
            gx, q, k, v = _norm_matmul(ys, norm_mix_w[l], w_even_in_b[e], even_segs, tm=tm_s)
            ya, c_new, h_new = _mixer_a(gx, state_conv[e], state_lru[e], pa, batch=batch_s,
                                        seq=seq_s, tt=seq_s)
            yb = _attn_sample(q, k, v, cache_k[e], cache_v[e], pb, lam_init, batch=batch_s,
                              t_new=seq_s)
            ys = _matmul_res([ya, yb], w_even_out_b[e], ys, tm=tm_s)
            s_conv.append(c_new)
            s_lru.append(h_new.reshape(batch_s, D_RNN))
            s_k.append(k.reshape(batch_s, seq_s, B_HEADS, B_E))
            s_v.append(v.reshape(batch_s, seq_s, B_HEADS, B_E))
        else:
            o = l // 2
            (z,) = _norm_matmul(yp, norm_mix_w[l], w_odd_in_b[o], odd_segs,
                                act="gelu", tm=tm_p)
            gated, _ = _gating(z, gmlp_vnorm_w[o], gmlp_ws[o], gmlp_bs[o].T, chunk=GMLP_CHUNK,
                               n_chunks=4, emit_v=False)
            yp = _matmul_res([gated], w_odd_out_b[o], yp, tm=tm_p)
            (z,) = _norm_matmul(ys, norm_mix_w[l], w_odd_in_b[o], odd_segs,
                                act="gelu", tm=tm_s)
            gated, gv = _gating(z, gmlp_vnorm_w[o], gmlp_ws[o][:, :seq_s, :seq_s],
                                gmlp_bs[o][:, :seq_s].T, chunk=seq_s, n_chunks=1, emit_v=True)
            ys = _matmul_res([gated], w_odd_out_b[o], ys, tm=tm_s)
            s_gv.append(gv.reshape(batch_s, seq_s, D_C))
        last = l == DEPTH - 1
        yp = _ffn(yp, norm_ffn_w[l], w_up_b[l], w_down_b[l], norm_final_w, tm=tm_p, final=last)
        ys = _ffn(ys, norm_ffn_w[l], w_up_b[l], w_down_b[l], norm_final_w, tm=tm_s, final=last)

    return (yp.reshape(batch_p, seq_p, D_MODEL), ys.reshape(batch_s, seq_s, D_MODEL),
            jnp.stack(p_conv), jnp.stack(p_lru), jnp.stack(p_k), jnp.stack(p_v),
            jnp.stack(s_conv), jnp.stack(s_lru), jnp.stack(s_k), jnp.stack(s_v),
            jnp.stack(s_gv))
```

```python
import functools
import math

import jax
import jax.numpy as jnp
from jax import lax
from jax.experimental import pallas as pl
from jax.experimental.pallas import tpu as pltpu

D_MODEL = 2048
DEPTH = 4
CHUNK = 64
EPS = 1e-6
D_RNN = D_MODEL // 2
LRU_BLOCKS = 16
LRU_BW = D_RNN // LRU_BLOCKS
CONV_W = 4
LRU_C = 8.0
B_HEADS = 8
B_DH = 64
B_E = 2 * B_DH
B_W = B_HEADS * B_E
N_BUCKETS = 32
MAX_DIST = 512
GMLP_CHUNK = 128
D_C = D_MODEL
C_GROUPS = 16
C_GW = D_C // C_GROUPS
D_FF = 4 * D_MODEL

F32 = jnp.float32
BF16 = jnp.bfloat16
MASK_VALUE = -1e30
LOG2E = math.log2(math.e)
MIB = 1024 * 1024

_NB = N_BUCKETS // 2
_MAX_EXACT = _NB // 2
_BUCKET_THRESHOLDS = tuple(
    int(math.ceil(_MAX_EXACT * (MAX_DIST / _MAX_EXACT) ** (k / (_NB - _MAX_EXACT)) - 1e-9))
    for k in range(1, _NB - _MAX_EXACT))
_FAR_BUCKET = _NB - 1
_FAR_DIST = _BUCKET_THRESHOLDS[-1]


def _dot(a, b):
    return jnp.dot(a, b, preferred_element_type=F32)


def _dot_nt(a, b):
    return lax.dot_general(a, b, (((1,), (1,)), ((), ())), preferred_element_type=F32)


def _rms_rows(x, w):
    ms = jnp.mean(x * x, axis=-1, keepdims=True)
    return x * lax.rsqrt(ms + EPS) * w


def _gelu_tanh(x):
    c = math.sqrt(2.0 / math.pi)
    return x * (0.5 * (1.0 + jnp.tanh(c * (x + 0.044715 * (x * x * x)))))


def _sigmoid(x):
    return 1.0 / (1.0 + jnp.exp(-x))


def _params(sem, vmem_mib):
    return pltpu.CompilerParams(dimension_semantics=sem, vmem_limit_bytes=vmem_mib * MIB)


def _norm_matmul_kernel(x_ref, nw_ref, w_ref, *refs, segs, act, tm, row_chunk):
    out_refs = refs[:len(segs)]
    xn_ref = refs[len(segs)]
    n = pl.program_id(1)

    @pl.when(n == 0)
    def _():
        def body(c, carry):
            r = pl.multiple_of(c * row_chunk, row_chunk)
            x = x_ref[pl.ds(r, row_chunk), :]
            xn_ref[pl.ds(r, row_chunk), :] = _rms_rows(x, nw_ref[...]).astype(BF16)
            return carry
        lax.fori_loop(0, tm // row_chunk, body, 0)

    acc = _dot(xn_ref[...], w_ref[...])
    if act == "gelu":
        acc = _gelu_tanh(acc)
    for (lo, hi, scale), o_ref in zip(segs, out_refs):
        def store(o_ref=o_ref, scale=scale):
            val = acc if scale == 1.0 else acc * scale
            o_ref[...] = val.astype(o_ref.dtype)
        if len(segs) == 1:
            store()
        else:
            pl.when((n >= lo) & (n < hi))(store)


def _norm_matmul(x, nw, w, layer, seg_defs, *, act=None, tm, tn=512):
    m, d = x.shape
    n_total = w.shape[2]
    assert m % tm == 0 and n_total % tn == 0
    segs, out_shapes, out_specs = [], [], []
    for first, cols, dtype, scale in seg_defs:
        assert cols % tn == 0 and first % tn == 0
        lo, nb = first // tn, cols // tn
        segs.append((lo, lo + nb, scale))
        out_shapes.append(jax.ShapeDtypeStruct((m, cols), dtype))
        out_specs.append(pl.BlockSpec(
            (tm, tn), lambda i, j, lo=lo, nb=nb: (i, jnp.clip(j - lo, 0, nb - 1))))
    kern = functools.partial(_norm_matmul_kernel, segs=tuple(segs), act=act, tm=tm,
                             row_chunk=min(tm, 64))
    return pl.pallas_call(
        kern,
        out_shape=out_shapes,
        grid=(m // tm, n_total // tn),
        in_specs=[pl.BlockSpec((tm, d), lambda i, j: (i, 0)),
                  pl.BlockSpec((1, d), lambda i, j: (0, 0)),
                  pl.BlockSpec((None, d, tn), lambda i, j: (layer, 0, j))],
        out_specs=out_specs,
        scratch_shapes=[pltpu.VMEM((tm, d), BF16)],
        compiler_params=_params(("arbitrary", "arbitrary"), 52),
        name="norm_matmul",
    )(x, nw.reshape(1, d), w)


def _matmul_res_kernel(*refs, n_in):
    a_refs = refs[:n_in]
    w_refs = refs[n_in:2 * n_in]
    res_ref = refs[2 * n_in]
    o_ref = refs[2 * n_in + 1]
    acc = res_ref[...]
    for a_ref, w_ref in zip(a_refs, w_refs):
        acc = acc + _dot(a_ref[...], w_ref[...])
    o_ref[...] = acc


def _matmul_res(a_list, w, layer, res, *, tm, tn=1024):
    m, n_total = res.shape
    n_in = len(a_list)
    kb = w.shape[1] // n_in
    assert all(a.shape == (m, kb) for a in a_list) and m % tm == 0 and n_total % tn == 0
    in_specs = [pl.BlockSpec((tm, kb), lambda i, j: (i, 0)) for _ in a_list]
    in_specs += [pl.BlockSpec((None, kb, tn), lambda i, j, b=b: (layer, b, j))
                 for b in range(n_in)]
    in_specs += [pl.BlockSpec((tm, tn), lambda i, j: (i, j))]
    return pl.pallas_call(
        functools.partial(_matmul_res_kernel, n_in=n_in),
        out_shape=jax.ShapeDtypeStruct((m, n_total), F32),
        grid=(m // tm, n_total // tn),
        in_specs=in_specs,
        out_specs=pl.BlockSpec((tm, tn), lambda i, j: (i, j)),
        compiler_params=_params(("arbitrary", "arbitrary"), 48),
        name="matmul_res",
    )(*a_list, *([w] * n_in), res)


def _ffn_kernel(x_ref, nw_ref, wu_ref, wd_ref, fw_ref, o_ref, *refs, tm, row_chunk, col_chunk,
                final, emit_bf16):
    if emit_bf16:
        wub_ref, wdb_ref, xn_ref = refs
        wub_ref[...] = wu_ref[...].astype(BF16)
        wdb_ref[...] = wd_ref[...].astype(BF16)
        wu_ref, wd_ref = wub_ref, wdb_ref
    else:
        (xn_ref,) = refs
    f = pl.program_id(1)

    @pl.when(f == 0)
    def _():
        def body(c, carry):
            r = pl.multiple_of(c * row_chunk, row_chunk)
            x = x_ref[pl.ds(r, row_chunk), :]
            xn_ref[pl.ds(r, row_chunk), :] = _rms_rows(x, nw_ref[...]).astype(BF16)
            o_ref[pl.ds(r, row_chunk), :] = x
            return carry
        lax.fori_loop(0, tm // row_chunk, body, 0)

    h = _dot(xn_ref[...], wu_ref[...])
    h = jnp.maximum(h, 0.0)
    h = (h * h).astype(BF16)
    d = o_ref.shape[1]
    for c0 in range(0, d, col_chunk):
        o_ref[:, c0:c0 + col_chunk] += _dot(h, wd_ref[:, c0:c0 + col_chunk])

    if final:
        @pl.when(f == pl.num_programs(1) - 1)
        def _():
            def body(c, carry):
                r = pl.multiple_of(c * row_chunk, row_chunk)
                o_ref[pl.ds(r, row_chunk), :] = _rms_rows(o_ref[pl.ds(r, row_chunk), :], fw_ref[...])
                return carry
            lax.fori_loop(0, tm // row_chunk, body, 0)


def _ffn(x, nw, wu, wd, layer, fw, *, tm, tf=512, final=False, emit_bf16=False):
    m, d = x.shape
    dff = wu.shape[2]
    assert m % tm == 0 and dff % tf == 0 and (not emit_bf16 or m == tm)
    kern = functools.partial(_ffn_kernel, tm=tm, row_chunk=min(tm, 64), col_chunk=512, final=final,
                             emit_bf16=emit_bf16)
    out_shape = [jax.ShapeDtypeStruct((m, d), F32)]
    out_specs = [pl.BlockSpec((tm, d), lambda i, f: (i, 0))]
    if emit_bf16:
        out_shape += [jax.ShapeDtypeStruct((1, d, dff), BF16),
                      jax.ShapeDtypeStruct((1, dff, d), BF16)]
        out_specs += [pl.BlockSpec((None, d, tf), lambda i, f: (0, 0, f)),
                      pl.BlockSpec((None, tf, d), lambda i, f: (0, f, 0))]
    res = pl.pallas_call(
        kern,
        out_shape=out_shape,
        grid=(m // tm, dff // tf),
        in_specs=[pl.BlockSpec((tm, d), lambda i, f: (i, 0)),
                  pl.BlockSpec((1, d), lambda i, f: (0, 0)),
                  pl.BlockSpec((None, d, tf), lambda i, f: (layer, 0, f)),
                  pl.BlockSpec((None, tf, d), lambda i, f: (layer, f, 0)),
                  pl.BlockSpec((1, d), lambda i, f: (0, 0))],
        out_specs=out_specs,
        scratch_shapes=[pltpu.VMEM((tm, d), BF16)],
        compiler_params=_params(("arbitrary", "arbitrary"), 56),
        name="ffn",
    )(x, nw.reshape(1, d), wu, wd, fw.reshape(1, d))
    return res if emit_bf16 else res[0]


_XP_PAD = 8


def _scan_group(a, u, hb, row):
    for dist in (1, 2, 4):
        keep = row >= dist
        a_s = jnp.where(keep, pltpu.roll(a, dist, 0), 1.0)
        u_s = jnp.where(keep, pltpu.roll(u, dist, 0), 0.0)
        u = a * u_s + u
        a = a * a_s
    return a * hb + u


def _mixer_a_kernel(g_ref, xr_ref, cprev_ref, hprev_ref, cw_ref, cb_ref, wa_ref, ba_ref,
                    wx_ref, bx_ref, lam_ref, ya_ref, conv_ref, hlast_ref, xp_ref, h_ref,
                    *, tt, rc):
    t = pl.program_id(1)
    hist = CONV_W - 1

    @pl.when(t == 0)
    def _():
        xp_ref[_XP_PAD - hist:_XP_PAD, :] = cprev_ref[0]
        h_ref[...] = hprev_ref[0]

    @pl.when(t > 0)
    def _():
        xp_ref[_XP_PAD - hist:_XP_PAD, :] = xp_ref[_XP_PAD + tt - hist:_XP_PAD + tt, :]

    xp_ref[_XP_PAD:_XP_PAD + tt, :] = xr_ref[...]

    z = -lam_ref[...]
    softplus = jnp.maximum(z, 0.0) + jnp.log1p(jnp.exp(-jnp.abs(z)))
    row = lax.broadcasted_iota(jnp.int32, (8, D_RNN), 0)
    nblk = D_RNN // 256
    hb = jnp.broadcast_to(h_ref[...], (8, D_RNN))

    for c in range(tt // rc):
        r0 = c * rc
        xc = cb_ref[...]
        for j in range(CONV_W):
            s = _XP_PAD - hist + j + r0
            xc = xc + xp_ref[s:s + rc, :] * cw_ref[j:j + 1, :]
        xcb = xc.astype(BF16)
        r_lin = jnp.concatenate(
            [_dot(xcb[:, 256 * b:256 * (b + 1)], wa_ref[b]) for b in range(nblk)], axis=1)
        i_lin = jnp.concatenate(
            [_dot(xcb[:, 256 * b:256 * (b + 1)], wx_ref[b]) for b in range(nblk)], axis=1)
        r = _sigmoid(r_lin + ba_ref[...])
        i = _sigmoid(i_lin + bx_ref[...])
        log_a = (-LRU_C) * r * softplus
        a = jnp.exp(log_a)
        th = jnp.tanh(log_a)
        u = jnp.sqrt((-2.0 * th) / (1.0 - th)) * (i * xc)
        hs = []
        for k in range(rc // 8):
            hrows = _scan_group(a[8 * k:8 * k + 8], u[8 * k:8 * k + 8], hb, row)
            hb = jnp.broadcast_to(hrows[7:8, :], (8, D_RNN))
            hs.append(hrows)
        hs = jnp.concatenate(hs, axis=0)
        g = g_ref[r0:r0 + rc, :]
        ya_ref[r0:r0 + rc, :] = (_gelu_tanh(g) * hs).astype(ya_ref.dtype)

    h_ref[...] = hb[0:1, :]
    conv_ref[0] = xr_ref[tt - hist:tt, :]
    hlast_ref[0] = hb[0:1, :]


def _mixer_a(gx, conv_prev, h_prev, p, *, batch, seq, tt):
    assert seq % tt == 0 and tt % 8 == 0
    rc = min(tt, 64)
    nt = seq // tt
    kern = functools.partial(_mixer_a_kernel, tt=tt, rc=rc)
    row_blk = lambda b, t: (b * nt + t, 0)
    vec = lambda b, t: (0, 0)
    return pl.pallas_call(
        kern,
        out_shape=[jax.ShapeDtypeStruct((batch * seq, D_RNN), BF16),
                   jax.ShapeDtypeStruct((batch, CONV_W - 1, D_RNN), F32),
                   jax.ShapeDtypeStruct((batch, 1, D_RNN), F32)],
        grid=(batch, nt),
        in_specs=[pl.BlockSpec((tt, D_RNN), row_blk),
                  pl.BlockSpec((tt, D_RNN), lambda b, t: (b * nt + t, 1)),
                  pl.BlockSpec((1, CONV_W - 1, D_RNN), lambda b, t: (b, 0, 0)),
                  pl.BlockSpec((1, 1, D_RNN), lambda b, t: (b, 0, 0)),
                  pl.BlockSpec((CONV_W, D_RNN), vec),
                  pl.BlockSpec((1, D_RNN), vec),
                  pl.BlockSpec((D_RNN // 256, 256, 256), lambda b, t: (0, 0, 0)),
                  pl.BlockSpec((1, D_RNN), vec),
                  pl.BlockSpec((D_RNN // 256, 256, 256), lambda b, t: (0, 0, 0)),
                  pl.BlockSpec((1, D_RNN), vec),
                  pl.BlockSpec((1, D_RNN), vec)],
        out_specs=[pl.BlockSpec((tt, D_RNN), row_blk),
                   pl.BlockSpec((1, CONV_W - 1, D_RNN), lambda b, t: (b, 0, 0)),
                   pl.BlockSpec((1, 1, D_RNN), lambda b, t: (b, 0, 0))],
        scratch_shapes=[pltpu.VMEM((_XP_PAD + tt, D_RNN), F32),
                        pltpu.VMEM((1, D_RNN), F32)],
        compiler_params=_params(("arbitrary", "arbitrary"), 40),
        name="mixer_a",
    )(gx, gx, conv_prev, h_prev.reshape(batch, 1, D_RNN), p["conv_w"], p["conv_b"],
      p["wa_bd"], p["ba"], p["wx_bd"], p["bx"], p["lam"])


def _rel_bias(d, tab_ref, h):
    n = jnp.abs(d)
    large = jnp.full(d.shape, _MAX_EXACT, jnp.int32)
    for thr in _BUCKET_THRESHOLDS:
        large = large + (n >= thr).astype(jnp.int32)
    bucket = jnp.where(d > 0, _NB, 0) + jnp.where(n < _MAX_EXACT, n, large)
    val = jnp.full(d.shape, tab_ref[0, h], F32)
    for b in range(1, N_BUCKETS):
        val = jnp.where(bucket == b, tab_ref[b, h], val)
    return val


def _split_q(q):
    lane = lax.broadcasted_iota(jnp.int32, q.shape, 1)
    zero = jnp.zeros_like(q)
    return jnp.where(lane < B_DH, q, zero), jnp.where(lane >= B_DH, q, zero)


def _diff_lambda(lamv_ref, lam_init):
    lv = lamv_ref[...]
    s01 = jnp.sum(lv[0:1, :] * lv[1:2, :], axis=-1, keepdims=True)
    s23 = jnp.sum(lv[2:3, :] * lv[3:4, :], axis=-1, keepdims=True)
    return jnp.exp(s01) - jnp.exp(s23) + lam_init


def _diff_finish(o1, l1, o2, l2, lam, sub, lam_init):
    o = o1 / l1 - lam * (o2 / l2)
    return _rms_rows(o, sub) * (1.0 - lam_init)


def _attn_prompt_kernel(tab_ref, lamv_ref, subc_ref, q_ref, k_ref, vt_ref, o_ref,
                        b0_ref, b1_ref, qz_ref, s_ref, m_ref, l_ref, acc_ref, *, tq, lam_init):
    h = pl.program_id(0)
    qi = pl.program_id(1)
    far_bias = tab_ref[_FAR_BUCKET, h]

    @pl.when(qi == 0)
    def _():
        ki_ = lax.broadcasted_iota(jnp.int32, (tq, tq), 0)
        qi_ = lax.broadcasted_iota(jnp.int32, (tq, tq), 1)
        d = ki_ - qi_
        visible = (ki_ // CHUNK) <= (qi_ // CHUNK)
        b0_ref[...] = jnp.where(visible, (_rel_bias(d, tab_ref, h) - far_bias) * LOG2E,
                                MASK_VALUE)
        b1_ref[...] = (_rel_bias(d - tq, tab_ref, h) - far_bias) * LOG2E

    qt = q_ref[...].astype(F32).T
    row = lax.broadcasted_iota(jnp.int32, qt.shape, 0)
    qz_ref[:, 0:tq] = jnp.where(row < B_DH, qt, 0.0).astype(BF16)
    qz_ref[:, tq:2 * tq] = jnp.where(row >= B_DH, qt, 0.0).astype(BF16)
    m_ref[...] = jnp.full(m_ref.shape, MASK_VALUE, F32)
    l_ref[...] = jnp.zeros(l_ref.shape, F32)
    acc_ref[...] = jnp.zeros(acc_ref.shape, F32)

    def scores(ki, br):
        r = pl.multiple_of(ki * tq, tq)
        s_ref[br] = _dot(k_ref[pl.ds(r, tq), :], qz_ref[:, br * tq:(br + 1) * tq])

    def softmax_pv(ki, br, bias_ref):
        cols = slice(br * tq, (br + 1) * tq)
        r = pl.multiple_of(ki * tq, tq)
        s = s_ref[br]
        if bias_ref is not None:
            s = s + bias_ref[...]
        m_old = m_ref[:, cols]
        m_new = jnp.maximum(m_old, jnp.max(s, axis=0, keepdims=True))
        alpha = jnp.exp2(m_old - m_new)
        e = jnp.exp2(s - m_new)
        l_ref[:, cols] = alpha * l_ref[:, cols] + jnp.sum(e, axis=0, keepdims=True)
        acc_ref[:, cols] = (alpha * acc_ref[:, cols]
                            + _dot(vt_ref[:, pl.ds(r, tq)], e.astype(BF16)))
        m_ref[:, cols] = m_new

    def iteration(ki, bias_ref, prefetch):
        scores(ki, 1)
        softmax_pv(ki, 0, bias_ref)
        if prefetch:
            scores(ki + 1, 0)
        softmax_pv(ki, 1, bias_ref)

    scores(0, 0)

    def far_body(ki, carry):
        iteration(ki, None, True)
        return carry
    lax.fori_loop(0, jnp.maximum(qi - 1, 0), far_body, 0)

    @pl.when(qi >= 1)
    def _():
        iteration(qi - 1, b1_ref, True)

    iteration(qi, b0_ref, False)

    lam = _diff_lambda(lamv_ref, lam_init)
    o = (acc_ref[:, 0:tq] / l_ref[:, 0:tq]
         - lam * (acc_ref[:, tq:2 * tq] / l_ref[:, tq:2 * tq]))
    ms = jnp.mean(o * o, axis=0, keepdims=True)
    o = o * lax.rsqrt(ms + EPS) * subc_ref[...] * (1.0 - lam_init)
    o_ref[...] = o.T.astype(o_ref.dtype)


def _attn_prompt(q, k, vt, p, lam_init, *, tq=512):
    s = q.shape[0]
    assert s % tq == 0 and tq % CHUNK == 0 and tq + 1 >= _FAR_DIST
    kern = functools.partial(_attn_prompt_kernel, tq=tq, lam_init=lam_init)
    return pl.pallas_call(
        kern,
        out_shape=jax.ShapeDtypeStruct((s, B_W), BF16),
        grid=(B_HEADS, s // tq),
        in_specs=[pl.BlockSpec(memory_space=pltpu.SMEM),
                  pl.BlockSpec((4, B_DH), lambda h, i: (0, 0)),
                  pl.BlockSpec((B_E, 1), lambda h, i: (0, 0)),
                  pl.BlockSpec((tq, B_E), lambda h, i: (i, h)),
                  pl.BlockSpec((s, B_E), lambda h, i: (0, h)),
                  pl.BlockSpec((B_E, s), lambda h, i: (h, 0))],
        out_specs=pl.BlockSpec((tq, B_E), lambda h, i: (i, h)),
        scratch_shapes=[pltpu.VMEM((tq, tq), F32), pltpu.VMEM((tq, tq), F32),
                        pltpu.VMEM((B_E, 2 * tq), BF16),
                        pltpu.VMEM((2, tq, tq), F32),
                        pltpu.VMEM((1, 2 * tq), F32), pltpu.VMEM((1, 2 * tq), F32),
                        pltpu.VMEM((B_E, 2 * tq), F32)],
        compiler_params=_params(("arbitrary", "arbitrary"), 48),
        name="attn_prompt",
    )(p["bias_table"], p["lam_vecs"], p["subln_w"].reshape(B_E, 1), q, k, vt)


def _attn_sample_kernel(tab_ref, lamv_ref, sub_ref, q_ref, kn_ref, vn_ref, kc_ref, vc_ref, o_ref,
                        bc_ref, bn_ref, *, t_new, past, lam_init):
    @pl.when(pl.program_id(0) == 0)
    def _():
        qpos = past + lax.broadcasted_iota(jnp.int32, (t_new, past), 0)
        d_c = lax.broadcasted_iota(jnp.int32, (t_new, past), 1) - qpos
        d_n = (lax.broadcasted_iota(jnp.int32, (t_new, t_new), 1)
               - lax.broadcasted_iota(jnp.int32, (t_new, t_new), 0))
        for h in range(B_HEADS):
            bc_ref[h] = LOG2E * _rel_bias(d_c, tab_ref, h)
            bn_ref[h] = LOG2E * _rel_bias(d_n, tab_ref, h)

    lam = _diff_lambda(lamv_ref, lam_init)
    for h in range(B_HEADS):
        cols = slice(h * B_E, (h + 1) * B_E)
        q1, q2 = _split_q(q_ref[:, cols])
        kc = kc_ref[:, h, :].astype(BF16)
        vc = vc_ref[:, h, :].astype(BF16)
        kn = kn_ref[:, cols].astype(BF16)
        vn = vn_ref[:, cols].astype(BF16)

        def branch(qz):
            sc = _dot_nt(qz, kc) + bc_ref[h]
            sn = _dot_nt(qz, kn) + bn_ref[h]
            m = jnp.maximum(jnp.max(sc, axis=-1, keepdims=True),
                            jnp.max(sn, axis=-1, keepdims=True))
            ec = jnp.exp2(sc - m)
            en = jnp.exp2(sn - m)
            l = jnp.sum(ec, axis=-1, keepdims=True) + jnp.sum(en, axis=-1, keepdims=True)
            return _dot(ec.astype(BF16), vc) + _dot(en.astype(BF16), vn), l

        o1, l1 = branch(q1)
        o2, l2 = branch(q2)
        o_ref[:, cols] = _diff_finish(o1, l1, o2, l2, lam, sub_ref[...],
                                      lam_init).astype(o_ref.dtype)


def _attn_sample(q, k_new, v_new, k_cache, v_cache, layer, p, lam_init, *, batch, t_new):
    past = k_cache.shape[2]
    kern = functools.partial(_attn_sample_kernel, t_new=t_new, past=past, lam_init=lam_init)
    new_blk = pl.BlockSpec((t_new, B_W), lambda b: (b, 0))
    cache_blk = pl.BlockSpec((None, None, past, B_HEADS, B_E), lambda b: (layer, b, 0, 0, 0))
    return pl.pallas_call(
        kern,
        out_shape=jax.ShapeDtypeStruct((batch * t_new, B_W), BF16),
        grid=(batch,),
        in_specs=[pl.BlockSpec(memory_space=pltpu.SMEM),
                  pl.BlockSpec((4, B_DH), lambda b: (0, 0)),
                  pl.BlockSpec((1, B_E), lambda b: (0, 0)),
                  new_blk, new_blk, new_blk, cache_blk, cache_blk],
        out_specs=new_blk,
        scratch_shapes=[pltpu.VMEM((B_HEADS, t_new, past), F32),
                        pltpu.VMEM((B_HEADS, t_new, t_new), F32)],
        compiler_params=_params(("arbitrary",), 40),
        name="attn_sample",
    )(p["bias_table"], p["lam_vecs"], p["subln_w"], q, k_new, v_new, k_cache, v_cache)


def _gating_kernel(u_ref, v_ref, vnw_ref, ws_ref, bs_ref, *refs, chunk, n_chunks, emit_v):
    if emit_v:
        o_ref, vout_ref, wsm_ref = refs
    else:
        o_ref, wsm_ref = refs
        vout_ref = None

    @pl.when(pl.program_id(0) == 0)
    def _():
        ri = lax.broadcasted_iota(jnp.int32, (chunk, chunk), 0)
        ci = lax.broadcasted_iota(jnp.int32, (chunk, chunk), 1)
        for g in range(C_GROUPS):
            wsm_ref[g] = jnp.where(ci <= ri, ws_ref[g], 0.0).astype(BF16)

    for c in range(n_chunks):
        r0 = c * chunk
        vn = _rms_rows(v_ref[r0:r0 + chunk, :], vnw_ref[...])
        if emit_v:
            vout_ref[r0:r0 + chunk, :] = vn
        vnb = vn.astype(BF16)
        for g in range(C_GROUPS):
            cols = slice(g * C_GW, (g + 1) * C_GW)
            mix = _dot(wsm_ref[g], vnb[:, cols]) + bs_ref[:, g:g + 1]
            o_ref[r0:r0 + chunk, cols] = (u_ref[r0:r0 + chunk, cols] * mix).astype(o_ref.dtype)


def _gating(z, vnw, ws, bs_t, *, chunk, n_chunks, emit_v):
    m = z.shape[0]
    rt = chunk * n_chunks
    assert m % rt == 0
    kern = functools.partial(_gating_kernel, chunk=chunk, n_chunks=n_chunks, emit_v=emit_v)
    out_shape = [jax.ShapeDtypeStruct((m, D_C), BF16)]
    out_specs = [pl.BlockSpec((rt, D_C), lambda i: (i, 0))]
    if emit_v:
        out_shape.append(jax.ShapeDtypeStruct((m, D_C), F32))
        out_specs.append(pl.BlockSpec((rt, D_C), lambda i: (i, 0)))
    res = pl.pallas_call(
        kern,
        out_shape=out_shape,
        grid=(m // rt,),
        in_specs=[pl.BlockSpec((rt, D_C), lambda i: (i, 0)),
                  pl.BlockSpec((rt, D_C), lambda i: (i, 1)),
                  pl.BlockSpec((1, D_C), lambda i: (0, 0)),
                  pl.BlockSpec((C_GROUPS, chunk, chunk), lambda i: (0, 0, 0)),
                  pl.BlockSpec((chunk, C_GROUPS), lambda i: (0, 0))],
        out_specs=out_specs,
        scratch_shapes=[pltpu.VMEM((C_GROUPS, chunk, chunk), BF16)],
        compiler_params=_params(("arbitrary",), 40),
        name="gating",
    )(z, z, vnw.reshape(1, D_C), ws, bs_t)
    return res if emit_v else (res[0], None)


def _block_diag(w):
    per = 256 // LRU_BW
    w4 = w.reshape(LRU_BLOCKS // per, per, LRU_BW, LRU_BW)
    eye = jnp.eye(per, dtype=w.dtype)
    bd = jnp.einsum("gaij,ab->gaibj", w4, eye)
    return bd.reshape(LRU_BLOCKS // per, 256, 256).astype(BF16)


def kernel(x_prompt, x_sample, state_conv, state_lru, cache_k, cache_v, w_even_in, w_even_out,
           conv_w, conv_b, lru_wa, lru_ba, lru_wx, lru_bx, lru_lambda, lam_vecs, subln_w,
           rel_bias_table, w_odd_in, w_odd_out, gmlp_vnorm_w, gmlp_ws, gmlp_bs, norm_mix_w,
           norm_ffn_w, norm_final_w, w_ff_up, w_ff_down):
    batch_p, seq_p, _ = x_prompt.shape
    batch_s, seq_s, _ = x_sample.shape
    assert batch_p == 1
    yp = x_prompt.reshape(batch_p * seq_p, D_MODEL)
    ys = x_sample.reshape(batch_s * seq_s, D_MODEL)
    tm_p = 1024
    tm_s = batch_s * seq_s
    scale = B_DH ** -0.5 * LOG2E

    w_even_in_b = w_even_in.astype(BF16)
    w_even_out_b = w_even_out.astype(BF16)
    w_odd_in_b = w_odd_in.astype(BF16)
    w_odd_out_b = w_odd_out.astype(BF16)

    q0, k0, v0 = 2 * D_RNN, 2 * D_RNN + B_W, 2 * D_RNN + 2 * B_W
    even_segs = [(0, 2 * D_RNN, F32, 1.0), (q0, B_W, BF16, scale), (k0, B_W, F32, 1.0),
                 (v0, B_W, F32, 1.0)]
    even_segs_p = even_segs + [(k0, B_W, BF16, 1.0)]
    odd_segs = [(0, 2 * D_C, F32, 1.0)]
    zeros_conv = jnp.zeros((batch_p, CONV_W - 1, D_RNN), F32)
    zeros_h = jnp.zeros((batch_p, D_RNN), F32)

    p_conv, p_lru, p_k, p_v = [], [], [], []
    s_conv, s_lru, s_k, s_v, s_gv = [], [], [], [], []
    for l in range(DEPTH):
        if l % 2 == 0:
            e = l // 2
            lam_init = 0.8 - 0.6 * math.exp(-0.3 * l)
            pa = dict(conv_w=conv_w[e], conv_b=conv_b[e].reshape(1, D_RNN),
                      wa_bd=_block_diag(lru_wa[e]), ba=lru_ba[e].reshape(1, D_RNN),
                      wx_bd=_block_diag(lru_wx[e]), bx=lru_bx[e].reshape(1, D_RNN),
                      lam=lru_lambda[e].reshape(1, D_RNN))
            pb = dict(bias_table=rel_bias_table, lam_vecs=lam_vecs[e],
                      subln_w=subln_w[e].reshape(1, B_E))
            gx, q, k, v, k_b = _norm_matmul(yp, norm_mix_w[l], w_even_in_b, e, even_segs_p,
                                            tm=tm_p)
            ya, c_new, h_new = _mixer_a(gx, zeros_conv, zeros_h, pa, batch=batch_p, seq=seq_p,
                                        tt=256)
            yb = _attn_prompt(q, k_b, v.T.astype(BF16), pb, lam_init)
            yp = _matmul_res([ya, yb], w_even_out_b, e, yp, tm=tm_p)
            p_conv.append(c_new)
            p_lru.append(h_new.reshape(batch_p, D_RNN))
            p_k.append(k.reshape(batch_p, seq_p, B_HEADS, B_E))
            p_v.append(v.reshape(batch_p, seq_p, B_HEADS, B_E))
            gx, q, k, v = _norm_matmul(ys, norm_mix_w[l], w_even_in_b, e, even_segs, tm=tm_s)
            ya, c_new, h_new = _mixer_a(gx, state_conv[e], state_lru[e], pa, batch=batch_s,
                                        seq=seq_s, tt=seq_s)
            yb = _attn_sample(q, k, v, cache_k, cache_v, e, pb, lam_init, batch=batch_s,
                              t_new=seq_s)
            ys = _matmul_res([ya, yb], w_even_out_b, e, ys, tm=tm_s)
            s_conv.append(c_new)
            s_lru.append(h_new.reshape(batch_s, D_RNN))
            s_k.append(k.reshape(batch_s, seq_s, B_HEADS, B_E))
            s_v.append(v.reshape(batch_s, seq_s, B_HEADS, B_E))
        else:
            o = l // 2
            (z,) = _norm_matmul(yp, norm_mix_w[l], w_odd_in_b, o, odd_segs,
                                act="gelu", tm=tm_p)
            gated, _ = _gating(z, gmlp_vnorm_w[o], gmlp_ws[o], gmlp_bs[o].T, chunk=GMLP_CHUNK,
                               n_chunks=4, emit_v=False)
            yp = _matmul_res([gated], w_odd_out_b, o, yp, tm=tm_p)
            (z,) = _norm_matmul(ys, norm_mix_w[l], w_odd_in_b, o, odd_segs,
                                act="gelu", tm=tm_s)
            gated, gv = _gating(z, gmlp_vnorm_w[o], gmlp_ws[o][:, :seq_s, :seq_s],
                                gmlp_bs[o][:, :seq_s].T, chunk=seq_s, n_chunks=1, emit_v=True)
            ys = _matmul_res([gated], w_odd_out_b, o, ys, tm=tm_s)
            s_gv.append(gv.reshape(batch_s, seq_s, D_C))
        last = l == DEPTH - 1
        ys, wu_b, wd_b = _ffn(ys, norm_ffn_w[l], w_ff_up, w_ff_down, l, norm_final_w, tm=tm_s,
                              final=last, emit_bf16=True)
        yp = _ffn(yp, norm_ffn_w[l], wu_b, wd_b, 0, norm_final_w, tm=tm_p, final=last)

    return (yp.reshape(batch_p, seq_p, D_MODEL), ys.reshape(batch_s, seq_s, D_MODEL),
            jnp.stack(p_conv), jnp.stack(p_lru), jnp.stack(p_k), jnp.stack(p_v),
            jnp.stack(s_conv), jnp.stack(s_lru), jnp.stack(s_k), jnp.stack(s_v),
            jnp.stack(s_gv))
```

```python
import functools
import math

import jax
import jax.numpy as jnp
from jax import lax
from jax.experimental import pallas as pl
from jax.experimental.pallas import tpu as pltpu

D_MODEL = 2048
DEPTH = 4
CHUNK = 64
EPS = 1e-6
D_RNN = D_MODEL // 2
LRU_BLOCKS = 16
LRU_BW = D_RNN // LRU_BLOCKS
CONV_W = 4
LRU_C = 8.0
B_HEADS = 8
B_DH = 64
B_E = 2 * B_DH
B_W = B_HEADS * B_E
N_BUCKETS = 32
MAX_DIST = 512
GMLP_CHUNK = 128
D_C = D_MODEL
C_GROUPS = 16
C_GW = D_C // C_GROUPS
D_FF = 4 * D_MODEL

F32 = jnp.float32
BF16 = jnp.bfloat16
MASK_VALUE = -1e30
LOG2E = math.log2(math.e)
MIB = 1024 * 1024

_NB = N_BUCKETS // 2
_MAX_EXACT = _NB // 2
_BUCKET_THRESHOLDS = tuple(
    int(math.ceil(_MAX_EXACT * (MAX_DIST / _MAX_EXACT) ** (k / (_NB - _MAX_EXACT)) - 1e-9))
    for k in range(1, _NB - _MAX_EXACT))
_FAR_BUCKET = _NB - 1
_FAR_DIST = _BUCKET_THRESHOLDS[-1]


def _dot(a, b):
    return jnp.dot(a, b, preferred_element_type=F32)


def _dot_nt(a, b):
    return lax.dot_general(a, b, (((1,), (1,)), ((), ())), preferred_element_type=F32)


def _rms_rows(x, w):
    ms = jnp.mean(x * x, axis=-1, keepdims=True)
    return x * lax.rsqrt(ms + EPS) * w


def _gelu_tanh(x):
    c = math.sqrt(2.0 / math.pi)
    return x * (0.5 * (1.0 + jnp.tanh(c * (x + 0.044715 * (x * x * x)))))


def _sigmoid(x):
    return 1.0 / (1.0 + jnp.exp(-x))


def _params(sem, vmem_mib):
    return pltpu.CompilerParams(dimension_semantics=sem, vmem_limit_bytes=vmem_mib * MIB)


def _norm_matmul_kernel(x_ref, nw_ref, w_ref, *refs, segs, act, tm, row_chunk, emit_bf16):
    out_refs = refs[:len(segs)]
    if emit_bf16:
        wb_ref, xn_ref = refs[len(segs):]
        wb_ref[...] = w_ref[...].astype(BF16)
        w_ref = wb_ref
    else:
        xn_ref = refs[len(segs)]
    n = pl.program_id(1)

    @pl.when(n == 0)
    def _():
        def body(c, carry):
            r = pl.multiple_of(c * row_chunk, row_chunk)
            x = x_ref[pl.ds(r, row_chunk), :]
            xn_ref[pl.ds(r, row_chunk), :] = _rms_rows(x, nw_ref[...]).astype(BF16)
            return carry
        lax.fori_loop(0, tm // row_chunk, body, 0)

    acc = _dot(xn_ref[...], w_ref[...])
    if act == "gelu":
        acc = _gelu_tanh(acc)
    for (lo, hi, scale, transposed), o_ref in zip(segs, out_refs):
        def store(o_ref=o_ref, scale=scale, transposed=transposed):
            val = acc if scale == 1.0 else acc * scale
            if transposed:
                val = val.T
            o_ref[...] = val.astype(o_ref.dtype)
        if len(segs) == 1:
            store()
        else:
            pl.when((n >= lo) & (n < hi))(store)


def _norm_matmul(x, nw, w, layer, seg_defs, *, act=None, tm, tn=512, emit_bf16=False):
    m, d = x.shape
    n_total = w.shape[2]
    assert m % tm == 0 and n_total % tn == 0 and (not emit_bf16 or m == tm)
    segs, out_shapes, out_specs = [], [], []
    for first, cols, dtype, scale, transposed in seg_defs:
        assert cols % tn == 0 and first % tn == 0
        lo, nb = first // tn, cols // tn
        segs.append((lo, lo + nb, scale, transposed))
        if transposed:
            out_shapes.append(jax.ShapeDtypeStruct((cols, m), dtype))
            out_specs.append(pl.BlockSpec(
                (tn, tm), lambda i, j, lo=lo, nb=nb: (jnp.clip(j - lo, 0, nb - 1), i)))
        else:
            out_shapes.append(jax.ShapeDtypeStruct((m, cols), dtype))
            out_specs.append(pl.BlockSpec(
                (tm, tn), lambda i, j, lo=lo, nb=nb: (i, jnp.clip(j - lo, 0, nb - 1))))
    if emit_bf16:
        out_shapes.append(jax.ShapeDtypeStruct((1, d, n_total), BF16))
        out_specs.append(pl.BlockSpec((None, d, tn), lambda i, j: (0, 0, j)))
    kern = functools.partial(_norm_matmul_kernel, segs=tuple(segs), act=act, tm=tm,
                             row_chunk=min(tm, 64), emit_bf16=emit_bf16)
    return pl.pallas_call(
        kern,
        out_shape=out_shapes,
        grid=(m // tm, n_total // tn),
        in_specs=[pl.BlockSpec((tm, d), lambda i, j: (i, 0)),
                  pl.BlockSpec((1, d), lambda i, j: (0, 0)),
                  pl.BlockSpec((None, d, tn), lambda i, j: (layer, 0, j))],
        out_specs=out_specs,
        scratch_shapes=[pltpu.VMEM((tm, d), BF16)],
        compiler_params=_params(("arbitrary", "arbitrary"), 52),
        name="norm_matmul",
    )(x, nw.reshape(1, d), w)


def _matmul_res_kernel(*refs, n_in, emit_bf16):
    a_refs = refs[:n_in]
    w_refs = refs[n_in:2 * n_in]
    res_ref = refs[2 * n_in]
    o_ref = refs[2 * n_in + 1]
    acc = res_ref[...]
    if emit_bf16:
        wb_ref = refs[2 * n_in + 2]
        for b, (a_ref, w_ref) in enumerate(zip(a_refs, w_refs)):
            wb_ref[b] = w_ref[...].astype(BF16)
            acc = acc + _dot(a_ref[...], wb_ref[b])
    else:
        for a_ref, w_ref in zip(a_refs, w_refs):
            acc = acc + _dot(a_ref[...], w_ref[...])
    o_ref[...] = acc


def _matmul_res(a_list, w, layer, res, *, tm, tn=1024, emit_bf16=False):
    m, n_total = res.shape
    n_in = len(a_list)
    k_total = w.shape[1]
    kb = k_total // n_in
    assert all(a.shape == (m, kb) for a in a_list) and m % tm == 0 and n_total % tn == 0
    assert not emit_bf16 or m == tm
    in_specs = [pl.BlockSpec((tm, kb), lambda i, j: (i, 0)) for _ in a_list]
    in_specs += [pl.BlockSpec((None, kb, tn), lambda i, j, b=b: (layer, b, j))
                 for b in range(n_in)]
    in_specs += [pl.BlockSpec((tm, tn), lambda i, j: (i, j))]
    out_shape = [jax.ShapeDtypeStruct((m, n_total), F32)]
    out_specs = [pl.BlockSpec((tm, tn), lambda i, j: (i, j))]
    if emit_bf16:
        out_shape += [jax.ShapeDtypeStruct((n_in, kb, n_total), BF16)]
        out_specs += [pl.BlockSpec((n_in, kb, tn), lambda i, j: (0, 0, j))]
    res_out = pl.pallas_call(
        functools.partial(_matmul_res_kernel, n_in=n_in, emit_bf16=emit_bf16),
        out_shape=out_shape,
        grid=(m // tm, n_total // tn),
        in_specs=in_specs,
        out_specs=out_specs,
        compiler_params=_params(("arbitrary", "arbitrary"), 48),
        name="matmul_res",
    )(*a_list, *([w] * n_in), res)
    if emit_bf16:
        return res_out[0], res_out[1].reshape(1, k_total, n_total)
    return res_out[0]


def _ffn_kernel(x_ref, nw_ref, wu_ref, wd_ref, fw_ref, o_ref, *refs, tm, row_chunk, col_chunk,
                final, emit_bf16):
    if emit_bf16:
        wub_ref, wdb_ref, xn_ref = refs
        wub_ref[...] = wu_ref[...].astype(BF16)
        wdb_ref[...] = wd_ref[...].astype(BF16)
        wu_ref, wd_ref = wub_ref, wdb_ref
    else:
        (xn_ref,) = refs
    f = pl.program_id(1)

    @pl.when(f == 0)
    def _():
        def body(c, carry):
            r = pl.multiple_of(c * row_chunk, row_chunk)
            x = x_ref[pl.ds(r, row_chunk), :]
            xn_ref[pl.ds(r, row_chunk), :] = _rms_rows(x, nw_ref[...]).astype(BF16)
            o_ref[pl.ds(r, row_chunk), :] = x
            return carry
        lax.fori_loop(0, tm // row_chunk, body, 0)

    h = _dot(xn_ref[...], wu_ref[...])
    h = jnp.maximum(h, 0.0)
    h = (h * h).astype(BF16)
    d = o_ref.shape[1]
    for c0 in range(0, d, col_chunk):
        o_ref[:, c0:c0 + col_chunk] += _dot(h, wd_ref[:, c0:c0 + col_chunk])

    if final:
        @pl.when(f == pl.num_programs(1) - 1)
        def _():
            def body(c, carry):
                r = pl.multiple_of(c * row_chunk, row_chunk)
                o_ref[pl.ds(r, row_chunk), :] = _rms_rows(o_ref[pl.ds(r, row_chunk), :], fw_ref[...])
                return carry
            lax.fori_loop(0, tm // row_chunk, body, 0)


def _ffn(x, nw, wu, wd, layer, fw, *, tm, tf=512, final=False, emit_bf16=False):
    m, d = x.shape
    dff = wu.shape[2]
    assert m % tm == 0 and dff % tf == 0 and (not emit_bf16 or m == tm)
    kern = functools.partial(_ffn_kernel, tm=tm, row_chunk=min(tm, 64), col_chunk=512, final=final,
                             emit_bf16=emit_bf16)
    out_shape = [jax.ShapeDtypeStruct((m, d), F32)]
    out_specs = [pl.BlockSpec((tm, d), lambda i, f: (i, 0))]
    if emit_bf16:
        out_shape += [jax.ShapeDtypeStruct((1, d, dff), BF16),
                      jax.ShapeDtypeStruct((1, dff, d), BF16)]
        out_specs += [pl.BlockSpec((None, d, tf), lambda i, f: (0, 0, f)),
                      pl.BlockSpec((None, tf, d), lambda i, f: (0, f, 0))]
    res = pl.pallas_call(
        kern,
        out_shape=out_shape,
        grid=(m // tm, dff // tf),
        in_specs=[pl.BlockSpec((tm, d), lambda i, f: (i, 0)),
                  pl.BlockSpec((1, d), lambda i, f: (0, 0)),
                  pl.BlockSpec((None, d, tf), lambda i, f: (layer, 0, f)),
                  pl.BlockSpec((None, tf, d), lambda i, f: (layer, f, 0)),
                  pl.BlockSpec((1, d), lambda i, f: (0, 0))],
        out_specs=out_specs,
        scratch_shapes=[pltpu.VMEM((tm, d), BF16)],
        compiler_params=_params(("arbitrary", "arbitrary"), 56),
        name="ffn",
    )(x, nw.reshape(1, d), wu, wd, fw.reshape(1, d))
    return res if emit_bf16 else res[0]


_XP_PAD = 8


def _scan_group(a, u, hb, row):
    for dist in (1, 2, 4):
        keep = row >= dist
        a_s = jnp.where(keep, pltpu.roll(a, dist, 0), 1.0)
        u_s = jnp.where(keep, pltpu.roll(u, dist, 0), 0.0)
        u = a * u_s + u
        a = a * a_s
    return a * hb + u


def _mixer_a_kernel(g_ref, xr_ref, cprev_ref, hprev_ref, cw_ref, cb_ref, wa_ref, ba_ref,
                    wx_ref, bx_ref, lam_ref, ya_ref, conv_ref, hlast_ref, xp_ref, h_ref,
                    *, tt, rc):
    t = pl.program_id(1)
    hist = CONV_W - 1

    @pl.when(t == 0)
    def _():
        xp_ref[_XP_PAD - hist:_XP_PAD, :] = cprev_ref[0]
        h_ref[...] = hprev_ref[0]

    @pl.when(t > 0)
    def _():
        xp_ref[_XP_PAD - hist:_XP_PAD, :] = xp_ref[_XP_PAD + tt - hist:_XP_PAD + tt, :]

    xp_ref[_XP_PAD:_XP_PAD + tt, :] = xr_ref[...]

    z = -lam_ref[...]
    softplus = jnp.maximum(z, 0.0) + jnp.log1p(jnp.exp(-jnp.abs(z)))
    row = lax.broadcasted_iota(jnp.int32, (8, D_RNN), 0)
    nblk = D_RNN // 256
    hb = jnp.broadcast_to(h_ref[...], (8, D_RNN))

    for c in range(tt // rc):
        r0 = c * rc
        xc = cb_ref[...]
        for j in range(CONV_W):
            s = _XP_PAD - hist + j + r0
            xc = xc + xp_ref[s:s + rc, :] * cw_ref[j:j + 1, :]
        xcb = xc.astype(BF16)
        r_lin = jnp.concatenate(
            [_dot(xcb[:, 256 * b:256 * (b + 1)], wa_ref[b]) for b in range(nblk)], axis=1)
        i_lin = jnp.concatenate(
            [_dot(xcb[:, 256 * b:256 * (b + 1)], wx_ref[b]) for b in range(nblk)], axis=1)
        r = _sigmoid(r_lin + ba_ref[...])
        i = _sigmoid(i_lin + bx_ref[...])
        log_a = (-LRU_C) * r * softplus
        a = jnp.exp(log_a)
        th = jnp.tanh(log_a)
        u = jnp.sqrt((-2.0 * th) / (1.0 - th)) * (i * xc)
        hs = []
        for k in range(rc // 8):
            hrows = _scan_group(a[8 * k:8 * k + 8], u[8 * k:8 * k + 8], hb, row)
            hb = jnp.broadcast_to(hrows[7:8, :], (8, D_RNN))
            hs.append(hrows)
        hs = jnp.concatenate(hs, axis=0)
        g = g_ref[r0:r0 + rc, :]
        ya_ref[r0:r0 + rc, :] = (_gelu_tanh(g) * hs).astype(ya_ref.dtype)

    h_ref[...] = hb[0:1, :]
    conv_ref[0] = xr_ref[tt - hist:tt, :]
    hlast_ref[0] = hb[0:1, :]


def _mixer_a(gx, conv_prev, h_prev, p, *, batch, seq, tt):
    assert seq % tt == 0 and tt % 8 == 0
    rc = min(tt, 64)
    nt = seq // tt
    kern = functools.partial(_mixer_a_kernel, tt=tt, rc=rc)
    row_blk = lambda b, t: (b * nt + t, 0)
    vec = lambda b, t: (0, 0)
    return pl.pallas_call(
        kern,
        out_shape=[jax.ShapeDtypeStruct((batch * seq, D_RNN), BF16),
                   jax.ShapeDtypeStruct((batch, CONV_W - 1, D_RNN), F32),
                   jax.ShapeDtypeStruct((batch, 1, D_RNN), F32)],
        grid=(batch, nt),
        in_specs=[pl.BlockSpec((tt, D_RNN), row_blk),
                  pl.BlockSpec((tt, D_RNN), lambda b, t: (b * nt + t, 1)),
                  pl.BlockSpec((1, CONV_W - 1, D_RNN), lambda b, t: (b, 0, 0)),
                  pl.BlockSpec((1, 1, D_RNN), lambda b, t: (b, 0, 0)),
                  pl.BlockSpec((CONV_W, D_RNN), vec),
                  pl.BlockSpec((1, D_RNN), vec),
                  pl.BlockSpec((D_RNN // 256, 256, 256), lambda b, t: (0, 0, 0)),
                  pl.BlockSpec((1, D_RNN), vec),
                  pl.BlockSpec((D_RNN // 256, 256, 256), lambda b, t: (0, 0, 0)),
                  pl.BlockSpec((1, D_RNN), vec),
                  pl.BlockSpec((1, D_RNN), vec)],
        out_specs=[pl.BlockSpec((tt, D_RNN), row_blk),
                   pl.BlockSpec((1, CONV_W - 1, D_RNN), lambda b, t: (b, 0, 0)),
                   pl.BlockSpec((1, 1, D_RNN), lambda b, t: (b, 0, 0))],
        scratch_shapes=[pltpu.VMEM((_XP_PAD + tt, D_RNN), F32),
                        pltpu.VMEM((1, D_RNN), F32)],
        compiler_params=_params(("arbitrary", "arbitrary"), 40),
        name="mixer_a",
    )(gx, gx, conv_prev, h_prev.reshape(batch, 1, D_RNN), p["conv_w"], p["conv_b"],
      p["wa_bd"], p["ba"], p["wx_bd"], p["bx"], p["lam"])


def _rel_bias(d, tab_ref, h):
    n = jnp.abs(d)
    large = jnp.full(d.shape, _MAX_EXACT, jnp.int32)
    for thr in _BUCKET_THRESHOLDS:
        large = large + (n >= thr).astype(jnp.int32)
    bucket = jnp.where(d > 0, _NB, 0) + jnp.where(n < _MAX_EXACT, n, large)
    val = jnp.full(d.shape, tab_ref[0, h], F32)
    for b in range(1, N_BUCKETS):
        val = jnp.where(bucket == b, tab_ref[b, h], val)
    return val


def _split_q(q):
    lane = lax.broadcasted_iota(jnp.int32, q.shape, 1)
    zero = jnp.zeros_like(q)
    return jnp.where(lane < B_DH, q, zero), jnp.where(lane >= B_DH, q, zero)


def _diff_lambda(lamv_ref, lam_init):
    lv = lamv_ref[...]
    s01 = jnp.sum(lv[0:1, :] * lv[1:2, :], axis=-1, keepdims=True)
    s23 = jnp.sum(lv[2:3, :] * lv[3:4, :], axis=-1, keepdims=True)
    return jnp.exp(s01) - jnp.exp(s23) + lam_init


def _diff_finish(o1, l1, o2, l2, lam, sub, lam_init):
    o = o1 / l1 - lam * (o2 / l2)
    return _rms_rows(o, sub) * (1.0 - lam_init)


def _attn_prompt_kernel(tab_ref, lamv_ref, subc_ref, q_ref, k_ref, vt_ref, o_ref,
                        b0_ref, b1_ref, qz_ref, s_ref, m_ref, l_ref, acc_ref, *, tq, lam_init):
    h = pl.program_id(0)
    qi = pl.program_id(1)
    far_bias = tab_ref[_FAR_BUCKET, h]

    @pl.when(qi == 0)
    def _():
        ki_ = lax.broadcasted_iota(jnp.int32, (tq, tq), 0)
        qi_ = lax.broadcasted_iota(jnp.int32, (tq, tq), 1)
        d = ki_ - qi_
        visible = (ki_ // CHUNK) <= (qi_ // CHUNK)
        b0_ref[...] = jnp.where(visible, (_rel_bias(d, tab_ref, h) - far_bias) * LOG2E,
                                MASK_VALUE)
        b1_ref[...] = (_rel_bias(d - tq, tab_ref, h) - far_bias) * LOG2E

    qt = q_ref[...].astype(F32).T
    row = lax.broadcasted_iota(jnp.int32, qt.shape, 0)
    qz_ref[:, 0:tq] = jnp.where(row < B_DH, qt, 0.0).astype(BF16)
    qz_ref[:, tq:2 * tq] = jnp.where(row >= B_DH, qt, 0.0).astype(BF16)
    m_ref[...] = jnp.full(m_ref.shape, MASK_VALUE, F32)
    l_ref[...] = jnp.zeros(l_ref.shape, F32)
    acc_ref[...] = jnp.zeros(acc_ref.shape, F32)

    def scores(ki, br):
        r = pl.multiple_of(ki * tq, tq)
        s_ref[br] = _dot(k_ref[pl.ds(r, tq), :], qz_ref[:, br * tq:(br + 1) * tq])

    def softmax_pv(ki, br, bias_ref):
        cols = slice(br * tq, (br + 1) * tq)
        r = pl.multiple_of(ki * tq, tq)
        s = s_ref[br]
        if bias_ref is not None:
            s = s + bias_ref[...]
        m_old = m_ref[:, cols]
        m_new = jnp.maximum(m_old, jnp.max(s, axis=0, keepdims=True))
        alpha = jnp.exp2(m_old - m_new)
        e = jnp.exp2(s - m_new)
        l_ref[:, cols] = alpha * l_ref[:, cols] + jnp.sum(e, axis=0, keepdims=True)
        acc_ref[:, cols] = (alpha * acc_ref[:, cols]
                            + _dot(vt_ref[:, pl.ds(r, tq)], e.astype(BF16)))
        m_ref[:, cols] = m_new

    def iteration(ki, bias_ref, prefetch):
        scores(ki, 1)
        softmax_pv(ki, 0, bias_ref)
        if prefetch:
            scores(ki + 1, 0)
        softmax_pv(ki, 1, bias_ref)

    scores(0, 0)

    def far_body(ki, carry):
        iteration(ki, None, True)
        return carry
    lax.fori_loop(0, jnp.maximum(qi - 1, 0), far_body, 0)

    @pl.when(qi >= 1)
    def _():
        iteration(qi - 1, b1_ref, True)

    iteration(qi, b0_ref, False)

    lam = _diff_lambda(lamv_ref, lam_init)
    o = (acc_ref[:, 0:tq] / l_ref[:, 0:tq]
         - lam * (acc_ref[:, tq:2 * tq] / l_ref[:, tq:2 * tq]))
    ms = jnp.mean(o * o, axis=0, keepdims=True)
    o = o * lax.rsqrt(ms + EPS) * subc_ref[...] * (1.0 - lam_init)
    o_ref[...] = o.T.astype(o_ref.dtype)


def _attn_prompt(q, k, vt, p, lam_init, *, tq=512):
    s = q.shape[0]
    assert s % tq == 0 and tq % CHUNK == 0 and tq + 1 >= _FAR_DIST
    kern = functools.partial(_attn_prompt_kernel, tq=tq, lam_init=lam_init)
    return pl.pallas_call(
        kern,
        out_shape=jax.ShapeDtypeStruct((s, B_W), BF16),
        grid=(B_HEADS, s // tq),
        in_specs=[pl.BlockSpec(memory_space=pltpu.SMEM),
                  pl.BlockSpec((4, B_DH), lambda h, i: (0, 0)),
                  pl.BlockSpec((B_E, 1), lambda h, i: (0, 0)),
                  pl.BlockSpec((tq, B_E), lambda h, i: (i, h)),
                  pl.BlockSpec((s, B_E), lambda h, i: (0, h)),
                  pl.BlockSpec((B_E, s), lambda h, i: (h, 0))],
        out_specs=pl.BlockSpec((tq, B_E), lambda h, i: (i, h)),
        scratch_shapes=[pltpu.VMEM((tq, tq), F32), pltpu.VMEM((tq, tq), F32),
                        pltpu.VMEM((B_E, 2 * tq), BF16),
                        pltpu.VMEM((2, tq, tq), F32),
                        pltpu.VMEM((1, 2 * tq), F32), pltpu.VMEM((1, 2 * tq), F32),
                        pltpu.VMEM((B_E, 2 * tq), F32)],
        compiler_params=_params(("arbitrary", "arbitrary"), 48),
        name="attn_prompt",
    )(p["bias_table"], p["lam_vecs"], p["subln_w"].reshape(B_E, 1), q, k, vt)


_SAMPLE_COL_CHUNK = 2048


def _attn_sample_kernel(tab_ref, lamv_ref, sub_ref, q_ref, kn_ref, vn_ref, kc_ref, vc_ref, o_ref,
                        bc_ref, bn_ref, s_ref, *, t_new, past, lam_init):
    rows_h = 2 * t_new
    n_keys = past * B_HEADS

    @pl.when(pl.program_id(0) == 0)
    def _():
        t_c = lax.broadcasted_iota(jnp.int32, (rows_h, n_keys), 0) % t_new
        col = lax.broadcasted_iota(jnp.int32, (rows_h, n_keys), 1)
        d_c = col // B_HEADS - past - t_c
        head_c = col % B_HEADS
        t_n = lax.broadcasted_iota(jnp.int32, (rows_h, B_HEADS * t_new), 0) % t_new
        col_n = lax.broadcasted_iota(jnp.int32, (rows_h, B_HEADS * t_new), 1)
        d_n = col_n % t_new - t_n
        head_n = col_n // t_new
        for h in range(B_HEADS):
            rows = slice(h * rows_h, (h + 1) * rows_h)
            bc_ref[rows, :] = jnp.where(head_c == h, LOG2E * _rel_bias(d_c, tab_ref, h),
                                        MASK_VALUE)
            bn_ref[rows, :] = jnp.where(head_n == h, LOG2E * _rel_bias(d_n, tab_ref, h),
                                        MASK_VALUE)

    q_parts, kn_parts, vn_parts = [], [], []
    for h in range(B_HEADS):
        cols = slice(h * B_E, (h + 1) * B_E)
        q_parts.extend(_split_q(q_ref[:, cols]))
        kn_parts.append(kn_ref[:, cols])
        vn_parts.append(vn_ref[:, cols])
    qblk = jnp.concatenate(q_parts, axis=0)
    kn = jnp.concatenate(kn_parts, axis=0).astype(BF16)
    vn = jnp.concatenate(vn_parts, axis=0).astype(BF16)

    s_n = _dot_nt(qblk, kn) + bn_ref[...]
    m = jnp.max(s_n, axis=-1, keepdims=True)
    for c0 in range(0, n_keys, _SAMPLE_COL_CHUNK):
        cc = slice(c0, c0 + _SAMPLE_COL_CHUNK)
        s = _dot_nt(qblk, kc_ref[cc, :].astype(BF16)) + bc_ref[:, cc]
        s_ref[:, cc] = s
        m = jnp.maximum(m, jnp.max(s, axis=-1, keepdims=True))

    e_n = jnp.exp2(s_n - m)
    l = jnp.sum(e_n, axis=-1, keepdims=True)
    o = _dot(e_n.astype(BF16), vn)
    for c0 in range(0, n_keys, _SAMPLE_COL_CHUNK):
        cc = slice(c0, c0 + _SAMPLE_COL_CHUNK)
        e = jnp.exp2(s_ref[:, cc] - m)
        l = l + jnp.sum(e, axis=-1, keepdims=True)
        o = o + _dot(e.astype(BF16), vc_ref[cc, :].astype(BF16))

    lam = _diff_lambda(lamv_ref, lam_init)
    for h in range(B_HEADS):
        r1 = slice(h * rows_h, h * rows_h + t_new)
        r2 = slice(h * rows_h + t_new, (h + 1) * rows_h)
        o_ref[:, h * B_E:(h + 1) * B_E] = _diff_finish(
            o[r1], l[r1], o[r2], l[r2], lam, sub_ref[...], lam_init).astype(o_ref.dtype)


def _attn_sample(q, k_new, v_new, k_cache, v_cache, layer, p, lam_init, *, batch, t_new):
    n_layers, _, past, _, _ = k_cache.shape
    n_keys = past * B_HEADS
    assert n_keys % _SAMPLE_COL_CHUNK == 0
    kern = functools.partial(_attn_sample_kernel, t_new=t_new, past=past, lam_init=lam_init)
    new_blk = pl.BlockSpec((t_new, B_W), lambda b: (b, 0))
    cache_blk = pl.BlockSpec((None, None, n_keys, B_E), lambda b: (layer, b, 0, 0))
    n_rows = B_HEADS * 2 * t_new
    return pl.pallas_call(
        kern,
        out_shape=jax.ShapeDtypeStruct((batch * t_new, B_W), BF16),
        grid=(batch,),
        in_specs=[pl.BlockSpec(memory_space=pltpu.SMEM),
                  pl.BlockSpec((4, B_DH), lambda b: (0, 0)),
                  pl.BlockSpec((1, B_E), lambda b: (0, 0)),
                  new_blk, new_blk, new_blk, cache_blk, cache_blk],
        out_specs=new_blk,
        scratch_shapes=[pltpu.VMEM((n_rows, n_keys), F32),
                        pltpu.VMEM((n_rows, B_HEADS * t_new), F32),
                        pltpu.VMEM((n_rows, n_keys), F32)],
        compiler_params=_params(("arbitrary",), 48),
        name="attn_sample",
    )(p["bias_table"], p["lam_vecs"], p["subln_w"], q, k_new, v_new,
      k_cache.reshape(n_layers, batch, n_keys, B_E), v_cache.reshape(n_layers, batch, n_keys, B_E))


def _gating_kernel(u_ref, v_ref, vnw_ref, ws_ref, bs_ref, *refs, chunk, n_chunks, emit_v):
    if emit_v:
        o_ref, vout_ref, wsm_ref = refs
    else:
        o_ref, wsm_ref = refs
        vout_ref = None

    @pl.when(pl.program_id(0) == 0)
    def _():
        ri = lax.broadcasted_iota(jnp.int32, (chunk, chunk), 0)
        ci = lax.broadcasted_iota(jnp.int32, (chunk, chunk), 1)
        for g in range(C_GROUPS):
            wsm_ref[g] = jnp.where(ci <= ri, ws_ref[g], 0.0).astype(BF16)

    for c in range(n_chunks):
        r0 = c * chunk
        vn = _rms_rows(v_ref[r0:r0 + chunk, :], vnw_ref[...])
        if emit_v:
            vout_ref[r0:r0 + chunk, :] = vn
        vnb = vn.astype(BF16)
        for g in range(C_GROUPS):
            cols = slice(g * C_GW, (g + 1) * C_GW)
            mix = _dot(wsm_ref[g], vnb[:, cols]) + bs_ref[:, g:g + 1]
            o_ref[r0:r0 + chunk, cols] = (u_ref[r0:r0 + chunk, cols] * mix).astype(o_ref.dtype)


def _gating(z, vnw, ws, bs_t, *, chunk, n_chunks, emit_v):
    m = z.shape[0]
    rt = chunk * n_chunks
    assert m % rt == 0
    kern = functools.partial(_gating_kernel, chunk=chunk, n_chunks=n_chunks, emit_v=emit_v)
    out_shape = [jax.ShapeDtypeStruct((m, D_C), BF16)]
    out_specs = [pl.BlockSpec((rt, D_C), lambda i: (i, 0))]
    if emit_v:
        out_shape.append(jax.ShapeDtypeStruct((m, D_C), F32))
        out_specs.append(pl.BlockSpec((rt, D_C), lambda i: (i, 0)))
    res = pl.pallas_call(
        kern,
        out_shape=out_shape,
        grid=(m // rt,),
        in_specs=[pl.BlockSpec((rt, D_C), lambda i: (i, 0)),
                  pl.BlockSpec((rt, D_C), lambda i: (i, 1)),
                  pl.BlockSpec((1, D_C), lambda i: (0, 0)),
                  pl.BlockSpec((C_GROUPS, chunk, chunk), lambda i: (0, 0, 0)),
                  pl.BlockSpec((chunk, C_GROUPS), lambda i: (0, 0))],
        out_specs=out_specs,
        scratch_shapes=[pltpu.VMEM((C_GROUPS, chunk, chunk), BF16)],
        compiler_params=_params(("arbitrary",), 40),
        name="gating",
    )(z, z, vnw.reshape(1, D_C), ws, bs_t)
    return res if emit_v else (res[0], None)


def _block_diag(w):
    per = 256 // LRU_BW
    w4 = w.reshape(LRU_BLOCKS // per, per, LRU_BW, LRU_BW)
    eye = jnp.eye(per, dtype=w.dtype)
    bd = jnp.einsum("gaij,ab->gaibj", w4, eye)
    return bd.reshape(LRU_BLOCKS // per, 256, 256).astype(BF16)


def kernel(x_prompt, x_sample, state_conv, state_lru, cache_k, cache_v, w_even_in, w_even_out,
           conv_w, conv_b, lru_wa, lru_ba, lru_wx, lru_bx, lru_lambda, lam_vecs, subln_w,
           rel_bias_table, w_odd_in, w_odd_out, gmlp_vnorm_w, gmlp_ws, gmlp_bs, norm_mix_w,
           norm_ffn_w, norm_final_w, w_ff_up, w_ff_down):
    batch_p, seq_p, _ = x_prompt.shape
    batch_s, seq_s, _ = x_sample.shape
    assert batch_p == 1
    yp = x_prompt.reshape(batch_p * seq_p, D_MODEL)
    ys = x_sample.reshape(batch_s * seq_s, D_MODEL)
    tm_p = 1024
    tm_s = batch_s * seq_s
    scale = B_DH ** -0.5 * LOG2E

    q0, k0, v0 = 2 * D_RNN, 2 * D_RNN + B_W, 2 * D_RNN + 2 * B_W
    even_segs = [(0, 2 * D_RNN, F32, 1.0, False), (q0, B_W, BF16, scale, False),
                 (k0, B_W, F32, 1.0, False), (v0, B_W, F32, 1.0, False)]
    even_segs_p = even_segs + [(k0, B_W, BF16, 1.0, False), (v0, B_W, BF16, 1.0, True)]
    odd_segs = [(0, 2 * D_C, F32, 1.0, False)]
    zeros_conv = jnp.zeros((batch_p, CONV_W - 1, D_RNN), F32)
    zeros_h = jnp.zeros((batch_p, D_RNN), F32)

    p_conv, p_lru, p_k, p_v = [], [], [], []
    s_conv, s_lru, s_k, s_v, s_gv = [], [], [], [], []
    for l in range(DEPTH):
        if l % 2 == 0:
            e = l // 2
            lam_init = 0.8 - 0.6 * math.exp(-0.3 * l)
            pa = dict(conv_w=conv_w[e], conv_b=conv_b[e].reshape(1, D_RNN),
                      wa_bd=_block_diag(lru_wa[e]), ba=lru_ba[e].reshape(1, D_RNN),
                      wx_bd=_block_diag(lru_wx[e]), bx=lru_bx[e].reshape(1, D_RNN),
                      lam=lru_lambda[e].reshape(1, D_RNN))
            pb = dict(bias_table=rel_bias_table, lam_vecs=lam_vecs[e],
                      subln_w=subln_w[e].reshape(1, B_E))
            gx, q, k, v, w_in_b = _norm_matmul(ys, norm_mix_w[l], w_even_in, e, even_segs,
                                               tm=tm_s, emit_bf16=True)
            ya, c_new, h_new = _mixer_a(gx, state_conv[e], state_lru[e], pa, batch=batch_s,
                                        seq=seq_s, tt=seq_s)
            yb = _attn_sample(q, k, v, cache_k, cache_v, e, pb, lam_init, batch=batch_s,
                              t_new=seq_s)
            ys, w_out_b = _matmul_res([ya, yb], w_even_out, e, ys, tm=tm_s, emit_bf16=True)
            s_conv.append(c_new)
            s_lru.append(h_new.reshape(batch_s, D_RNN))
            s_k.append(k.reshape(batch_s, seq_s, B_HEADS, B_E))
            s_v.append(v.reshape(batch_s, seq_s, B_HEADS, B_E))
            gx, q, k, v, k_b, vt_b = _norm_matmul(yp, norm_mix_w[l], w_in_b, 0, even_segs_p,
                                                  tm=tm_p)
            ya, c_new, h_new = _mixer_a(gx, zeros_conv, zeros_h, pa, batch=batch_p, seq=seq_p,
                                        tt=256)
            yb = _attn_prompt(q, k_b, vt_b, pb, lam_init)
            yp = _matmul_res([ya, yb], w_out_b, 0, yp, tm=tm_p)
            p_conv.append(c_new)
            p_lru.append(h_new.reshape(batch_p, D_RNN))
            p_k.append(k.reshape(batch_p, seq_p, B_HEADS, B_E))
            p_v.append(v.reshape(batch_p, seq_p, B_HEADS, B_E))
        else:
            o = l // 2
            z, w_in_b = _norm_matmul(ys, norm_mix_w[l], w_odd_in, o, odd_segs, act="gelu",
                                     tm=tm_s, emit_bf16=True)
            gated, gv = _gating(z, gmlp_vnorm_w[o], gmlp_ws[o][:, :seq_s, :seq_s],
                                gmlp_bs[o][:, :seq_s].T, chunk=seq_s, n_chunks=1, emit_v=True)
            ys, w_out_b = _matmul_res([gated], w_odd_out, o, ys, tm=tm_s, emit_bf16=True)
            s_gv.append(gv.reshape(batch_s, seq_s, D_C))
            (z,) = _norm_matmul(yp, norm_mix_w[l], w_in_b, 0, odd_segs, act="gelu", tm=tm_p,
                                tn=1024)
            gated, _ = _gating(z, gmlp_vnorm_w[o], gmlp_ws[o], gmlp_bs[o].T, chunk=GMLP_CHUNK,
                               n_chunks=4, emit_v=False)
            yp = _matmul_res([gated], w_out_b, 0, yp, tm=tm_p)
        last = l == DEPTH - 1
        ys, wu_b, wd_b = _ffn(ys, norm_ffn_w[l], w_ff_up, w_ff_down, l, norm_final_w, tm=tm_s,
                              final=last, emit_bf16=True)
        yp = _ffn(yp, norm_ffn_w[l], wu_b, wd_b, 0, norm_final_w, tm=tm_p, final=last)

    return (yp.reshape(batch_p, seq_p, D_MODEL), ys.reshape(batch_s, seq_s, D_MODEL),
            jnp.stack(p_conv), jnp.stack(p_lru), jnp.stack(p_k), jnp.stack(p_v),
            jnp.stack(s_conv), jnp.stack(s_lru), jnp.stack(s_k), jnp.stack(s_v),
            jnp.stack(s_gv))
```

```python
import functools
import math

import jax
import jax.numpy as jnp
from jax import lax
from jax.experimental import pallas as pl
from jax.experimental.pallas import tpu as pltpu

D_MODEL = 2048
DEPTH = 4
CHUNK = 64
EPS = 1e-6
D_RNN = D_MODEL // 2
LRU_BLOCKS = 16
LRU_BW = D_RNN // LRU_BLOCKS
CONV_W = 4
LRU_C = 8.0
B_HEADS = 8
B_DH = 64
B_E = 2 * B_DH
B_W = B_HEADS * B_E
N_BUCKETS = 32
MAX_DIST = 512
GMLP_CHUNK = 128
D_C = D_MODEL
C_GROUPS = 16
C_GW = D_C // C_GROUPS
D_FF = 4 * D_MODEL

F32 = jnp.float32
BF16 = jnp.bfloat16
MASK_VALUE = -1e30
LOG2E = math.log2(math.e)
MIB = 1024 * 1024

_NB = N_BUCKETS // 2
_MAX_EXACT = _NB // 2
_BUCKET_THRESHOLDS = tuple(
    int(math.ceil(_MAX_EXACT * (MAX_DIST / _MAX_EXACT) ** (k / (_NB - _MAX_EXACT)) - 1e-9))
    for k in range(1, _NB - _MAX_EXACT))
_FAR_BUCKET = _NB - 1
_FAR_DIST = _BUCKET_THRESHOLDS[-1]


def _dot(a, b):
    return jnp.dot(a, b, preferred_element_type=F32)


def _dot_nt(a, b):
    return lax.dot_general(a, b, (((1,), (1,)), ((), ())), preferred_element_type=F32)


def _rms_rows(x, w):
    ms = jnp.mean(x * x, axis=-1, keepdims=True)
    return x * lax.rsqrt(ms + EPS) * w


def _gelu_tanh(x):
    c = math.sqrt(2.0 / math.pi)
    return x * (0.5 * (1.0 + jnp.tanh(c * (x + 0.044715 * (x * x * x)))))


def _sigmoid(x):
    return 1.0 / (1.0 + jnp.exp(-x))


def _params(sem, vmem_mib):
    return pltpu.CompilerParams(dimension_semantics=sem, vmem_limit_bytes=vmem_mib * MIB)


def _norm_matmul_kernel(x_ref, nw_ref, w_ref, *refs, segs, act, tm, row_chunk, emit_bf16):
    out_refs = refs[:len(segs)]
    if emit_bf16:
        wb_ref, xn_ref = refs[len(segs):]
        wb_ref[...] = w_ref[...].astype(BF16)
        w_ref = wb_ref
    else:
        xn_ref = refs[len(segs)]
    n = pl.program_id(1)

    @pl.when(n == 0)
    def _():
        def body(c, carry):
            r = pl.multiple_of(c * row_chunk, row_chunk)
            x = x_ref[pl.ds(r, row_chunk), :]
            xn_ref[pl.ds(r, row_chunk), :] = _rms_rows(x, nw_ref[...]).astype(BF16)
            return carry
        lax.fori_loop(0, tm // row_chunk, body, 0)

    acc = _dot(xn_ref[...], w_ref[...])
    if act == "gelu":
        acc = _gelu_tanh(acc)
    for (lo, hi, scale, transposed), o_ref in zip(segs, out_refs):
        def store(o_ref=o_ref, scale=scale, transposed=transposed):
            val = acc if scale == 1.0 else acc * scale
            if transposed:
                val = val.T
            o_ref[...] = val.astype(o_ref.dtype)
        if len(segs) == 1:
            store()
        else:
            pl.when((n >= lo) & (n < hi))(store)


def _norm_matmul(x, nw, w, layer, seg_defs, *, act=None, tm, tn=512, emit_bf16=False):
    m, d = x.shape
    n_total = w.shape[2]
    assert m % tm == 0 and n_total % tn == 0 and (not emit_bf16 or m == tm)
    segs, out_shapes, out_specs = [], [], []
    for first, cols, dtype, scale, transposed in seg_defs:
        assert cols % tn == 0 and first % tn == 0
        lo, nb = first // tn, cols // tn
        segs.append((lo, lo + nb, scale, transposed))
        if transposed:
            out_shapes.append(jax.ShapeDtypeStruct((cols, m), dtype))
            out_specs.append(pl.BlockSpec(
                (tn, tm), lambda i, j, lo=lo, nb=nb: (jnp.clip(j - lo, 0, nb - 1), i)))
        else:
            out_shapes.append(jax.ShapeDtypeStruct((m, cols), dtype))
            out_specs.append(pl.BlockSpec(
                (tm, tn), lambda i, j, lo=lo, nb=nb: (i, jnp.clip(j - lo, 0, nb - 1))))
    if emit_bf16:
        out_shapes.append(jax.ShapeDtypeStruct((1, d, n_total), BF16))
        out_specs.append(pl.BlockSpec((None, d, tn), lambda i, j: (0, 0, j)))
    kern = functools.partial(_norm_matmul_kernel, segs=tuple(segs), act=act, tm=tm,
                             row_chunk=min(tm, 64), emit_bf16=emit_bf16)
    return pl.pallas_call(
        kern,
        out_shape=out_shapes,
        grid=(m // tm, n_total // tn),
        in_specs=[pl.BlockSpec((tm, d), lambda i, j: (i, 0)),
                  pl.BlockSpec((1, d), lambda i, j: (0, 0)),
                  pl.BlockSpec((None, d, tn), lambda i, j: (layer, 0, j))],
        out_specs=out_specs,
        scratch_shapes=[pltpu.VMEM((tm, d), BF16)],
        compiler_params=_params(("arbitrary", "arbitrary"), 52),
        name="norm_matmul",
    )(x, nw.reshape(1, d), w)


def _matmul_res_kernel(*refs, n_in, emit_bf16):
    a_refs = refs[:n_in]
    w_refs = refs[n_in:2 * n_in]
    res_ref = refs[2 * n_in]
    o_ref = refs[2 * n_in + 1]
    acc = res_ref[...]
    if emit_bf16:
        wb_ref = refs[2 * n_in + 2]
        for b, (a_ref, w_ref) in enumerate(zip(a_refs, w_refs)):
            wb_ref[b] = w_ref[...].astype(BF16)
            acc = acc + _dot(a_ref[...], wb_ref[b])
    else:
        for a_ref, w_ref in zip(a_refs, w_refs):
            acc = acc + _dot(a_ref[...], w_ref[...])
    o_ref[...] = acc


def _matmul_res(a_list, w, layer, res, *, tm, tn=1024, emit_bf16=False):
    m, n_total = res.shape
    n_in = len(a_list)
    k_total = w.shape[1]
    kb = k_total // n_in
    assert all(a.shape == (m, kb) for a in a_list) and m % tm == 0 and n_total % tn == 0
    assert not emit_bf16 or m == tm
    in_specs = [pl.BlockSpec((tm, kb), lambda i, j: (i, 0)) for _ in a_list]
    in_specs += [pl.BlockSpec((None, kb, tn), lambda i, j, b=b: (layer, b, j))
                 for b in range(n_in)]
    in_specs += [pl.BlockSpec((tm, tn), lambda i, j: (i, j))]
    out_shape = [jax.ShapeDtypeStruct((m, n_total), F32)]
    out_specs = [pl.BlockSpec((tm, tn), lambda i, j: (i, j))]
    if emit_bf16:
        out_shape += [jax.ShapeDtypeStruct((n_in, kb, n_total), BF16)]
        out_specs += [pl.BlockSpec((n_in, kb, tn), lambda i, j: (0, 0, j))]
    res_out = pl.pallas_call(
        functools.partial(_matmul_res_kernel, n_in=n_in, emit_bf16=emit_bf16),
        out_shape=out_shape,
        grid=(m // tm, n_total // tn),
        in_specs=in_specs,
        out_specs=out_specs,
        compiler_params=_params(("arbitrary", "arbitrary"), 48),
        name="matmul_res",
    )(*a_list, *([w] * n_in), res)
    if emit_bf16:
        return res_out[0], res_out[1].reshape(1, k_total, n_total)
    return res_out[0]


def _ffn_kernel(x_ref, nw_ref, wu_ref, wd_ref, fw_ref, o_ref, *refs, tm, row_chunk, col_chunk,
                final, emit_bf16):
    if emit_bf16:
        wub_ref, wdb_ref, xn_ref = refs
        wub_ref[...] = wu_ref[...].astype(BF16)
        wdb_ref[...] = wd_ref[...].astype(BF16)
        wu_ref, wd_ref = wub_ref, wdb_ref
    else:
        (xn_ref,) = refs
    f = pl.program_id(1)

    @pl.when(f == 0)
    def _():
        def body(c, carry):
            r = pl.multiple_of(c * row_chunk, row_chunk)
            x = x_ref[pl.ds(r, row_chunk), :]
            xn_ref[pl.ds(r, row_chunk), :] = _rms_rows(x, nw_ref[...]).astype(BF16)
            o_ref[pl.ds(r, row_chunk), :] = x
            return carry
        lax.fori_loop(0, tm // row_chunk, body, 0)

    h = _dot(xn_ref[...], wu_ref[...])
    h = jnp.maximum(h, 0.0)
    h = (h * h).astype(BF16)
    d = o_ref.shape[1]
    for c0 in range(0, d, col_chunk):
        o_ref[:, c0:c0 + col_chunk] += _dot(h, wd_ref[:, c0:c0 + col_chunk])

    if final:
        @pl.when(f == pl.num_programs(1) - 1)
        def _():
            def body(c, carry):
                r = pl.multiple_of(c * row_chunk, row_chunk)
                o_ref[pl.ds(r, row_chunk), :] = _rms_rows(o_ref[pl.ds(r, row_chunk), :], fw_ref[...])
                return carry
            lax.fori_loop(0, tm // row_chunk, body, 0)


def _ffn(x, nw, wu, wd, layer, fw, *, tm, tf=512, final=False, emit_bf16=False):
    m, d = x.shape
    dff = wu.shape[2]
    assert m % tm == 0 and dff % tf == 0 and (not emit_bf16 or m == tm)
    kern = functools.partial(_ffn_kernel, tm=tm, row_chunk=min(tm, 64), col_chunk=512, final=final,
                             emit_bf16=emit_bf16)
    out_shape = [jax.ShapeDtypeStruct((m, d), F32)]
    out_specs = [pl.BlockSpec((tm, d), lambda i, f: (i, 0))]
    if emit_bf16:
        out_shape += [jax.ShapeDtypeStruct((1, d, dff), BF16),
                      jax.ShapeDtypeStruct((1, dff, d), BF16)]
        out_specs += [pl.BlockSpec((None, d, tf), lambda i, f: (0, 0, f)),
                      pl.BlockSpec((None, tf, d), lambda i, f: (0, f, 0))]
    res = pl.pallas_call(
        kern,
        out_shape=out_shape,
        grid=(m // tm, dff // tf),
        in_specs=[pl.BlockSpec((tm, d), lambda i, f: (i, 0)),
                  pl.BlockSpec((1, d), lambda i, f: (0, 0)),
                  pl.BlockSpec((None, d, tf), lambda i, f: (layer, 0, f)),
                  pl.BlockSpec((None, tf, d), lambda i, f: (layer, f, 0)),
                  pl.BlockSpec((1, d), lambda i, f: (0, 0))],
        out_specs=out_specs,
        scratch_shapes=[pltpu.VMEM((tm, d), BF16)],
        compiler_params=_params(("arbitrary", "arbitrary"), 56),
        name="ffn",
    )(x, nw.reshape(1, d), wu, wd, fw.reshape(1, d))
    return res if emit_bf16 else res[0]


_XP_PAD = 8


def _scan_group(a, u, hb, row):
    for dist in (1, 2, 4):
        keep = row >= dist
        a_s = jnp.where(keep, pltpu.roll(a, dist, 0), 1.0)
        u_s = jnp.where(keep, pltpu.roll(u, dist, 0), 0.0)
        u = a * u_s + u
        a = a * a_s
    return a * hb + u


def _mixer_a_kernel(g_ref, xr_ref, cprev_ref, hprev_ref, cw_ref, cb_ref, wa_ref, ba_ref,
                    wx_ref, bx_ref, lam_ref, ya_ref, conv_ref, hlast_ref, xp_ref, h_ref,
                    *, tt, rc):
    t = pl.program_id(1)
    hist = CONV_W - 1

    @pl.when(t == 0)
    def _():
        xp_ref[_XP_PAD - hist:_XP_PAD, :] = cprev_ref[0]
        h_ref[...] = hprev_ref[0]

    @pl.when(t > 0)
    def _():
        xp_ref[_XP_PAD - hist:_XP_PAD, :] = xp_ref[_XP_PAD + tt - hist:_XP_PAD + tt, :]

    xp_ref[_XP_PAD:_XP_PAD + tt, :] = xr_ref[...]

    z = -lam_ref[...]
    softplus = jnp.maximum(z, 0.0) + jnp.log1p(jnp.exp(-jnp.abs(z)))
    row = lax.broadcasted_iota(jnp.int32, (8, D_RNN), 0)
    nblk = D_RNN // 256
    hb = jnp.broadcast_to(h_ref[...], (8, D_RNN))

    for c in range(tt // rc):
        r0 = c * rc
        xc = cb_ref[...]
        for j in range(CONV_W):
            s = _XP_PAD - hist + j + r0
            xc = xc + xp_ref[s:s + rc, :] * cw_ref[j:j + 1, :]
        xcb = xc.astype(BF16)
        r_lin = jnp.concatenate(
            [_dot(xcb[:, 256 * b:256 * (b + 1)], wa_ref[b]) for b in range(nblk)], axis=1)
        i_lin = jnp.concatenate(
            [_dot(xcb[:, 256 * b:256 * (b + 1)], wx_ref[b]) for b in range(nblk)], axis=1)
        r = _sigmoid(r_lin + ba_ref[...])
        i = _sigmoid(i_lin + bx_ref[...])
        log_a = (-LRU_C) * r * softplus
        a = jnp.exp(log_a)
        th = jnp.tanh(log_a)
        u = jnp.sqrt((-2.0 * th) / (1.0 - th)) * (i * xc)
        hs = []
        for k in range(rc // 8):
            hrows = _scan_group(a[8 * k:8 * k + 8], u[8 * k:8 * k + 8], hb, row)
            hb = jnp.broadcast_to(hrows[7:8, :], (8, D_RNN))
            hs.append(hrows)
        hs = jnp.concatenate(hs, axis=0)
        g = g_ref[r0:r0 + rc, :]
        ya_ref[r0:r0 + rc, :] = (_gelu_tanh(g) * hs).astype(ya_ref.dtype)

    h_ref[...] = hb[0:1, :]
    conv_ref[0] = xr_ref[tt - hist:tt, :]
    hlast_ref[0] = hb[0:1, :]


def _mixer_a(gx, conv_prev, h_prev, p, *, batch, seq, tt):
    assert seq % tt == 0 and tt % 8 == 0
    rc = min(tt, 64)
    nt = seq // tt
    kern = functools.partial(_mixer_a_kernel, tt=tt, rc=rc)
    row_blk = lambda b, t: (b * nt + t, 0)
    vec = lambda b, t: (0, 0)
    return pl.pallas_call(
        kern,
        out_shape=[jax.ShapeDtypeStruct((batch * seq, D_RNN), BF16),
                   jax.ShapeDtypeStruct((batch, CONV_W - 1, D_RNN), F32),
                   jax.ShapeDtypeStruct((batch, 1, D_RNN), F32)],
        grid=(batch, nt),
        in_specs=[pl.BlockSpec((tt, D_RNN), row_blk),
                  pl.BlockSpec((tt, D_RNN), lambda b, t: (b * nt + t, 1)),
                  pl.BlockSpec((1, CONV_W - 1, D_RNN), lambda b, t: (b, 0, 0)),
                  pl.BlockSpec((1, 1, D_RNN), lambda b, t: (b, 0, 0)),
                  pl.BlockSpec((CONV_W, D_RNN), vec),
                  pl.BlockSpec((1, D_RNN), vec),
                  pl.BlockSpec((D_RNN // 256, 256, 256), lambda b, t: (0, 0, 0)),
                  pl.BlockSpec((1, D_RNN), vec),
                  pl.BlockSpec((D_RNN // 256, 256, 256), lambda b, t: (0, 0, 0)),
                  pl.BlockSpec((1, D_RNN), vec),
                  pl.BlockSpec((1, D_RNN), vec)],
        out_specs=[pl.BlockSpec((tt, D_RNN), row_blk),
                   pl.BlockSpec((1, CONV_W - 1, D_RNN), lambda b, t: (b, 0, 0)),
                   pl.BlockSpec((1, 1, D_RNN), lambda b, t: (b, 0, 0))],
        scratch_shapes=[pltpu.VMEM((_XP_PAD + tt, D_RNN), F32),
                        pltpu.VMEM((1, D_RNN), F32)],
        compiler_params=_params(("arbitrary", "arbitrary"), 40),
        name="mixer_a",
    )(gx, gx, conv_prev, h_prev.reshape(batch, 1, D_RNN), p["conv_w"], p["conv_b"],
      p["wa_bd"], p["ba"], p["wx_bd"], p["bx"], p["lam"])


def _rel_bias(d, tab_ref, h):
    n = jnp.abs(d)
    large = jnp.full(d.shape, _MAX_EXACT, jnp.int32)
    for thr in _BUCKET_THRESHOLDS:
        large = large + (n >= thr).astype(jnp.int32)
    bucket = jnp.where(d > 0, _NB, 0) + jnp.where(n < _MAX_EXACT, n, large)
    val = jnp.full(d.shape, tab_ref[0, h], F32)
    for b in range(1, N_BUCKETS):
        val = jnp.where(bucket == b, tab_ref[b, h], val)
    return val


def _split_q(q):
    lane = lax.broadcasted_iota(jnp.int32, q.shape, 1)
    zero = jnp.zeros_like(q)
    return jnp.where(lane < B_DH, q, zero), jnp.where(lane >= B_DH, q, zero)


def _diff_lambda(lamv_ref, lam_init):
    lv = lamv_ref[...]
    s01 = jnp.sum(lv[0:1, :] * lv[1:2, :], axis=-1, keepdims=True)
    s23 = jnp.sum(lv[2:3, :] * lv[3:4, :], axis=-1, keepdims=True)
    return jnp.exp(s01) - jnp.exp(s23) + lam_init


def _diff_finish(o1, l1, o2, l2, lam, sub, lam_init):
    o = o1 / l1 - lam * (o2 / l2)
    return _rms_rows(o, sub) * (1.0 - lam_init)


def _attn_prompt_kernel(tab_ref, lamv_ref, subc_ref, q_ref, k_ref, vt_ref, o_ref,
                        b0_ref, b1_ref, qz_ref, s_ref, m_ref, l_ref, acc_ref, *, tq, tk, lam_init):
    h = pl.program_id(0)
    qi = pl.program_id(1)
    nq = tq // tk
    far_bias = tab_ref[_FAR_BUCKET, h]

    @pl.when(qi == 0)
    def _():
        ki_ = lax.broadcasted_iota(jnp.int32, (tk, tk), 0)
        qi_ = lax.broadcasted_iota(jnp.int32, (tk, tk), 1)
        d = ki_ - qi_
        visible = (ki_ // CHUNK) <= (qi_ // CHUNK)
        b0_ref[...] = jnp.where(visible, (_rel_bias(d, tab_ref, h) - far_bias) * LOG2E,
                                MASK_VALUE)
        b1_ref[...] = (_rel_bias(d - tk, tab_ref, h) - far_bias) * LOG2E

    qt = q_ref[...].astype(F32).T
    row = lax.broadcasted_iota(jnp.int32, qt.shape, 0)
    qz_ref[:, 0:tq] = jnp.where(row < B_DH, qt, 0.0).astype(BF16)
    qz_ref[:, tq:2 * tq] = jnp.where(row >= B_DH, qt, 0.0).astype(BF16)
    m_ref[...] = jnp.full(m_ref.shape, MASK_VALUE, F32)
    l_ref[...] = jnp.zeros(l_ref.shape, F32)
    acc_ref[...] = jnp.zeros(acc_ref.shape, F32)

    def scores(ki, br, c, slot):
        r = pl.multiple_of(ki * tk, tk)
        c0 = br * tq + c * tk
        s_ref[slot] = _dot(k_ref[pl.ds(r, tk), :], qz_ref[:, c0:c0 + tk])

    def softmax_pv(ki, br, c, slot, bias_ref):
        c0 = br * tq + c * tk
        cols = slice(c0, c0 + tk)
        r = pl.multiple_of(ki * tk, tk)
        s = s_ref[slot]
        if bias_ref is not None:
            s = s + bias_ref[...]
        m_old = m_ref[:, cols]
        m_new = jnp.maximum(m_old, jnp.max(s, axis=0, keepdims=True))
        alpha = jnp.exp2(m_old - m_new)
        e = jnp.exp2(s - m_new)
        l_ref[:, cols] = alpha * l_ref[:, cols] + jnp.sum(e, axis=0, keepdims=True)
        acc_ref[:, cols] = (alpha * acc_ref[:, cols]
                            + _dot(vt_ref[:, pl.ds(r, tk)], e.astype(BF16)))
        m_ref[:, cols] = m_new

    def run_tile(ki, items, after):
        assert len(items) % 2 == 0
        for idx, (br, c, bias_ref) in enumerate(items):
            slot = idx % 2
            if idx + 1 < len(items):
                scores(ki, items[idx + 1][0], items[idx + 1][1], 1 - slot)
            elif after is not None:
                scores(ki + 1, 0, after, 1 - slot)
            softmax_pv(ki, br, c, slot, bias_ref)

    def block_bias(c, j):
        return b0_ref if c == j else b1_ref if c == j + 1 else None

    scores(0, 0, 0, 0)

    def far_body(ki, carry):
        run_tile(ki, [(br, c, None) for br in range(2) for c in range(nq)], 0)
        return carry
    lax.fori_loop(0, jnp.maximum(nq * qi - 1, 0), far_body, 0)

    @pl.when(qi >= 1)
    def _():
        run_tile(nq * qi - 1, [(br, c, block_bias(c, -1)) for br in range(2)
                               for c in range(nq)], 0)

    for j in range(nq):
        run_tile(nq * qi + j, [(br, c, block_bias(c, j)) for br in range(2)
                               for c in range(j, nq)], j + 1 if j + 1 < nq else None)

    lam = _diff_lambda(lamv_ref, lam_init)
    o = (acc_ref[:, 0:tq] / l_ref[:, 0:tq]
         - lam * (acc_ref[:, tq:2 * tq] / l_ref[:, tq:2 * tq]))
    ms = jnp.mean(o * o, axis=0, keepdims=True)
    o = o * lax.rsqrt(ms + EPS) * subc_ref[...] * (1.0 - lam_init)
    o_ref[...] = o.T.astype(o_ref.dtype)


def _attn_prompt(q, k, vt, p, lam_init, *, tq=2048, tk=512):
    s = q.shape[0]
    assert s % tq == 0 and tq % tk == 0 and tk % CHUNK == 0 and tk + 1 >= _FAR_DIST
    kern = functools.partial(_attn_prompt_kernel, tq=tq, tk=tk, lam_init=lam_init)
    return pl.pallas_call(
        kern,
        out_shape=jax.ShapeDtypeStruct((s, B_W), BF16),
        grid=(B_HEADS, s // tq),
        in_specs=[pl.BlockSpec(memory_space=pltpu.SMEM),
                  pl.BlockSpec((4, B_DH), lambda h, i: (0, 0)),
                  pl.BlockSpec((B_E, 1), lambda h, i: (0, 0)),
                  pl.BlockSpec((tq, B_E), lambda h, i: (i, h)),
                  pl.BlockSpec((s, B_E), lambda h, i: (0, h)),
                  pl.BlockSpec((B_E, s), lambda h, i: (h, 0))],
        out_specs=pl.BlockSpec((tq, B_E), lambda h, i: (i, h)),
        scratch_shapes=[pltpu.VMEM((tk, tk), F32), pltpu.VMEM((tk, tk), F32),
                        pltpu.VMEM((B_E, 2 * tq), BF16),
                        pltpu.VMEM((2, tk, tk), F32),
                        pltpu.VMEM((1, 2 * tq), F32), pltpu.VMEM((1, 2 * tq), F32),
                        pltpu.VMEM((B_E, 2 * tq), F32)],
        compiler_params=_params(("arbitrary", "arbitrary"), 48),
        name="attn_prompt",
    )(p["bias_table"], p["lam_vecs"], p["subln_w"].reshape(B_E, 1), q, k, vt)


_SAMPLE_COL_CHUNK = 2048


def _attn_sample_kernel(tab_ref, lamv_ref, sub_ref, q_ref, kn_ref, vn_ref, kc_ref, vc_ref, o_ref,
                        bc_ref, bn_ref, s_ref, *, t_new, past, lam_init):
    rows_h = 2 * t_new
    n_keys = past * B_HEADS

    @pl.when(pl.program_id(0) == 0)
    def _():
        t_c = lax.broadcasted_iota(jnp.int32, (rows_h, n_keys), 0) % t_new
        col = lax.broadcasted_iota(jnp.int32, (rows_h, n_keys), 1)
        d_c = col // B_HEADS - past - t_c
        head_c = col % B_HEADS
        t_n = lax.broadcasted_iota(jnp.int32, (rows_h, B_HEADS * t_new), 0) % t_new
        col_n = lax.broadcasted_iota(jnp.int32, (rows_h, B_HEADS * t_new), 1)
        d_n = col_n % t_new - t_n
        head_n = col_n // t_new
        for h in range(B_HEADS):
            rows = slice(h * rows_h, (h + 1) * rows_h)
            bc_ref[rows, :] = jnp.where(head_c == h, LOG2E * _rel_bias(d_c, tab_ref, h),
                                        MASK_VALUE)
            bn_ref[rows, :] = jnp.where(head_n == h, LOG2E * _rel_bias(d_n, tab_ref, h),
                                        MASK_VALUE)

    q_parts, kn_parts, vn_parts = [], [], []
    for h in range(B_HEADS):
        cols = slice(h * B_E, (h + 1) * B_E)
        q_parts.extend(_split_q(q_ref[:, cols]))
        kn_parts.append(kn_ref[:, cols])
        vn_parts.append(vn_ref[:, cols])
    qblk = jnp.concatenate(q_parts, axis=0)
    kn = jnp.concatenate(kn_parts, axis=0).astype(BF16)
    vn = jnp.concatenate(vn_parts, axis=0).astype(BF16)

    s_n = _dot_nt(qblk, kn) + bn_ref[...]
    m = jnp.max(s_n, axis=-1, keepdims=True)
    for c0 in range(0, n_keys, _SAMPLE_COL_CHUNK):
        cc = slice(c0, c0 + _SAMPLE_COL_CHUNK)
        s = _dot_nt(qblk, kc_ref[cc, :].astype(BF16)) + bc_ref[:, cc]
        s_ref[:, cc] = s
        m = jnp.maximum(m, jnp.max(s, axis=-1, keepdims=True))

    e_n = jnp.exp2(s_n - m)
    l = jnp.sum(e_n, axis=-1, keepdims=True)
    o = _dot(e_n.astype(BF16), vn)
    for c0 in range(0, n_keys, _SAMPLE_COL_CHUNK):
        cc = slice(c0, c0 + _SAMPLE_COL_CHUNK)
        e = jnp.exp2(s_ref[:, cc] - m)
        l = l + jnp.sum(e, axis=-1, keepdims=True)
        o = o + _dot(e.astype(BF16), vc_ref[cc, :].astype(BF16))

    lam = _diff_lambda(lamv_ref, lam_init)
    for h in range(B_HEADS):
        r1 = slice(h * rows_h, h * rows_h + t_new)
        r2 = slice(h * rows_h + t_new, (h + 1) * rows_h)
        o_ref[:, h * B_E:(h + 1) * B_E] = _diff_finish(
            o[r1], l[r1], o[r2], l[r2], lam, sub_ref[...], lam_init).astype(o_ref.dtype)


def _attn_sample(q, k_new, v_new, k_cache, v_cache, layer, p, lam_init, *, batch, t_new):
    n_layers, _, past, _, _ = k_cache.shape
    n_keys = past * B_HEADS
    assert n_keys % _SAMPLE_COL_CHUNK == 0
    kern = functools.partial(_attn_sample_kernel, t_new=t_new, past=past, lam_init=lam_init)
    new_blk = pl.BlockSpec((t_new, B_W), lambda b: (b, 0))
    cache_blk = pl.BlockSpec((None, None, n_keys, B_E), lambda b: (layer, b, 0, 0))
    n_rows = B_HEADS * 2 * t_new
    return pl.pallas_call(
        kern,
        out_shape=jax.ShapeDtypeStruct((batch * t_new, B_W), BF16),
        grid=(batch,),
        in_specs=[pl.BlockSpec(memory_space=pltpu.SMEM),
                  pl.BlockSpec((4, B_DH), lambda b: (0, 0)),
                  pl.BlockSpec((1, B_E), lambda b: (0, 0)),
                  new_blk, new_blk, new_blk, cache_blk, cache_blk],
        out_specs=new_blk,
        scratch_shapes=[pltpu.VMEM((n_rows, n_keys), F32),
                        pltpu.VMEM((n_rows, B_HEADS * t_new), F32),
                        pltpu.VMEM((n_rows, n_keys), F32)],
        compiler_params=_params(("arbitrary",), 48),
        name="attn_sample",
    )(p["bias_table"], p["lam_vecs"], p["subln_w"], q, k_new, v_new,
      k_cache.reshape(n_layers, batch, n_keys, B_E), v_cache.reshape(n_layers, batch, n_keys, B_E))


def _gating_kernel(u_ref, v_ref, vnw_ref, ws_ref, bs_ref, *refs, chunk, n_chunks, emit_v):
    if emit_v:
        o_ref, vout_ref, wsm_ref = refs
    else:
        o_ref, wsm_ref = refs
        vout_ref = None

    @pl.when(pl.program_id(0) == 0)
    def _():
        ri = lax.broadcasted_iota(jnp.int32, (chunk, chunk), 0)
        ci = lax.broadcasted_iota(jnp.int32, (chunk, chunk), 1)
        for g in range(C_GROUPS):
            wsm_ref[g] = jnp.where(ci <= ri, ws_ref[g], 0.0).astype(BF16)

    for c in range(n_chunks):
        r0 = c * chunk
        vn = _rms_rows(v_ref[r0:r0 + chunk, :], vnw_ref[...])
        if emit_v:
            vout_ref[r0:r0 + chunk, :] = vn
        vnb = vn.astype(BF16)
        for g in range(C_GROUPS):
            cols = slice(g * C_GW, (g + 1) * C_GW)
            mix = _dot(wsm_ref[g], vnb[:, cols]) + bs_ref[:, g:g + 1]
            o_ref[r0:r0 + chunk, cols] = (u_ref[r0:r0 + chunk, cols] * mix).astype(o_ref.dtype)


def _gating(z, vnw, ws, bs_t, *, chunk, n_chunks, emit_v):
    m = z.shape[0]
    rt = chunk * n_chunks
    assert m % rt == 0
    kern = functools.partial(_gating_kernel, chunk=chunk, n_chunks=n_chunks, emit_v=emit_v)
    out_shape = [jax.ShapeDtypeStruct((m, D_C), BF16)]
    out_specs = [pl.BlockSpec((rt, D_C), lambda i: (i, 0))]
    if emit_v:
        out_shape.append(jax.ShapeDtypeStruct((m, D_C), F32))
        out_specs.append(pl.BlockSpec((rt, D_C), lambda i: (i, 0)))
    res = pl.pallas_call(
        kern,
        out_shape=out_shape,
        grid=(m // rt,),
        in_specs=[pl.BlockSpec((rt, D_C), lambda i: (i, 0)),
                  pl.BlockSpec((rt, D_C), lambda i: (i, 1)),
                  pl.BlockSpec((1, D_C), lambda i: (0, 0)),
                  pl.BlockSpec((C_GROUPS, chunk, chunk), lambda i: (0, 0, 0)),
                  pl.BlockSpec((chunk, C_GROUPS), lambda i: (0, 0))],
        out_specs=out_specs,
        scratch_shapes=[pltpu.VMEM((C_GROUPS, chunk, chunk), BF16)],
        compiler_params=_params(("arbitrary",), 40),
        name="gating",
    )(z, z, vnw.reshape(1, D_C), ws, bs_t)
    return res if emit_v else (res[0], None)


def _block_diag(w):
    per = 256 // LRU_BW
    w4 = w.reshape(LRU_BLOCKS // per, per, LRU_BW, LRU_BW)
    eye = jnp.eye(per, dtype=w.dtype)
    bd = jnp.einsum("gaij,ab->gaibj", w4, eye)
    return bd.reshape(LRU_BLOCKS // per, 256, 256).astype(BF16)


def kernel(x_prompt, x_sample, state_conv, state_lru, cache_k, cache_v, w_even_in, w_even_out,
           conv_w, conv_b, lru_wa, lru_ba, lru_wx, lru_bx, lru_lambda, lam_vecs, subln_w,
           rel_bias_table, w_odd_in, w_odd_out, gmlp_vnorm_w, gmlp_ws, gmlp_bs, norm_mix_w,
           norm_ffn_w, norm_final_w, w_ff_up, w_ff_down):
    batch_p, seq_p, _ = x_prompt.shape
    batch_s, seq_s, _ = x_sample.shape
    assert batch_p == 1
    yp = x_prompt.reshape(batch_p * seq_p, D_MODEL)
    ys = x_sample.reshape(batch_s * seq_s, D_MODEL)
    tm_p = 1024
    tm_s = batch_s * seq_s
    scale = B_DH ** -0.5 * LOG2E

    q0, k0, v0 = 2 * D_RNN, 2 * D_RNN + B_W, 2 * D_RNN + 2 * B_W
    even_segs = [(0, 2 * D_RNN, F32, 1.0, False), (q0, B_W, BF16, scale, False),
                 (k0, B_W, F32, 1.0, False), (v0, B_W, F32, 1.0, False)]
    even_segs_p = even_segs + [(k0, B_W, BF16, 1.0, False), (v0, B_W, BF16, 1.0, True)]
    odd_segs = [(0, 2 * D_C, F32, 1.0, False)]
    zeros_conv = jnp.zeros((batch_p, CONV_W - 1, D_RNN), F32)
    zeros_h = jnp.zeros((batch_p, D_RNN), F32)

    p_conv, p_lru, p_k, p_v = [], [], [], []
    s_conv, s_lru, s_k, s_v, s_gv = [], [], [], [], []
    for l in range(DEPTH):
        if l % 2 == 0:
            e = l // 2
            lam_init = 0.8 - 0.6 * math.exp(-0.3 * l)
            pa = dict(conv_w=conv_w[e], conv_b=conv_b[e].reshape(1, D_RNN),
                      wa_bd=_block_diag(lru_wa[e]), ba=lru_ba[e].reshape(1, D_RNN),
                      wx_bd=_block_diag(lru_wx[e]), bx=lru_bx[e].reshape(1, D_RNN),
                      lam=lru_lambda[e].reshape(1, D_RNN))
            pb = dict(bias_table=rel_bias_table, lam_vecs=lam_vecs[e],
                      subln_w=subln_w[e].reshape(1, B_E))
            gx, q, k, v, w_in_b = _norm_matmul(ys, norm_mix_w[l], w_even_in, e, even_segs,
                                               tm=tm_s, emit_bf16=True)
            ya, c_new, h_new = _mixer_a(gx, state_conv[e], state_lru[e], pa, batch=batch_s,
                                        seq=seq_s, tt=seq_s)
            yb = _attn_sample(q, k, v, cache_k, cache_v, e, pb, lam_init, batch=batch_s,
                              t_new=seq_s)
            ys, w_out_b = _matmul_res([ya, yb], w_even_out, e, ys, tm=tm_s, emit_bf16=True)
            s_conv.append(c_new)
            s_lru.append(h_new.reshape(batch_s, D_RNN))
            s_k.append(k.reshape(batch_s, seq_s, B_HEADS, B_E))
            s_v.append(v.reshape(batch_s, seq_s, B_HEADS, B_E))
            gx, q, k, v, k_b, vt_b = _norm_matmul(yp, norm_mix_w[l], w_in_b, 0, even_segs_p,
                                                  tm=tm_p)
            ya, c_new, h_new = _mixer_a(gx, zeros_conv, zeros_h, pa, batch=batch_p, seq=seq_p,
                                        tt=256)
            yb = _attn_prompt(q, k_b, vt_b, pb, lam_init)
            yp = _matmul_res([ya, yb], w_out_b, 0, yp, tm=tm_p)
            p_conv.append(c_new)
            p_lru.append(h_new.reshape(batch_p, D_RNN))
            p_k.append(k.reshape(batch_p, seq_p, B_HEADS, B_E))
            p_v.append(v.reshape(batch_p, seq_p, B_HEADS, B_E))
        else:
            o = l // 2
            z, w_in_b = _norm_matmul(ys, norm_mix_w[l], w_odd_in, o, odd_segs, act="gelu",
                                     tm=tm_s, emit_bf16=True)
            gated, gv = _gating(z, gmlp_vnorm_w[o], gmlp_ws[o][:, :seq_s, :seq_s],
                                gmlp_bs[o][:, :seq_s].T, chunk=seq_s, n_chunks=1, emit_v=True)
            ys, w_out_b = _matmul_res([gated], w_odd_out, o, ys, tm=tm_s, emit_bf16=True)
            s_gv.append(gv.reshape(batch_s, seq_s, D_C))
            (z,) = _norm_matmul(yp, norm_mix_w[l], w_in_b, 0, odd_segs, act="gelu", tm=tm_p,
                                tn=1024)
            gated, _ = _gating(z, gmlp_vnorm_w[o], gmlp_ws[o], gmlp_bs[o].T, chunk=GMLP_CHUNK,
                               n_chunks=4, emit_v=False)
            yp = _matmul_res([gated], w_out_b, 0, yp, tm=tm_p)
        last = l == DEPTH - 1
        ys, wu_b, wd_b = _ffn(ys, norm_ffn_w[l], w_ff_up, w_ff_down, l, norm_final_w, tm=tm_s,
                              final=last, emit_bf16=True)
        yp = _ffn(yp, norm_ffn_w[l], wu_b, wd_b, 0, norm_final_w, tm=tm_p, final=last)

    return (yp.reshape(batch_p, seq_p, D_MODEL), ys.reshape(batch_s, seq_s, D_MODEL),
            jnp.stack(p_conv), jnp.stack(p_lru), jnp.stack(p_k), jnp.stack(p_v),
            jnp.stack(s_conv), jnp.stack(s_lru), jnp.stack(s_k), jnp.stack(s_v),
            jnp.stack(s_gv))
```

```python
import functools
import math

import jax
import jax.numpy as jnp
from jax import lax
from jax.experimental import pallas as pl
from jax.experimental.pallas import tpu as pltpu

D_MODEL = 2048
DEPTH = 4
CHUNK = 64
EPS = 1e-6
D_RNN = D_MODEL // 2
LRU_BLOCKS = 16
LRU_BW = D_RNN // LRU_BLOCKS
CONV_W = 4
LRU_C = 8.0
B_HEADS = 8
B_DH = 64
B_E = 2 * B_DH
B_W = B_HEADS * B_E
N_BUCKETS = 32
MAX_DIST = 512
GMLP_CHUNK = 128
D_C = D_MODEL
C_GROUPS = 16
C_GW = D_C // C_GROUPS
D_FF = 4 * D_MODEL

F32 = jnp.float32
BF16 = jnp.bfloat16
MASK_VALUE = -1e30
LOG2E = math.log2(math.e)
MIB = 1024 * 1024

_NB = N_BUCKETS // 2
_MAX_EXACT = _NB // 2
_BUCKET_THRESHOLDS = tuple(
    int(math.ceil(_MAX_EXACT * (MAX_DIST / _MAX_EXACT) ** (k / (_NB - _MAX_EXACT)) - 1e-9))
    for k in range(1, _NB - _MAX_EXACT))
_FAR_BUCKET = _NB - 1
_FAR_DIST = _BUCKET_THRESHOLDS[-1]


def _dot(a, b):
    return jnp.dot(a, b, preferred_element_type=F32)


def _dot_nt(a, b):
    return lax.dot_general(a, b, (((1,), (1,)), ((), ())), preferred_element_type=F32)


def _rms_rows(x, w):
    ms = jnp.mean(x * x, axis=-1, keepdims=True)
    return x * lax.rsqrt(ms + EPS) * w


def _gelu_tanh(x):
    c = math.sqrt(2.0 / math.pi)
    return x * (0.5 * (1.0 + jnp.tanh(c * (x + 0.044715 * (x * x * x)))))


def _sigmoid(x):
    return 0.5 * jnp.tanh(0.5 * x) + 0.5


def _params(sem, vmem_mib):
    return pltpu.CompilerParams(dimension_semantics=sem, vmem_limit_bytes=vmem_mib * MIB)


def _norm_matmul_kernel(x_ref, nw_ref, w_ref, *refs, segs, act, tm, tn, row_chunk, emit_bf16,
                        resident_w):
    out_refs = refs[:len(segs)]
    n = pl.program_id(1)
    if emit_bf16:
        wb_ref, xn_ref = refs[len(segs):]
        wb_ref[...] = w_ref[...].astype(BF16)
        w_ref = wb_ref
    else:
        xn_ref = refs[len(segs)]
    if resident_w:
        w_ref = w_ref.at[:, pl.ds(pl.multiple_of(n * tn, tn), tn)]

    @pl.when(n == 0)
    def _():
        def body(c, carry):
            r = pl.multiple_of(c * row_chunk, row_chunk)
            x = x_ref[pl.ds(r, row_chunk), :]
            xn_ref[pl.ds(r, row_chunk), :] = _rms_rows(x, nw_ref[...]).astype(BF16)
            return carry
        lax.fori_loop(0, tm // row_chunk, body, 0)

    acc = _dot(xn_ref[...], w_ref[...])
    if act == "gelu":
        acc = _gelu_tanh(acc)
    for (lo, hi, scale, transposed), o_ref in zip(segs, out_refs):
        def store(o_ref=o_ref, scale=scale, transposed=transposed):
            val = acc if scale == 1.0 else acc * scale
            if transposed:
                val = val.T
            o_ref[...] = val.astype(o_ref.dtype)
        if len(segs) == 1:
            store()
        else:
            pl.when((n >= lo) & (n < hi))(store)


def _norm_matmul(x, nw, w, layer, seg_defs, *, act=None, tm, tn=512, emit_bf16=False,
                 resident_w=False, single_buffer_x=False):
    m, d = x.shape
    n_total = w.shape[2]
    assert m % tm == 0 and n_total % tn == 0 and (not emit_bf16 or m == tm)
    assert not (emit_bf16 and resident_w)
    segs, out_shapes, out_specs = [], [], []
    for first, cols, dtype, scale, transposed in seg_defs:
        assert cols % tn == 0 and first % tn == 0
        lo, nb = first // tn, cols // tn
        segs.append((lo, lo + nb, scale, transposed))
        if transposed:
            out_shapes.append(jax.ShapeDtypeStruct((cols, m), dtype))
            out_specs.append(pl.BlockSpec(
                (tn, tm), lambda i, j, lo=lo, nb=nb: (jnp.clip(j - lo, 0, nb - 1), i)))
        else:
            out_shapes.append(jax.ShapeDtypeStruct((m, cols), dtype))
            out_specs.append(pl.BlockSpec(
                (tm, tn), lambda i, j, lo=lo, nb=nb: (i, jnp.clip(j - lo, 0, nb - 1))))
    if emit_bf16:
        out_shapes.append(jax.ShapeDtypeStruct((1, d, n_total), BF16))
        out_specs.append(pl.BlockSpec((None, d, tn), lambda i, j: (0, 0, j)))
    kern = functools.partial(_norm_matmul_kernel, segs=tuple(segs), act=act, tm=tm, tn=tn,
                             row_chunk=min(tm, 64), emit_bf16=emit_bf16, resident_w=resident_w)
    if resident_w:
        w_spec = pl.BlockSpec((None, d, n_total), lambda i, j: (layer, 0, 0),
                              pipeline_mode=pl.Buffered(1))
    else:
        w_spec = pl.BlockSpec((None, d, tn), lambda i, j: (layer, 0, j))
    x_mode = dict(pipeline_mode=pl.Buffered(1)) if single_buffer_x else {}
    return pl.pallas_call(
        kern,
        out_shape=out_shapes,
        grid=(m // tm, n_total // tn),
        in_specs=[pl.BlockSpec((tm, d), lambda i, j: (i, 0), **x_mode),
                  pl.BlockSpec((1, d), lambda i, j: (0, 0)),
                  w_spec],
        out_specs=out_specs,
        scratch_shapes=[pltpu.VMEM((tm, d), BF16)],
        compiler_params=_params(("arbitrary", "arbitrary"), 56),
        name="norm_matmul",
    )(x, nw.reshape(1, d), w)


def _matmul_res_kernel(*refs, n_in, tn, emit_bf16, resident_w):
    a_refs = refs[:n_in]
    w_refs = refs[n_in:2 * n_in]
    res_ref = refs[2 * n_in]
    o_ref = refs[2 * n_in + 1]
    if resident_w:
        c0 = pl.multiple_of(pl.program_id(1) * tn, tn)
        w_refs = [w_ref.at[:, pl.ds(c0, tn)] for w_ref in w_refs]
    acc = res_ref[...]
    if emit_bf16:
        wb_ref = refs[2 * n_in + 2]
        for b, (a_ref, w_ref) in enumerate(zip(a_refs, w_refs)):
            wb_ref[b] = w_ref[...].astype(BF16)
            acc = acc + _dot(a_ref[...], wb_ref[b])
    else:
        for a_ref, w_ref in zip(a_refs, w_refs):
            acc = acc + _dot(a_ref[...], w_ref[...])
    o_ref[...] = acc


def _matmul_res(a_list, w, layer, res, *, tm, tn=1024, emit_bf16=False, resident_w=False):
    m, n_total = res.shape
    n_in = len(a_list)
    k_total = w.shape[1]
    kb = k_total // n_in
    assert all(a.shape == (m, kb) for a in a_list) and m % tm == 0 and n_total % tn == 0
    assert (not emit_bf16 or m == tm) and not (emit_bf16 and resident_w)
    in_specs = [pl.BlockSpec((tm, kb), lambda i, j: (i, 0)) for _ in a_list]
    if resident_w:
        in_specs += [pl.BlockSpec((None, kb, n_total), lambda i, j, b=b: (layer, b, 0),
                                  pipeline_mode=pl.Buffered(1)) for b in range(n_in)]
    else:
        in_specs += [pl.BlockSpec((None, kb, tn), lambda i, j, b=b: (layer, b, j))
                     for b in range(n_in)]
    in_specs += [pl.BlockSpec((tm, tn), lambda i, j: (i, j))]
    out_shape = [jax.ShapeDtypeStruct((m, n_total), F32)]
    out_specs = [pl.BlockSpec((tm, tn), lambda i, j: (i, j))]
    if emit_bf16:
        out_shape += [jax.ShapeDtypeStruct((n_in, kb, n_total), BF16)]
        out_specs += [pl.BlockSpec((n_in, kb, tn), lambda i, j: (0, 0, j))]
    res_out = pl.pallas_call(
        functools.partial(_matmul_res_kernel, n_in=n_in, tn=tn, emit_bf16=emit_bf16,
                          resident_w=resident_w),
        out_shape=out_shape,
        grid=(m // tm, n_total // tn),
        in_specs=in_specs,
        out_specs=out_specs,
        compiler_params=_params(("arbitrary", "arbitrary"), 48),
        name="matmul_res",
    )(*a_list, *([w] * n_in), res)
    if emit_bf16:
        return res_out[0], res_out[1].reshape(1, k_total, n_total)
    return res_out[0]


def _ffn_kernel(x_ref, nw_ref, wu_ref, wd_ref, fw_ref, o_ref, *refs, tm, row_chunk, col_chunk,
                final, emit_bf16):
    if emit_bf16:
        wub_ref, wdb_ref, xn_ref = refs
        wub_ref[...] = wu_ref[...].astype(BF16)
        wdb_ref[...] = wd_ref[...].astype(BF16)
        wu_ref, wd_ref = wub_ref, wdb_ref
    else:
        (xn_ref,) = refs
    f = pl.program_id(1)

    @pl.when(f == 0)
    def _():
        def body(c, carry):
            r = pl.multiple_of(c * row_chunk, row_chunk)
            x = x_ref[pl.ds(r, row_chunk), :]
            xn_ref[pl.ds(r, row_chunk), :] = _rms_rows(x, nw_ref[...]).astype(BF16)
            o_ref[pl.ds(r, row_chunk), :] = x
            return carry
        lax.fori_loop(0, tm // row_chunk, body, 0)

    h = _dot(xn_ref[...], wu_ref[...])
    h = jnp.maximum(h, 0.0)
    h = (h * h).astype(BF16)
    d = o_ref.shape[1]
    for c0 in range(0, d, col_chunk):
        o_ref[:, c0:c0 + col_chunk] += _dot(h, wd_ref[:, c0:c0 + col_chunk])

    if final:
        @pl.when(f == pl.num_programs(1) - 1)
        def _():
            def body(c, carry):
                r = pl.multiple_of(c * row_chunk, row_chunk)
                o_ref[pl.ds(r, row_chunk), :] = _rms_rows(o_ref[pl.ds(r, row_chunk), :], fw_ref[...])
                return carry
            lax.fori_loop(0, tm // row_chunk, body, 0)


def _ffn(x, nw, wu, wd, layer, fw, *, tm, tf=512, final=False, emit_bf16=False):
    m, d = x.shape
    dff = wu.shape[2]
    assert m % tm == 0 and dff % tf == 0 and (not emit_bf16 or m == tm)
    kern = functools.partial(_ffn_kernel, tm=tm, row_chunk=min(tm, 64), col_chunk=512, final=final,
                             emit_bf16=emit_bf16)
    out_shape = [jax.ShapeDtypeStruct((m, d), F32)]
    out_specs = [pl.BlockSpec((tm, d), lambda i, f: (i, 0))]
    if emit_bf16:
        out_shape += [jax.ShapeDtypeStruct((1, d, dff), BF16),
                      jax.ShapeDtypeStruct((1, dff, d), BF16)]
        out_specs += [pl.BlockSpec((None, d, tf), lambda i, f: (0, 0, f)),
                      pl.BlockSpec((None, tf, d), lambda i, f: (0, f, 0))]
    res = pl.pallas_call(
        kern,
        out_shape=out_shape,
        grid=(m // tm, dff // tf),
        in_specs=[pl.BlockSpec((tm, d), lambda i, f: (i, 0)),
                  pl.BlockSpec((1, d), lambda i, f: (0, 0)),
                  pl.BlockSpec((None, d, tf), lambda i, f: (layer, 0, f)),
                  pl.BlockSpec((None, tf, d), lambda i, f: (layer, f, 0)),
                  pl.BlockSpec((1, d), lambda i, f: (0, 0))],
        out_specs=out_specs,
        scratch_shapes=[pltpu.VMEM((tm, d), BF16)],
        compiler_params=_params(("arbitrary", "arbitrary"), 56),
        name="ffn",
    )(x, nw.reshape(1, d), wu, wd, fw.reshape(1, d))
    return res if emit_bf16 else res[0]


_XP_PAD = 8


def _scan_group(a, u, hb, row):
    for dist in (1, 2, 4):
        keep = row >= dist
        a_s = jnp.where(keep, pltpu.roll(a, dist, 0), 1.0)
        u_s = jnp.where(keep, pltpu.roll(u, dist, 0), 0.0)
        u = a * u_s + u
        a = a * a_s
    return a * hb + u


def _mixer_a_kernel(g_ref, xr_ref, cprev_ref, hprev_ref, cw_ref, cb_ref, wa_ref, ba_ref,
                    wx_ref, bx_ref, lam_ref, ya_ref, conv_ref, hlast_ref, xp_ref, h_ref,
                    *, tt, rc):
    t = pl.program_id(1)
    hist = CONV_W - 1

    @pl.when(t == 0)
    def _():
        xp_ref[_XP_PAD - hist:_XP_PAD, :] = cprev_ref[0]
        h_ref[...] = hprev_ref[0]

    @pl.when(t > 0)
    def _():
        xp_ref[_XP_PAD - hist:_XP_PAD, :] = xp_ref[_XP_PAD + tt - hist:_XP_PAD + tt, :]

    xp_ref[_XP_PAD:_XP_PAD + tt, :] = xr_ref[...]

    z = -lam_ref[...]
    softplus = jnp.maximum(z, 0.0) + jnp.log1p(jnp.exp(-jnp.abs(z)))
    row = lax.broadcasted_iota(jnp.int32, (8, D_RNN), 0)
    nblk = D_RNN // 256
    hb = jnp.broadcast_to(h_ref[...], (8, D_RNN))

    for c in range(tt // rc):
        r0 = c * rc
        xc = cb_ref[...]
        for j in range(CONV_W):
            s = _XP_PAD - hist + j + r0
            xc = xc + xp_ref[s:s + rc, :] * cw_ref[j:j + 1, :]
        xcb = xc.astype(BF16)
        r_lin = jnp.concatenate(
            [_dot(xcb[:, 256 * b:256 * (b + 1)], wa_ref[b]) for b in range(nblk)], axis=1)
        i_lin = jnp.concatenate(
            [_dot(xcb[:, 256 * b:256 * (b + 1)], wx_ref[b]) for b in range(nblk)], axis=1)
        r = _sigmoid(r_lin + ba_ref[...])
        i = _sigmoid(i_lin + bx_ref[...])
        log_a = (-LRU_C) * r * softplus
        a = jnp.exp(log_a)
        th = jnp.tanh(log_a)
        u = jnp.sqrt((-2.0 * th) / (1.0 - th)) * (i * xc)
        hs = []
        for k in range(rc // 8):
            hrows = _scan_group(a[8 * k:8 * k + 8], u[8 * k:8 * k + 8], hb, row)
            hb = jnp.broadcast_to(hrows[7:8, :], (8, D_RNN))
            hs.append(hrows)
        hs = jnp.concatenate(hs, axis=0)
        g = g_ref[r0:r0 + rc, :]
        ya_ref[r0:r0 + rc, :] = (_gelu_tanh(g) * hs).astype(ya_ref.dtype)

    h_ref[...] = hb[0:1, :]
    conv_ref[0] = xr_ref[tt - hist:tt, :]
    hlast_ref[0] = hb[0:1, :]


def _mixer_a(gx, conv_prev, h_prev, p, *, batch, seq, tt):
    assert seq % tt == 0 and tt % 8 == 0
    rc = min(tt, 64)
    nt = seq // tt
    kern = functools.partial(_mixer_a_kernel, tt=tt, rc=rc)
    row_blk = lambda b, t: (b * nt + t, 0)
    vec = lambda b, t: (0, 0)
    return pl.pallas_call(
        kern,
        out_shape=[jax.ShapeDtypeStruct((batch * seq, D_RNN), BF16),
                   jax.ShapeDtypeStruct((batch, CONV_W - 1, D_RNN), F32),
                   jax.ShapeDtypeStruct((batch, 1, D_RNN), F32)],
        grid=(batch, nt),
        in_specs=[pl.BlockSpec((tt, D_RNN), row_blk),
                  pl.BlockSpec((tt, D_RNN), lambda b, t: (b * nt + t, 1)),
                  pl.BlockSpec((1, CONV_W - 1, D_RNN), lambda b, t: (b, 0, 0)),
                  pl.BlockSpec((1, 1, D_RNN), lambda b, t: (b, 0, 0)),
                  pl.BlockSpec((CONV_W, D_RNN), vec),
                  pl.BlockSpec((1, D_RNN), vec),
                  pl.BlockSpec((D_RNN // 256, 256, 256), lambda b, t: (0, 0, 0)),
                  pl.BlockSpec((1, D_RNN), vec),
                  pl.BlockSpec((D_RNN // 256, 256, 256), lambda b, t: (0, 0, 0)),
                  pl.BlockSpec((1, D_RNN), vec),
                  pl.BlockSpec((1, D_RNN), vec)],
        out_specs=[pl.BlockSpec((tt, D_RNN), row_blk),
                   pl.BlockSpec((1, CONV_W - 1, D_RNN), lambda b, t: (b, 0, 0)),
                   pl.BlockSpec((1, 1, D_RNN), lambda b, t: (b, 0, 0))],
        scratch_shapes=[pltpu.VMEM((_XP_PAD + tt, D_RNN), F32),
                        pltpu.VMEM((1, D_RNN), F32)],
        compiler_params=_params(("arbitrary", "arbitrary"), 40),
        name="mixer_a",
    )(gx, gx, conv_prev, h_prev.reshape(batch, 1, D_RNN), p["conv_w"], p["conv_b"],
      p["wa_bd"], p["ba"], p["wx_bd"], p["bx"], p["lam"])


def _rel_bias(d, tab_ref, h):
    n = jnp.abs(d)
    large = jnp.full(d.shape, _MAX_EXACT, jnp.int32)
    for thr in _BUCKET_THRESHOLDS:
        large = large + (n >= thr).astype(jnp.int32)
    bucket = jnp.where(d > 0, _NB, 0) + jnp.where(n < _MAX_EXACT, n, large)
    val = jnp.full(d.shape, tab_ref[0, h], F32)
    for b in range(1, N_BUCKETS):
        val = jnp.where(bucket == b, tab_ref[b, h], val)
    return val


def _split_q(q):
    lane = lax.broadcasted_iota(jnp.int32, q.shape, 1)
    zero = jnp.zeros_like(q)
    return jnp.where(lane < B_DH, q, zero), jnp.where(lane >= B_DH, q, zero)


def _diff_lambda(lamv_ref, lam_init):
    lv = lamv_ref[...]
    s01 = jnp.sum(lv[0:1, :] * lv[1:2, :], axis=-1, keepdims=True)
    s23 = jnp.sum(lv[2:3, :] * lv[3:4, :], axis=-1, keepdims=True)
    return jnp.exp(s01) - jnp.exp(s23) + lam_init


def _diff_finish(o1, l1, o2, l2, lam, sub, lam_init):
    o = o1 / l1 - lam * (o2 / l2)
    return _rms_rows(o, sub) * (1.0 - lam_init)


def _attn_prompt_kernel(tab_ref, lamv_ref, subc_ref, q_ref, k_ref, vt_ref, o_ref,
                        b0_ref, b1_ref, qz_ref, s_ref, m_ref, l_ref, acc_ref, *, tq, tk, lam_init):
    h = pl.program_id(0)
    qi = pl.program_id(1)
    nq = tq // tk
    far_bias = tab_ref[_FAR_BUCKET, h]

    @pl.when(qi == 0)
    def _():
        ki_ = lax.broadcasted_iota(jnp.int32, (tk, tk), 0)
        qi_ = lax.broadcasted_iota(jnp.int32, (tk, tk), 1)
        d = ki_ - qi_
        visible = (ki_ // CHUNK) <= (qi_ // CHUNK)
        b0_ref[...] = jnp.where(visible, (_rel_bias(d, tab_ref, h) - far_bias) * LOG2E,
                                MASK_VALUE)
        b1_ref[...] = (_rel_bias(d - tk, tab_ref, h) - far_bias) * LOG2E

    qt = q_ref[...].astype(F32).T
    row = lax.broadcasted_iota(jnp.int32, qt.shape, 0)
    qz_ref[:, 0:tq] = jnp.where(row < B_DH, qt, 0.0).astype(BF16)
    qz_ref[:, tq:2 * tq] = jnp.where(row >= B_DH, qt, 0.0).astype(BF16)
    m_ref[...] = jnp.full(m_ref.shape, MASK_VALUE, F32)
    l_ref[...] = jnp.zeros(l_ref.shape, F32)
    acc_ref[...] = jnp.zeros(acc_ref.shape, F32)

    def scores(ki, br, c, slot):
        r = pl.multiple_of(ki * tk, tk)
        c0 = br * tq + c * tk
        s_ref[slot] = _dot(k_ref[pl.ds(r, tk), :], qz_ref[:, c0:c0 + tk])

    def softmax_pv(ki, br, c, slot, bias_ref):
        c0 = br * tq + c * tk
        cols = slice(c0, c0 + tk)
        r = pl.multiple_of(ki * tk, tk)
        s = s_ref[slot]
        if bias_ref is not None:
            s = s + bias_ref[...]
        m_old = m_ref[:, cols]
        m_new = jnp.maximum(m_old, jnp.max(s, axis=0, keepdims=True))
        alpha = jnp.exp2(m_old - m_new)
        e = jnp.exp2(s - m_new)
        l_ref[:, cols] = alpha * l_ref[:, cols] + jnp.sum(e, axis=0, keepdims=True)
        acc_ref[:, cols] = (alpha * acc_ref[:, cols]
                            + _dot(vt_ref[:, pl.ds(r, tk)], e.astype(BF16)))
        m_ref[:, cols] = m_new

    def run_tile(ki, items, after):
        assert len(items) % 2 == 0
        for idx, (br, c, bias_ref) in enumerate(items):
            slot = idx % 2
            if idx + 1 < len(items):
                scores(ki, items[idx + 1][0], items[idx + 1][1], 1 - slot)
            elif after is not None:
                scores(ki + 1, 0, after, 1 - slot)
            softmax_pv(ki, br, c, slot, bias_ref)

    def block_bias(c, j):
        return b0_ref if c == j else b1_ref if c == j + 1 else None

    scores(0, 0, 0, 0)

    def far_body(ki, carry):
        run_tile(ki, [(br, c, None) for br in range(2) for c in range(nq)], 0)
        return carry
    lax.fori_loop(0, jnp.maximum(nq * qi - 1, 0), far_body, 0)

    @pl.when(qi >= 1)
    def _():
        run_tile(nq * qi - 1, [(br, c, block_bias(c, -1)) for br in range(2)
                               for c in range(nq)], 0)

    for j in range(nq):
        run_tile(nq * qi + j, [(br, c, block_bias(c, j)) for br in range(2)
                               for c in range(j, nq)], j + 1 if j + 1 < nq else None)

    lam = _diff_lambda(lamv_ref, lam_init)
    o = (acc_ref[:, 0:tq] / l_ref[:, 0:tq]
         - lam * (acc_ref[:, tq:2 * tq] / l_ref[:, tq:2 * tq]))
    ms = jnp.mean(o * o, axis=0, keepdims=True)
    o = o * lax.rsqrt(ms + EPS) * subc_ref[...] * (1.0 - lam_init)
    o_ref[...] = o.T.astype(o_ref.dtype)


def _attn_prompt(q, k, vt, p, lam_init, *, tq=2048, tk=512):
    s = q.shape[0]
    assert s % tq == 0 and tq % tk == 0 and tk % CHUNK == 0 and tk + 1 >= _FAR_DIST
    kern = functools.partial(_attn_prompt_kernel, tq=tq, tk=tk, lam_init=lam_init)
    return pl.pallas_call(
        kern,
        out_shape=jax.ShapeDtypeStruct((s, B_W), BF16),
        grid=(B_HEADS, s // tq),
        in_specs=[pl.BlockSpec(memory_space=pltpu.SMEM),
                  pl.BlockSpec((4, B_DH), lambda h, i: (0, 0)),
                  pl.BlockSpec((B_E, 1), lambda h, i: (0, 0)),
                  pl.BlockSpec((tq, B_E), lambda h, i: (i, h)),
                  pl.BlockSpec((s, B_E), lambda h, i: (0, h)),
                  pl.BlockSpec((B_E, s), lambda h, i: (h, 0))],
        out_specs=pl.BlockSpec((tq, B_E), lambda h, i: (i, h)),
        scratch_shapes=[pltpu.VMEM((tk, tk), F32), pltpu.VMEM((tk, tk), F32),
                        pltpu.VMEM((B_E, 2 * tq), BF16),
                        pltpu.VMEM((2, tk, tk), F32),
                        pltpu.VMEM((1, 2 * tq), F32), pltpu.VMEM((1, 2 * tq), F32),
                        pltpu.VMEM((B_E, 2 * tq), F32)],
        compiler_params=_params(("arbitrary", "arbitrary"), 48),
        name="attn_prompt",
    )(p["bias_table"], p["lam_vecs"], p["subln_w"].reshape(B_E, 1), q, k, vt)


_SAMPLE_COL_CHUNK = 2048


def _attn_sample_kernel(tab_ref, lamv_ref, sub_ref, q_ref, kn_ref, vn_ref, kc_ref, vc_ref, o_ref,
                        bc_ref, bn_ref, s_ref, *, t_new, past, lam_init):
    rows_h = 2 * t_new
    n_keys = past * B_HEADS

    @pl.when(pl.program_id(0) == 0)
    def _():
        t_c = lax.broadcasted_iota(jnp.int32, (rows_h, n_keys), 0) % t_new
        col = lax.broadcasted_iota(jnp.int32, (rows_h, n_keys), 1)
        d_c = col // B_HEADS - past - t_c
        head_c = col % B_HEADS
        t_n = lax.broadcasted_iota(jnp.int32, (rows_h, B_HEADS * t_new), 0) % t_new
        col_n = lax.broadcasted_iota(jnp.int32, (rows_h, B_HEADS * t_new), 1)
        d_n = col_n % t_new - t_n
        head_n = col_n // t_new
        for h in range(B_HEADS):
            rows = slice(h * rows_h, (h + 1) * rows_h)
            bc_ref[rows, :] = jnp.where(head_c == h, LOG2E * _rel_bias(d_c, tab_ref, h),
                                        MASK_VALUE)
            bn_ref[rows, :] = jnp.where(head_n == h, LOG2E * _rel_bias(d_n, tab_ref, h),
                                        MASK_VALUE)

    q_parts, kn_parts, vn_parts = [], [], []
    for h in range(B_HEADS):
        cols = slice(h * B_E, (h + 1) * B_E)
        q_parts.extend(_split_q(q_ref[:, cols]))
        kn_parts.append(kn_ref[:, cols])
        vn_parts.append(vn_ref[:, cols])
    qblk = jnp.concatenate(q_parts, axis=0)
    kn = jnp.concatenate(kn_parts, axis=0).astype(BF16)
    vn = jnp.concatenate(vn_parts, axis=0).astype(BF16)

    s_n = _dot_nt(qblk, kn) + bn_ref[...]
    m = jnp.max(s_n, axis=-1, keepdims=True)
    for c0 in range(0, n_keys, _SAMPLE_COL_CHUNK):
        cc = slice(c0, c0 + _SAMPLE_COL_CHUNK)
        s = _dot_nt(qblk, kc_ref[cc, :].astype(BF16)) + bc_ref[:, cc]
        s_ref[:, cc] = s
        m = jnp.maximum(m, jnp.max(s, axis=-1, keepdims=True))

    e_n = jnp.exp2(s_n - m)
    l = jnp.sum(e_n, axis=-1, keepdims=True)
    o = _dot(e_n.astype(BF16), vn)
    for c0 in range(0, n_keys, _SAMPLE_COL_CHUNK):
        cc = slice(c0, c0 + _SAMPLE_COL_CHUNK)
        e = jnp.exp2(s_ref[:, cc] - m)
        l = l + jnp.sum(e, axis=-1, keepdims=True)
        o = o + _dot(e.astype(BF16), vc_ref[cc, :].astype(BF16))

    lam = _diff_lambda(lamv_ref, lam_init)
    for h in range(B_HEADS):
        r1 = slice(h * rows_h, h * rows_h + t_new)
        r2 = slice(h * rows_h + t_new, (h + 1) * rows_h)
        o_ref[:, h * B_E:(h + 1) * B_E] = _diff_finish(
            o[r1], l[r1], o[r2], l[r2], lam, sub_ref[...], lam_init).astype(o_ref.dtype)


def _attn_sample(q, k_new, v_new, k_cache, v_cache, layer, p, lam_init, *, batch, t_new):
    n_layers, _, past, _, _ = k_cache.shape
    n_keys = past * B_HEADS
    assert n_keys % _SAMPLE_COL_CHUNK == 0
    kern = functools.partial(_attn_sample_kernel, t_new=t_new, past=past, lam_init=lam_init)
    new_blk = pl.BlockSpec((t_new, B_W), lambda b: (b, 0))
    cache_blk = pl.BlockSpec((None, None, n_keys, B_E), lambda b: (layer, b, 0, 0))
    n_rows = B_HEADS * 2 * t_new
    return pl.pallas_call(
        kern,
        out_shape=jax.ShapeDtypeStruct((batch * t_new, B_W), BF16),
        grid=(batch,),
        in_specs=[pl.BlockSpec(memory_space=pltpu.SMEM),
                  pl.BlockSpec((4, B_DH), lambda b: (0, 0)),
                  pl.BlockSpec((1, B_E), lambda b: (0, 0)),
                  new_blk, new_blk, new_blk, cache_blk, cache_blk],
        out_specs=new_blk,
        scratch_shapes=[pltpu.VMEM((n_rows, n_keys), F32),
                        pltpu.VMEM((n_rows, B_HEADS * t_new), F32),
                        pltpu.VMEM((n_rows, n_keys), F32)],
        compiler_params=_params(("arbitrary",), 48),
        name="attn_sample",
    )(p["bias_table"], p["lam_vecs"], p["subln_w"], q, k_new, v_new,
      k_cache.reshape(n_layers, batch, n_keys, B_E), v_cache.reshape(n_layers, batch, n_keys, B_E))


def _gating_kernel(u_ref, v_ref, vnw_ref, ws_ref, bs_ref, *refs, chunk, n_chunks, emit_v):
    if emit_v:
        o_ref, vout_ref, wsm_ref = refs
    else:
        o_ref, wsm_ref = refs
        vout_ref = None

    @pl.when(pl.program_id(0) == 0)
    def _():
        ri = lax.broadcasted_iota(jnp.int32, (chunk, chunk), 0)
        ci = lax.broadcasted_iota(jnp.int32, (chunk, chunk), 1)
        for g in range(C_GROUPS):
            wsm_ref[g] = jnp.where(ci <= ri, ws_ref[g], 0.0).astype(BF16)

    for c in range(n_chunks):
        r0 = c * chunk
        vn = _rms_rows(v_ref[r0:r0 + chunk, :], vnw_ref[...])
        if emit_v:
            vout_ref[r0:r0 + chunk, :] = vn
        vnb = vn.astype(BF16)
        for g in range(C_GROUPS):
            cols = slice(g * C_GW, (g + 1) * C_GW)
            mix = _dot(wsm_ref[g], vnb[:, cols]) + bs_ref[:, g:g + 1]
            o_ref[r0:r0 + chunk, cols] = (u_ref[r0:r0 + chunk, cols] * mix).astype(o_ref.dtype)


def _gating(z, vnw, ws, bs_t, *, chunk, n_chunks, emit_v):
    m = z.shape[0]
    rt = chunk * n_chunks
    assert m % rt == 0
    kern = functools.partial(_gating_kernel, chunk=chunk, n_chunks=n_chunks, emit_v=emit_v)
    out_shape = [jax.ShapeDtypeStruct((m, D_C), BF16)]
    out_specs = [pl.BlockSpec((rt, D_C), lambda i: (i, 0))]
    if emit_v:
        out_shape.append(jax.ShapeDtypeStruct((m, D_C), F32))
        out_specs.append(pl.BlockSpec((rt, D_C), lambda i: (i, 0)))
    res = pl.pallas_call(
        kern,
        out_shape=out_shape,
        grid=(m // rt,),
        in_specs=[pl.BlockSpec((rt, D_C), lambda i: (i, 0)),
                  pl.BlockSpec((rt, D_C), lambda i: (i, 1)),
                  pl.BlockSpec((1, D_C), lambda i: (0, 0)),
                  pl.BlockSpec((C_GROUPS, chunk, chunk), lambda i: (0, 0, 0)),
                  pl.BlockSpec((chunk, C_GROUPS), lambda i: (0, 0))],
        out_specs=out_specs,
        scratch_shapes=[pltpu.VMEM((C_GROUPS, chunk, chunk), BF16)],
        compiler_params=_params(("arbitrary",), 40),
        name="gating",
    )(z, z, vnw.reshape(1, D_C), ws, bs_t)
    return res if emit_v else (res[0], None)


def _block_diag(w):
    per = 256 // LRU_BW
    w4 = w.reshape(LRU_BLOCKS // per, per, LRU_BW, LRU_BW)
    eye = jnp.eye(per, dtype=w.dtype)
    bd = jnp.einsum("gaij,ab->gaibj", w4, eye)
    return bd.reshape(LRU_BLOCKS // per, 256, 256).astype(BF16)


def kernel(x_prompt, x_sample, state_conv, state_lru, cache_k, cache_v, w_even_in, w_even_out,
           conv_w, conv_b, lru_wa, lru_ba, lru_wx, lru_bx, lru_lambda, lam_vecs, subln_w,
           rel_bias_table, w_odd_in, w_odd_out, gmlp_vnorm_w, gmlp_ws, gmlp_bs, norm_mix_w,
           norm_ffn_w, norm_final_w, w_ff_up, w_ff_down):
    batch_p, seq_p, _ = x_prompt.shape
    batch_s, seq_s, _ = x_sample.shape
    assert batch_p == 1
    yp = x_prompt.reshape(batch_p * seq_p, D_MODEL)
    ys = x_sample.reshape(batch_s * seq_s, D_MODEL)
    tm_p = 1024
    tm_s = batch_s * seq_s
    scale = B_DH ** -0.5 * LOG2E

    q0, k0, v0 = 2 * D_RNN, 2 * D_RNN + B_W, 2 * D_RNN + 2 * B_W
    even_segs = [(0, 2 * D_RNN, F32, 1.0, False), (q0, B_W, BF16, scale, False),
                 (k0, B_W, F32, 1.0, False), (v0, B_W, F32, 1.0, False)]
    even_segs_p = even_segs + [(k0, B_W, BF16, 1.0, False), (v0, B_W, BF16, 1.0, True)]
    odd_segs = [(0, 2 * D_C, F32, 1.0, False)]
    zeros_conv = jnp.zeros((batch_p, CONV_W - 1, D_RNN), F32)
    zeros_h = jnp.zeros((batch_p, D_RNN), F32)

    p_conv, p_lru, p_k, p_v = [], [], [], []
    s_conv, s_lru, s_k, s_v, s_gv = [], [], [], [], []
    for l in range(DEPTH):
        if l % 2 == 0:
            e = l // 2
            lam_init = 0.8 - 0.6 * math.exp(-0.3 * l)
            pa = dict(conv_w=conv_w[e], conv_b=conv_b[e].reshape(1, D_RNN),
                      wa_bd=_block_diag(lru_wa[e]), ba=lru_ba[e].reshape(1, D_RNN),
                      wx_bd=_block_diag(lru_wx[e]), bx=lru_bx[e].reshape(1, D_RNN),
                      lam=lru_lambda[e].reshape(1, D_RNN))
            pb = dict(bias_table=rel_bias_table, lam_vecs=lam_vecs[e],
                      subln_w=subln_w[e].reshape(1, B_E))
            gx, q, k, v, w_in_b = _norm_matmul(ys, norm_mix_w[l], w_even_in, e, even_segs,
                                               tm=tm_s, emit_bf16=True)
            ya, c_new, h_new = _mixer_a(gx, state_conv[e], state_lru[e], pa, batch=batch_s,
                                        seq=seq_s, tt=seq_s)
            yb = _attn_sample(q, k, v, cache_k, cache_v, e, pb, lam_init, batch=batch_s,
                              t_new=seq_s)
            ys, w_out_b = _matmul_res([ya, yb], w_even_out, e, ys, tm=tm_s, emit_bf16=True)
            s_conv.append(c_new)
            s_lru.append(h_new.reshape(batch_s, D_RNN))
            s_k.append(k.reshape(batch_s, seq_s, B_HEADS, B_E))
            s_v.append(v.reshape(batch_s, seq_s, B_HEADS, B_E))
            gx, q, k, v, k_b, vt_b = _norm_matmul(yp, norm_mix_w[l], w_in_b, 0, even_segs_p,
                                                  tm=tm_p, resident_w=True,
                                                  single_buffer_x=True)
            ya, c_new, h_new = _mixer_a(gx, zeros_conv, zeros_h, pa, batch=batch_p, seq=seq_p,
                                        tt=512)
            yb = _attn_prompt(q, k_b, vt_b, pb, lam_init)
            yp = _matmul_res([ya, yb], w_out_b, 0, yp, tm=tm_p, resident_w=True)
            p_conv.append(c_new)
            p_lru.append(h_new.reshape(batch_p, D_RNN))
            p_k.append(k.reshape(batch_p, seq_p, B_HEADS, B_E))
            p_v.append(v.reshape(batch_p, seq_p, B_HEADS, B_E))
        else:
            o = l // 2
            z, w_in_b = _norm_matmul(ys, norm_mix_w[l], w_odd_in, o, odd_segs, act="gelu",
                                     tm=tm_s, emit_bf16=True)
            gated, gv = _gating(z, gmlp_vnorm_w[o], gmlp_ws[o][:, :seq_s, :seq_s],
                                gmlp_bs[o][:, :seq_s].T, chunk=seq_s, n_chunks=1, emit_v=True)
            ys, w_out_b = _matmul_res([gated], w_odd_out, o, ys, tm=tm_s, emit_bf16=True)
            s_gv.append(gv.reshape(batch_s, seq_s, D_C))
            (z,) = _norm_matmul(yp, norm_mix_w[l], w_in_b, 0, odd_segs, act="gelu", tm=tm_p,
                                tn=1024, resident_w=True)
            gated, _ = _gating(z, gmlp_vnorm_w[o], gmlp_ws[o], gmlp_bs[o].T, chunk=GMLP_CHUNK,
                               n_chunks=4, emit_v=False)
            yp = _matmul_res([gated], w_out_b, 0, yp, tm=tm_p, resident_w=True)
        last = l == DEPTH - 1
        ys, wu_b, wd_b = _ffn(ys, norm_ffn_w[l], w_ff_up, w_ff_down, l, norm_final_w, tm=tm_s,
                              final=last, emit_bf16=True)
        yp = _ffn(yp, norm_ffn_w[l], wu_b, wd_b, 0, norm_final_w, tm=tm_p, final=last)

    return (yp.reshape(batch_p, seq_p, D_MODEL), ys.reshape(batch_s, seq_s, D_MODEL),
            jnp.stack(p_conv), jnp.stack(p_lru), jnp.stack(p_k), jnp.stack(p_v),
            jnp.stack(s_conv), jnp.stack(s_lru), jnp.stack(s_k), jnp.stack(s_v),
            jnp.stack(s_gv))
```

```python
import functools
import math

import jax
import jax.numpy as jnp
from jax import lax
from jax.experimental import pallas as pl
from jax.experimental.pallas import tpu as pltpu

D_MODEL = 2048
DEPTH = 4
CHUNK = 64
EPS = 1e-6
D_RNN = D_MODEL // 2
LRU_BLOCKS = 16
LRU_BW = D_RNN // LRU_BLOCKS
CONV_W = 4
LRU_C = 8.0
B_HEADS = 8
B_DH = 64
B_E = 2 * B_DH
B_W = B_HEADS * B_E
N_BUCKETS = 32
MAX_DIST = 512
GMLP_CHUNK = 128
D_C = D_MODEL
C_GROUPS = 16
C_GW = D_C // C_GROUPS
D_FF = 4 * D_MODEL

F32 = jnp.float32
BF16 = jnp.bfloat16
MASK_VALUE = -1e30
LOG2E = math.log2(math.e)
MIB = 1024 * 1024

_NB = N_BUCKETS // 2
_MAX_EXACT = _NB // 2
_BUCKET_THRESHOLDS = tuple(
    int(math.ceil(_MAX_EXACT * (MAX_DIST / _MAX_EXACT) ** (k / (_NB - _MAX_EXACT)) - 1e-9))
    for k in range(1, _NB - _MAX_EXACT))
_FAR_BUCKET = _NB - 1
_FAR_DIST = _BUCKET_THRESHOLDS[-1]


def _dot(a, b):
    return jnp.dot(a, b, preferred_element_type=F32)


def _dot_nt(a, b):
    return lax.dot_general(a, b, (((1,), (1,)), ((), ())), preferred_element_type=F32)


def _rms_rows(x, w):
    ms = jnp.mean(x * x, axis=-1, keepdims=True)
    return x * lax.rsqrt(ms + EPS) * w


def _gelu_tanh(x):
    c = math.sqrt(2.0 / math.pi)
    return x * (0.5 * (1.0 + jnp.tanh(c * (x + 0.044715 * (x * x * x)))))


def _sigmoid(x):
    return 0.5 * jnp.tanh(0.5 * x) + 0.5


def _params(sem, vmem_mib):
    return pltpu.CompilerParams(dimension_semantics=sem, vmem_limit_bytes=vmem_mib * MIB)


def _norm_matmul_kernel(x_ref, nw_ref, w_ref, *refs, segs, act, tm, norm_rows, emit_bf16):
    out_refs = refs[:len(segs)]
    n = pl.program_id(1)
    if emit_bf16:
        wb_ref, xn_ref = refs[len(segs):]
        wb_ref[...] = w_ref[...].astype(BF16)
        w_ref = wb_ref
    else:
        xn_ref = refs[len(segs)]

    def project(xn):
        acc = _dot(xn, w_ref[...])
        return _gelu_tanh(acc) if act == "gelu" else acc

    lo0, hi0, scale0, transposed0 = segs[0]
    assert lo0 == 0 and hi0 > 0 and not transposed0

    @pl.when(n == 0)
    def _():
        for r in range(0, tm, norm_rows):
            rows = slice(r, r + norm_rows)
            xn = _rms_rows(x_ref[rows, :], nw_ref[...]).astype(BF16)
            xn_ref[rows, :] = xn
            acc = project(xn)
            val = acc if scale0 == 1.0 else acc * scale0
            out_refs[0][rows, :] = val.astype(out_refs[0].dtype)

    @pl.when(n > 0)
    def _():
        acc = project(xn_ref[...])
        for (lo, hi, scale, transposed), o_ref in zip(segs, out_refs):
            def store(o_ref=o_ref, scale=scale, transposed=transposed):
                val = acc if scale == 1.0 else acc * scale
                if transposed:
                    val = val.T
                o_ref[...] = val.astype(o_ref.dtype)
            if len(segs) == 1:
                store()
            else:
                pl.when((n >= lo) & (n < hi))(store)


def _norm_matmul(x, nw, w, layer, seg_defs, *, act=None, tm, tn=512, emit_bf16=False):
    m, d = x.shape
    n_total = w.shape[2]
    assert m % tm == 0 and n_total % tn == 0 and (not emit_bf16 or m == tm)
    segs, out_shapes, out_specs = [], [], []
    for first, cols, dtype, scale, transposed in seg_defs:
        assert cols % tn == 0 and first % tn == 0
        lo, nb = first // tn, cols // tn
        segs.append((lo, lo + nb, scale, transposed))
        if transposed:
            out_shapes.append(jax.ShapeDtypeStruct((cols, m), dtype))
            out_specs.append(pl.BlockSpec(
                (tn, tm), lambda i, j, lo=lo, nb=nb: (jnp.clip(j - lo, 0, nb - 1), i)))
        else:
            out_shapes.append(jax.ShapeDtypeStruct((m, cols), dtype))
            out_specs.append(pl.BlockSpec(
                (tm, tn), lambda i, j, lo=lo, nb=nb: (i, jnp.clip(j - lo, 0, nb - 1))))
    if emit_bf16:
        out_shapes.append(jax.ShapeDtypeStruct((1, d, n_total), BF16))
        out_specs.append(pl.BlockSpec((None, d, tn), lambda i, j: (0, 0, j)))
    kern = functools.partial(_norm_matmul_kernel, segs=tuple(segs), act=act, tm=tm,
                             norm_rows=min(tm, 256), emit_bf16=emit_bf16)
    return pl.pallas_call(
        kern,
        out_shape=out_shapes,
        grid=(m // tm, n_total // tn),
        in_specs=[pl.BlockSpec((tm, d), lambda i, j: (i, 0)),
                  pl.BlockSpec((1, d), lambda i, j: (0, 0)),
                  pl.BlockSpec((None, d, tn), lambda i, j: (layer, 0, j))],
        out_specs=out_specs,
        scratch_shapes=[pltpu.VMEM((tm, d), BF16)],
        compiler_params=_params(("arbitrary", "arbitrary"), 56),
        name="norm_matmul",
    )(x, nw.reshape(1, d), w)


def _matmul_res_kernel(*refs, n_in, tn, emit_bf16, resident_w):
    a_refs = refs[:n_in]
    w_refs = refs[n_in:2 * n_in]
    res_ref = refs[2 * n_in]
    o_ref = refs[2 * n_in + 1]
    if resident_w:
        c0 = pl.multiple_of(pl.program_id(1) * tn, tn)
        w_refs = [w_ref.at[:, pl.ds(c0, tn)] for w_ref in w_refs]
    acc = res_ref[...]
    if emit_bf16:
        wb_ref = refs[2 * n_in + 2]
        for b, (a_ref, w_ref) in enumerate(zip(a_refs, w_refs)):
            wb_ref[b] = w_ref[...].astype(BF16)
            acc = acc + _dot(a_ref[...], wb_ref[b])
    else:
        for a_ref, w_ref in zip(a_refs, w_refs):
            acc = acc + _dot(a_ref[...], w_ref[...])
    o_ref[...] = acc


def _matmul_res(a_list, w, layer, res, *, tm, tn=1024, emit_bf16=False, resident_w=False):
    m, n_total = res.shape
    n_in = len(a_list)
    k_total = w.shape[1]
    kb = k_total // n_in
    assert all(a.shape == (m, kb) for a in a_list) and m % tm == 0 and n_total % tn == 0
    assert (not emit_bf16 or m == tm) and not (emit_bf16 and resident_w)
    in_specs = [pl.BlockSpec((tm, kb), lambda i, j: (i, 0)) for _ in a_list]
    if resident_w:
        in_specs += [pl.BlockSpec((None, kb, n_total), lambda i, j, b=b: (layer, b, 0),
                                  pipeline_mode=pl.Buffered(1)) for b in range(n_in)]
    else:
        in_specs += [pl.BlockSpec((None, kb, tn), lambda i, j, b=b: (layer, b, j))
                     for b in range(n_in)]
    in_specs += [pl.BlockSpec((tm, tn), lambda i, j: (i, j))]
    out_shape = [jax.ShapeDtypeStruct((m, n_total), F32)]
    out_specs = [pl.BlockSpec((tm, tn), lambda i, j: (i, j))]
    if emit_bf16:
        out_shape += [jax.ShapeDtypeStruct((n_in, kb, n_total), BF16)]
        out_specs += [pl.BlockSpec((n_in, kb, tn), lambda i, j: (0, 0, j))]
    res_out = pl.pallas_call(
        functools.partial(_matmul_res_kernel, n_in=n_in, tn=tn, emit_bf16=emit_bf16,
                          resident_w=resident_w),
        out_shape=out_shape,
        grid=(m // tm, n_total // tn),
        in_specs=in_specs,
        out_specs=out_specs,
        compiler_params=_params(("arbitrary", "arbitrary"), 48),
        name="matmul_res",
    )(*a_list, *([w] * n_in), res)
    if emit_bf16:
        return res_out[0], res_out[1].reshape(1, k_total, n_total)
    return res_out[0]


def _ffn_kernel(x_ref, nw_ref, wu_ref, wd_ref, fw_ref, o_ref, *refs, tm, norm_rows, row_chunk,
                col_chunk, final, emit_bf16):
    if emit_bf16:
        wub_ref, wdb_ref, xn_ref = refs
        wub_ref[...] = wu_ref[...].astype(BF16)
        wdb_ref[...] = wd_ref[...].astype(BF16)
        wu_ref, wd_ref = wub_ref, wdb_ref
    else:
        (xn_ref,) = refs
    f = pl.program_id(1)
    d = o_ref.shape[1]

    def mlp(xn):
        h = jnp.maximum(_dot(xn, wu_ref[...]), 0.0)
        return (h * h).astype(BF16)

    @pl.when(f == 0)
    def _():
        for r in range(0, tm, norm_rows):
            rows = slice(r, r + norm_rows)
            x = x_ref[rows, :]
            xn = _rms_rows(x, nw_ref[...]).astype(BF16)
            xn_ref[rows, :] = xn
            h = mlp(xn)
            for c0 in range(0, d, col_chunk):
                cols = slice(c0, c0 + col_chunk)
                o_ref[rows, cols] = x[:, cols] + _dot(h, wd_ref[:, cols])

    @pl.when(f > 0)
    def _():
        h = mlp(xn_ref[...])
        for c0 in range(0, d, col_chunk):
            o_ref[:, c0:c0 + col_chunk] += _dot(h, wd_ref[:, c0:c0 + col_chunk])

    if final:
        @pl.when(f == pl.num_programs(1) - 1)
        def _():
            def body(c, carry):
                r = pl.multiple_of(c * row_chunk, row_chunk)
                o_ref[pl.ds(r, row_chunk), :] = _rms_rows(o_ref[pl.ds(r, row_chunk), :], fw_ref[...])
                return carry
            lax.fori_loop(0, tm // row_chunk, body, 0)


def _ffn(x, nw, wu, wd, layer, fw, *, tm, tf=512, final=False, emit_bf16=False):
    m, d = x.shape
    dff = wu.shape[2]
    assert m % tm == 0 and dff % tf == 0 and (not emit_bf16 or m == tm)
    kern = functools.partial(_ffn_kernel, tm=tm, norm_rows=min(tm, 256), row_chunk=min(tm, 64),
                             col_chunk=512, final=final,
                             emit_bf16=emit_bf16)
    out_shape = [jax.ShapeDtypeStruct((m, d), F32)]
    out_specs = [pl.BlockSpec((tm, d), lambda i, f: (i, 0))]
    if emit_bf16:
        out_shape += [jax.ShapeDtypeStruct((1, d, dff), BF16),
                      jax.ShapeDtypeStruct((1, dff, d), BF16)]
        out_specs += [pl.BlockSpec((None, d, tf), lambda i, f: (0, 0, f)),
                      pl.BlockSpec((None, tf, d), lambda i, f: (0, f, 0))]
    res = pl.pallas_call(
        kern,
        out_shape=out_shape,
        grid=(m // tm, dff // tf),
        in_specs=[pl.BlockSpec((tm, d), lambda i, f: (i, 0)),
                  pl.BlockSpec((1, d), lambda i, f: (0, 0)),
                  pl.BlockSpec((None, d, tf), lambda i, f: (layer, 0, f)),
                  pl.BlockSpec((None, tf, d), lambda i, f: (layer, f, 0)),
                  pl.BlockSpec((1, d), lambda i, f: (0, 0))],
        out_specs=out_specs,
        scratch_shapes=[pltpu.VMEM((tm, d), BF16)],
        compiler_params=_params(("arbitrary", "arbitrary"), 56),
        name="ffn",
    )(x, nw.reshape(1, d), wu, wd, fw.reshape(1, d))
    return res if emit_bf16 else res[0]


_XP_PAD = 8


def _scan_group(a, u, hb, row):
    for dist in (1, 2, 4):
        keep = row >= dist
        a_s = jnp.where(keep, pltpu.roll(a, dist, 0), 1.0)
        u_s = jnp.where(keep, pltpu.roll(u, dist, 0), 0.0)
        u = a * u_s + u
        a = a * a_s
    return a * hb + u


def _mixer_a_kernel(g_ref, xr_ref, cprev_ref, hprev_ref, cw_ref, cb_ref, wa_ref, ba_ref,
                    wx_ref, bx_ref, lam_ref, ya_ref, conv_ref, hlast_ref, xp_ref, h_ref,
                    *, tt, rc):
    t = pl.program_id(1)
    hist = CONV_W - 1

    @pl.when(t == 0)
    def _():
        xp_ref[_XP_PAD - hist:_XP_PAD, :] = cprev_ref[0]
        h_ref[...] = hprev_ref[0]

    @pl.when(t > 0)
    def _():
        xp_ref[_XP_PAD - hist:_XP_PAD, :] = xp_ref[_XP_PAD + tt - hist:_XP_PAD + tt, :]

    xp_ref[_XP_PAD:_XP_PAD + tt, :] = xr_ref[...]

    z = -lam_ref[...]
    softplus = jnp.maximum(z, 0.0) + jnp.log1p(jnp.exp(-jnp.abs(z)))
    row = lax.broadcasted_iota(jnp.int32, (8, D_RNN), 0)
    nblk = D_RNN // 256
    hb = jnp.broadcast_to(h_ref[...], (8, D_RNN))

    for c in range(tt // rc):
        r0 = c * rc
        xc = cb_ref[...]
        for j in range(CONV_W):
            s = _XP_PAD - hist + j + r0
            xc = xc + xp_ref[s:s + rc, :] * cw_ref[j:j + 1, :]
        xcb = xc.astype(BF16)
        r_lin = jnp.concatenate(
            [_dot(xcb[:, 256 * b:256 * (b + 1)], wa_ref[b]) for b in range(nblk)], axis=1)
        i_lin = jnp.concatenate(
            [_dot(xcb[:, 256 * b:256 * (b + 1)], wx_ref[b]) for b in range(nblk)], axis=1)
        r = _sigmoid(r_lin + ba_ref[...])
        i = _sigmoid(i_lin + bx_ref[...])
        log_a = (-LRU_C) * r * softplus
        a = jnp.exp(log_a)
        th = jnp.tanh(log_a)
        u = jnp.sqrt((-2.0 * th) / (1.0 - th)) * (i * xc)
        hs = []
        for k in range(rc // 8):
            hrows = _scan_group(a[8 * k:8 * k + 8], u[8 * k:8 * k + 8], hb, row)
            hb = jnp.broadcast_to(hrows[7:8, :], (8, D_RNN))
            hs.append(hrows)
        hs = jnp.concatenate(hs, axis=0)
        g = g_ref[r0:r0 + rc, :]
        ya_ref[r0:r0 + rc, :] = (_gelu_tanh(g) * hs).astype(ya_ref.dtype)

    h_ref[...] = hb[0:1, :]
    conv_ref[0] = xr_ref[tt - hist:tt, :]
    hlast_ref[0] = hb[0:1, :]


def _mixer_a(gx, conv_prev, h_prev, p, *, batch, seq, tt):
    assert seq % tt == 0 and tt % 8 == 0
    rc = min(tt, 64)
    nt = seq // tt
    kern = functools.partial(_mixer_a_kernel, tt=tt, rc=rc)
    row_blk = lambda b, t: (b * nt + t, 0)
    vec = lambda b, t: (0, 0)
    return pl.pallas_call(
        kern,
        out_shape=[jax.ShapeDtypeStruct((batch * seq, D_RNN), BF16),
                   jax.ShapeDtypeStruct((batch, CONV_W - 1, D_RNN), F32),
                   jax.ShapeDtypeStruct((batch, 1, D_RNN), F32)],
        grid=(batch, nt),
        in_specs=[pl.BlockSpec((tt, D_RNN), row_blk),
                  pl.BlockSpec((tt, D_RNN), lambda b, t: (b * nt + t, 1)),
                  pl.BlockSpec((1, CONV_W - 1, D_RNN), lambda b, t: (b, 0, 0)),
                  pl.BlockSpec((1, 1, D_RNN), lambda b, t: (b, 0, 0)),
                  pl.BlockSpec((CONV_W, D_RNN), vec),
                  pl.BlockSpec((1, D_RNN), vec),
                  pl.BlockSpec((D_RNN // 256, 256, 256), lambda b, t: (0, 0, 0)),
                  pl.BlockSpec((1, D_RNN), vec),
                  pl.BlockSpec((D_RNN // 256, 256, 256), lambda b, t: (0, 0, 0)),
                  pl.BlockSpec((1, D_RNN), vec),
                  pl.BlockSpec((1, D_RNN), vec)],
        out_specs=[pl.BlockSpec((tt, D_RNN), row_blk),
                   pl.BlockSpec((1, CONV_W - 1, D_RNN), lambda b, t: (b, 0, 0)),
                   pl.BlockSpec((1, 1, D_RNN), lambda b, t: (b, 0, 0))],
        scratch_shapes=[pltpu.VMEM((_XP_PAD + tt, D_RNN), F32),
                        pltpu.VMEM((1, D_RNN), F32)],
        compiler_params=_params(("arbitrary", "arbitrary"), 40),
        name="mixer_a",
    )(gx, gx, conv_prev, h_prev.reshape(batch, 1, D_RNN), p["conv_w"], p["conv_b"],
      p["wa_bd"], p["ba"], p["wx_bd"], p["bx"], p["lam"])


def _rel_bias(d, tab_ref, h):
    n = jnp.abs(d)
    large = jnp.full(d.shape, _MAX_EXACT, jnp.int32)
    for thr in _BUCKET_THRESHOLDS:
        large = large + (n >= thr).astype(jnp.int32)
    bucket = jnp.where(d > 0, _NB, 0) + jnp.where(n < _MAX_EXACT, n, large)
    val = jnp.full(d.shape, tab_ref[0, h], F32)
    for b in range(1, N_BUCKETS):
        val = jnp.where(bucket == b, tab_ref[b, h], val)
    return val


def _split_q(q):
    lane = lax.broadcasted_iota(jnp.int32, q.shape, 1)
    zero = jnp.zeros_like(q)
    return jnp.where(lane < B_DH, q, zero), jnp.where(lane >= B_DH, q, zero)


def _diff_lambda(lamv_ref, lam_init):
    lv = lamv_ref[...]
    s01 = jnp.sum(lv[0:1, :] * lv[1:2, :], axis=-1, keepdims=True)
    s23 = jnp.sum(lv[2:3, :] * lv[3:4, :], axis=-1, keepdims=True)
    return jnp.exp(s01) - jnp.exp(s23) + lam_init


def _diff_finish(o1, l1, o2, l2, lam, sub, lam_init):
    o = o1 / l1 - lam * (o2 / l2)
    return _rms_rows(o, sub) * (1.0 - lam_init)


def _attn_prompt_kernel(tab_ref, lamv_ref, subc_ref, q_ref, k_ref, vt_ref, o_ref,
                        b0_ref, b1_ref, qz_ref, s_ref, m_ref, l_ref, acc_ref, *, tq, tk, lam_init):
    h = pl.program_id(0)
    qi = pl.program_id(1)
    nq = tq // tk
    far_bias = tab_ref[_FAR_BUCKET, h]

    @pl.when(qi == 0)
    def _():
        ki_ = lax.broadcasted_iota(jnp.int32, (tk, tk), 0)
        qi_ = lax.broadcasted_iota(jnp.int32, (tk, tk), 1)
        d = ki_ - qi_
        visible = (ki_ // CHUNK) <= (qi_ // CHUNK)
        b0_ref[...] = jnp.where(visible, (_rel_bias(d, tab_ref, h) - far_bias) * LOG2E,
                                MASK_VALUE)
        b1_ref[...] = (_rel_bias(d - tk, tab_ref, h) - far_bias) * LOG2E

    qt = q_ref[...].astype(F32).T
    row = lax.broadcasted_iota(jnp.int32, qt.shape, 0)
    qz_ref[:, 0:tq] = jnp.where(row < B_DH, qt, 0.0).astype(BF16)
    qz_ref[:, tq:2 * tq] = jnp.where(row >= B_DH, qt, 0.0).astype(BF16)
    m_ref[...] = jnp.full(m_ref.shape, MASK_VALUE, F32)
    l_ref[...] = jnp.zeros(l_ref.shape, F32)
    acc_ref[...] = jnp.zeros(acc_ref.shape, F32)

    def scores(ki, br, c, slot):
        r = pl.multiple_of(ki * tk, tk)
        c0 = br * tq + c * tk
        s_ref[slot] = _dot(k_ref[pl.ds(r, tk), :], qz_ref[:, c0:c0 + tk])

    def softmax_pv(ki, br, c, slot, bias_ref):
        c0 = br * tq + c * tk
        cols = slice(c0, c0 + tk)
        r = pl.multiple_of(ki * tk, tk)
        s = s_ref[slot]
        if bias_ref is not None:
            s = s + bias_ref[...]
        m_old = m_ref[:, cols]
        m_new = jnp.maximum(m_old, jnp.max(s, axis=0, keepdims=True))
        alpha = jnp.exp2(m_old - m_new)
        e = jnp.exp2(s - m_new)
        l_ref[:, cols] = alpha * l_ref[:, cols] + jnp.sum(e, axis=0, keepdims=True)
        acc_ref[:, cols] = (alpha * acc_ref[:, cols]
                            + _dot(vt_ref[:, pl.ds(r, tk)], e.astype(BF16)))
        m_ref[:, cols] = m_new

    def run_tile(ki, items, after):
        assert len(items) % 2 == 0
        for idx, (br, c, bias_ref) in enumerate(items):
            slot = idx % 2
            if idx + 1 < len(items):
                scores(ki, items[idx + 1][0], items[idx + 1][1], 1 - slot)
            elif after is not None:
                scores(ki + 1, 0, after, 1 - slot)
            softmax_pv(ki, br, c, slot, bias_ref)

    def block_bias(c, j):
        return b0_ref if c == j else b1_ref if c == j + 1 else None

    scores(0, 0, 0, 0)

    def far_body(ki, carry):
        run_tile(ki, [(br, c, None) for br in range(2) for c in range(nq)], 0)
        return carry
    lax.fori_loop(0, jnp.maximum(nq * qi - 1, 0), far_body, 0)

    @pl.when(qi >= 1)
    def _():
        run_tile(nq * qi - 1, [(br, c, block_bias(c, -1)) for br in range(2)
                               for c in range(nq)], 0)

    for j in range(nq):
        run_tile(nq * qi + j, [(br, c, block_bias(c, j)) for br in range(2)
                               for c in range(j, nq)], j + 1 if j + 1 < nq else None)

    lam = _diff_lambda(lamv_ref, lam_init)
    o = (acc_ref[:, 0:tq] / l_ref[:, 0:tq]
         - lam * (acc_ref[:, tq:2 * tq] / l_ref[:, tq:2 * tq]))
    ms = jnp.mean(o * o, axis=0, keepdims=True)
    o = o * lax.rsqrt(ms + EPS) * subc_ref[...] * (1.0 - lam_init)
    o_ref[...] = o.T.astype(o_ref.dtype)


def _attn_prompt(q, k, vt, p, lam_init, *, tq=2048, tk=512):
    s = q.shape[0]
    assert s % tq == 0 and tq % tk == 0 and tk % CHUNK == 0 and tk + 1 >= _FAR_DIST
    kern = functools.partial(_attn_prompt_kernel, tq=tq, tk=tk, lam_init=lam_init)
    return pl.pallas_call(
        kern,
        out_shape=jax.ShapeDtypeStruct((s, B_W), BF16),
        grid=(B_HEADS, s // tq),
        in_specs=[pl.BlockSpec(memory_space=pltpu.SMEM),
                  pl.BlockSpec((4, B_DH), lambda h, i: (0, 0)),
                  pl.BlockSpec((B_E, 1), lambda h, i: (0, 0)),
                  pl.BlockSpec((tq, B_E), lambda h, i: (i, h)),
                  pl.BlockSpec((s, B_E), lambda h, i: (0, h)),
                  pl.BlockSpec((B_E, s), lambda h, i: (h, 0))],
        out_specs=pl.BlockSpec((tq, B_E), lambda h, i: (i, h)),
        scratch_shapes=[pltpu.VMEM((tk, tk), F32), pltpu.VMEM((tk, tk), F32),
                        pltpu.VMEM((B_E, 2 * tq), BF16),
                        pltpu.VMEM((2, tk, tk), F32),
                        pltpu.VMEM((1, 2 * tq), F32), pltpu.VMEM((1, 2 * tq), F32),
                        pltpu.VMEM((B_E, 2 * tq), F32)],
        compiler_params=_params(("arbitrary", "arbitrary"), 48),
        name="attn_prompt",
    )(p["bias_table"], p["lam_vecs"], p["subln_w"].reshape(B_E, 1), q, k, vt)


_SAMPLE_COL_CHUNK = 2048


def _attn_sample_kernel(tab_ref, lamv_ref, sub_ref, q_ref, kn_ref, vn_ref, kc_ref, vc_ref, o_ref,
                        bc_ref, bn_ref, s_ref, *, t_new, past, lam_init):
    rows_h = 2 * t_new
    n_keys = past * B_HEADS

    @pl.when(pl.program_id(0) == 0)
    def _():
        t_c = lax.broadcasted_iota(jnp.int32, (rows_h, n_keys), 0) % t_new
        col = lax.broadcasted_iota(jnp.int32, (rows_h, n_keys), 1)
        d_c = col // B_HEADS - past - t_c
        head_c = col % B_HEADS
        t_n = lax.broadcasted_iota(jnp.int32, (rows_h, B_HEADS * t_new), 0) % t_new
        col_n = lax.broadcasted_iota(jnp.int32, (rows_h, B_HEADS * t_new), 1)
        d_n = col_n % t_new - t_n
        head_n = col_n // t_new
        for h in range(B_HEADS):
            rows = slice(h * rows_h, (h + 1) * rows_h)
            bc_ref[rows, :] = jnp.where(head_c == h, LOG2E * _rel_bias(d_c, tab_ref, h),
                                        MASK_VALUE)
            bn_ref[rows, :] = jnp.where(head_n == h, LOG2E * _rel_bias(d_n, tab_ref, h),
                                        MASK_VALUE)

    q_parts, kn_parts, vn_parts = [], [], []
    for h in range(B_HEADS):
        cols = slice(h * B_E, (h + 1) * B_E)
        q_parts.extend(_split_q(q_ref[:, cols]))
        kn_parts.append(kn_ref[:, cols])
        vn_parts.append(vn_ref[:, cols])
    qblk = jnp.concatenate(q_parts, axis=0)
    kn = jnp.concatenate(kn_parts, axis=0).astype(BF16)
    vn = jnp.concatenate(vn_parts, axis=0).astype(BF16)

    s_n = _dot_nt(qblk, kn) + bn_ref[...]
    m = jnp.max(s_n, axis=-1, keepdims=True)
    for c0 in range(0, n_keys, _SAMPLE_COL_CHUNK):
        cc = slice(c0, c0 + _SAMPLE_COL_CHUNK)
        s = _dot_nt(qblk, kc_ref[cc, :].astype(BF16)) + bc_ref[:, cc]
        s_ref[:, cc] = s
        m = jnp.maximum(m, jnp.max(s, axis=-1, keepdims=True))

    e_n = jnp.exp2(s_n - m)
    l = jnp.sum(e_n, axis=-1, keepdims=True)
    o = _dot(e_n.astype(BF16), vn)
    for c0 in range(0, n_keys, _SAMPLE_COL_CHUNK):
        cc = slice(c0, c0 + _SAMPLE_COL_CHUNK)
        e = jnp.exp2(s_ref[:, cc] - m)
        l = l + jnp.sum(e, axis=-1, keepdims=True)
        o = o + _dot(e.astype(BF16), vc_ref[cc, :].astype(BF16))

    lam = _diff_lambda(lamv_ref, lam_init)
    for h in range(B_HEADS):
        r1 = slice(h * rows_h, h * rows_h + t_new)
        r2 = slice(h * rows_h + t_new, (h + 1) * rows_h)
        o_ref[:, h * B_E:(h + 1) * B_E] = _diff_finish(
            o[r1], l[r1], o[r2], l[r2], lam, sub_ref[...], lam_init).astype(o_ref.dtype)


def _attn_sample(q, k_new, v_new, k_cache, v_cache, layer, p, lam_init, *, batch, t_new):
    n_layers, _, past, _, _ = k_cache.shape
    n_keys = past * B_HEADS
    assert n_keys % _SAMPLE_COL_CHUNK == 0
    kern = functools.partial(_attn_sample_kernel, t_new=t_new, past=past, lam_init=lam_init)
    new_blk = pl.BlockSpec((t_new, B_W), lambda b: (b, 0))
    cache_blk = pl.BlockSpec((None, None, n_keys, B_E), lambda b: (layer, b, 0, 0))
    n_rows = B_HEADS * 2 * t_new
    return pl.pallas_call(
        kern,
        out_shape=jax.ShapeDtypeStruct((batch * t_new, B_W), BF16),
        grid=(batch,),
        in_specs=[pl.BlockSpec(memory_space=pltpu.SMEM),
                  pl.BlockSpec((4, B_DH), lambda b: (0, 0)),
                  pl.BlockSpec((1, B_E), lambda b: (0, 0)),
                  new_blk, new_blk, new_blk, cache_blk, cache_blk],
        out_specs=new_blk,
        scratch_shapes=[pltpu.VMEM((n_rows, n_keys), F32),
                        pltpu.VMEM((n_rows, B_HEADS * t_new), F32),
                        pltpu.VMEM((n_rows, n_keys), F32)],
        compiler_params=_params(("arbitrary",), 48),
        name="attn_sample",
    )(p["bias_table"], p["lam_vecs"], p["subln_w"], q, k_new, v_new,
      k_cache.reshape(n_layers, batch, n_keys, B_E), v_cache.reshape(n_layers, batch, n_keys, B_E))


def _gating_kernel(u_ref, v_ref, vnw_ref, ws_ref, bs_ref, *refs, chunk, n_chunks, emit_v):
    if emit_v:
        o_ref, vout_ref, wsm_ref = refs
    else:
        o_ref, wsm_ref = refs
        vout_ref = None

    @pl.when(pl.program_id(0) == 0)
    def _():
        ri = lax.broadcasted_iota(jnp.int32, (chunk, chunk), 0)
        ci = lax.broadcasted_iota(jnp.int32, (chunk, chunk), 1)
        for g in range(C_GROUPS):
            wsm_ref[g] = jnp.where(ci <= ri, ws_ref[g], 0.0).astype(BF16)

    for c in range(n_chunks):
        r0 = c * chunk
        vn = _rms_rows(v_ref[r0:r0 + chunk, :], vnw_ref[...])
        if emit_v:
            vout_ref[r0:r0 + chunk, :] = vn
        vnb = vn.astype(BF16)
        for g in range(C_GROUPS):
            cols = slice(g * C_GW, (g + 1) * C_GW)
            mix = _dot(wsm_ref[g], vnb[:, cols]) + bs_ref[:, g:g + 1]
            o_ref[r0:r0 + chunk, cols] = (u_ref[r0:r0 + chunk, cols] * mix).astype(o_ref.dtype)


def _gating(z, vnw, ws, bs_t, *, chunk, n_chunks, emit_v):
    m = z.shape[0]
    rt = chunk * n_chunks
    assert m % rt == 0
    kern = functools.partial(_gating_kernel, chunk=chunk, n_chunks=n_chunks, emit_v=emit_v)
    out_shape = [jax.ShapeDtypeStruct((m, D_C), BF16)]
    out_specs = [pl.BlockSpec((rt, D_C), lambda i: (i, 0))]
    if emit_v:
        out_shape.append(jax.ShapeDtypeStruct((m, D_C), F32))
        out_specs.append(pl.BlockSpec((rt, D_C), lambda i: (i, 0)))
    res = pl.pallas_call(
        kern,
        out_shape=out_shape,
        grid=(m // rt,),
        in_specs=[pl.BlockSpec((rt, D_C), lambda i: (i, 0)),
                  pl.BlockSpec((rt, D_C), lambda i: (i, 1)),
                  pl.BlockSpec((1, D_C), lambda i: (0, 0)),
                  pl.BlockSpec((C_GROUPS, chunk, chunk), lambda i: (0, 0, 0)),
                  pl.BlockSpec((chunk, C_GROUPS), lambda i: (0, 0))],
        out_specs=out_specs,
        scratch_shapes=[pltpu.VMEM((C_GROUPS, chunk, chunk), BF16)],
        compiler_params=_params(("arbitrary",), 40),
        name="gating",
    )(z, z, vnw.reshape(1, D_C), ws, bs_t)
    return res if emit_v else (res[0], None)


def _block_diag(w):
    per = 256 // LRU_BW
    w4 = w.reshape(LRU_BLOCKS // per, per, LRU_BW, LRU_BW)
    eye = jnp.eye(per, dtype=w.dtype)
    bd = jnp.einsum("gaij,ab->gaibj", w4, eye)
    return bd.reshape(LRU_BLOCKS // per, 256, 256).astype(BF16)


def kernel(x_prompt, x_sample, state_conv, state_lru, cache_k, cache_v, w_even_in, w_even_out,
           conv_w, conv_b, lru_wa, lru_ba, lru_wx, lru_bx, lru_lambda, lam_vecs, subln_w,
           rel_bias_table, w_odd_in, w_odd_out, gmlp_vnorm_w, gmlp_ws, gmlp_bs, norm_mix_w,
           norm_ffn_w, norm_final_w, w_ff_up, w_ff_down):
    batch_p, seq_p, _ = x_prompt.shape
    batch_s, seq_s, _ = x_sample.shape
    assert batch_p == 1
    yp = x_prompt.reshape(batch_p * seq_p, D_MODEL)
    ys = x_sample.reshape(batch_s * seq_s, D_MODEL)
    tm_p = 1024
    tm_s = batch_s * seq_s
    scale = B_DH ** -0.5 * LOG2E

    q0, k0, v0 = 2 * D_RNN, 2 * D_RNN + B_W, 2 * D_RNN + 2 * B_W
    even_segs = [(0, 2 * D_RNN, F32, 1.0, False), (q0, B_W, BF16, scale, False),
                 (k0, B_W, F32, 1.0, False), (v0, B_W, F32, 1.0, False)]
    even_segs_p = even_segs + [(k0, B_W, BF16, 1.0, False), (v0, B_W, BF16, 1.0, True)]
    odd_segs = [(0, 2 * D_C, F32, 1.0, False)]
    zeros_conv = jnp.zeros((batch_p, CONV_W - 1, D_RNN), F32)
    zeros_h = jnp.zeros((batch_p, D_RNN), F32)

    p_conv, p_lru, p_k, p_v = [], [], [], []
    s_conv, s_lru, s_k, s_v, s_gv = [], [], [], [], []
    for l in range(DEPTH):
        if l % 2 == 0:
            e = l // 2
            lam_init = 0.8 - 0.6 * math.exp(-0.3 * l)
            pa = dict(conv_w=conv_w[e], conv_b=conv_b[e].reshape(1, D_RNN),
                      wa_bd=_block_diag(lru_wa[e]), ba=lru_ba[e].reshape(1, D_RNN),
                      wx_bd=_block_diag(lru_wx[e]), bx=lru_bx[e].reshape(1, D_RNN),
                      lam=lru_lambda[e].reshape(1, D_RNN))
            pb = dict(bias_table=rel_bias_table, lam_vecs=lam_vecs[e],
                      subln_w=subln_w[e].reshape(1, B_E))
            gx, q, k, v, w_in_b = _norm_matmul(ys, norm_mix_w[l], w_even_in, e, even_segs,
                                               tm=tm_s, emit_bf16=True)
            ya, c_new, h_new = _mixer_a(gx, state_conv[e], state_lru[e], pa, batch=batch_s,
                                        seq=seq_s, tt=seq_s)
            yb = _attn_sample(q, k, v, cache_k, cache_v, e, pb, lam_init, batch=batch_s,
                              t_new=seq_s)
            ys, w_out_b = _matmul_res([ya, yb], w_even_out, e, ys, tm=tm_s, emit_bf16=True)
            s_conv.append(c_new)
            s_lru.append(h_new.reshape(batch_s, D_RNN))
            s_k.append(k.reshape(batch_s, seq_s, B_HEADS, B_E))
            s_v.append(v.reshape(batch_s, seq_s, B_HEADS, B_E))
            gx, q, k, v, k_b, vt_b = _norm_matmul(yp, norm_mix_w[l], w_in_b, 0, even_segs_p,
                                                  tm=tm_p)
            ya, c_new, h_new = _mixer_a(gx, zeros_conv, zeros_h, pa, batch=batch_p, seq=seq_p,
                                        tt=512)
            yb = _attn_prompt(q, k_b, vt_b, pb, lam_init)
            yp = _matmul_res([ya, yb], w_out_b, 0, yp, tm=tm_p, resident_w=True)
            p_conv.append(c_new)
            p_lru.append(h_new.reshape(batch_p, D_RNN))
            p_k.append(k.reshape(batch_p, seq_p, B_HEADS, B_E))
            p_v.append(v.reshape(batch_p, seq_p, B_HEADS, B_E))
        else:
            o = l // 2
            z, w_in_b = _norm_matmul(ys, norm_mix_w[l], w_odd_in, o, odd_segs, act="gelu",
                                     tm=tm_s, emit_bf16=True)
            gated, gv = _gating(z, gmlp_vnorm_w[o], gmlp_ws[o][:, :seq_s, :seq_s],
                                gmlp_bs[o][:, :seq_s].T, chunk=seq_s, n_chunks=1, emit_v=True)
            ys, w_out_b = _matmul_res([gated], w_odd_out, o, ys, tm=tm_s, emit_bf16=True)
            s_gv.append(gv.reshape(batch_s, seq_s, D_C))
            (z,) = _norm_matmul(yp, norm_mix_w[l], w_in_b, 0, odd_segs, act="gelu", tm=tm_p,
                                tn=1024)
            gated, _ = _gating(z, gmlp_vnorm_w[o], gmlp_ws[o], gmlp_bs[o].T, chunk=GMLP_CHUNK,
                               n_chunks=4, emit_v=False)
            yp = _matmul_res([gated], w_out_b, 0, yp, tm=tm_p, resident_w=True)
        last = l == DEPTH - 1
        ys, wu_b, wd_b = _ffn(ys, norm_ffn_w[l], w_ff_up, w_ff_down, l, norm_final_w, tm=tm_s,
                              final=last, emit_bf16=True)
        yp = _ffn(yp, norm_ffn_w[l], wu_b, wd_b, 0, norm_final_w, tm=tm_p, final=last)

    return (yp.reshape(batch_p, seq_p, D_MODEL), ys.reshape(batch_s, seq_s, D_MODEL),
            jnp.stack(p_conv), jnp.stack(p_lru), jnp.stack(p_k), jnp.stack(p_v),
            jnp.stack(s_conv), jnp.stack(s_lru), jnp.stack(s_k), jnp.stack(s_v),
            jnp.stack(s_gv))
```

```python
import functools
import math

import jax
import jax.numpy as jnp
from jax import lax
from jax.experimental import pallas as pl
from jax.experimental.pallas import tpu as pltpu

D_MODEL = 2048
DEPTH = 4
CHUNK = 64
EPS = 1e-6
D_RNN = D_MODEL // 2
LRU_BLOCKS = 16
LRU_BW = D_RNN // LRU_BLOCKS
CONV_W = 4
LRU_C = 8.0
B_HEADS = 8
B_DH = 64
B_E = 2 * B_DH
B_W = B_HEADS * B_E
N_BUCKETS = 32
MAX_DIST = 512
GMLP_CHUNK = 128
D_C = D_MODEL
C_GROUPS = 16
C_GW = D_C // C_GROUPS
D_FF = 4 * D_MODEL

F32 = jnp.float32
BF16 = jnp.bfloat16
MASK_VALUE = -1e30
LOG2E = math.log2(math.e)
MIB = 1024 * 1024
MXU_DIM = 256
NORM_ROWS = 256
_VMEM_MIB = dict(norm_matmul=56, matmul_res=48, ffn=56, mixer_a=40, attn_prompt=48,
                 attn_sample=48, gating=40)

_NB = N_BUCKETS // 2
_MAX_EXACT = _NB // 2
_BUCKET_THRESHOLDS = tuple(
    int(math.ceil(_MAX_EXACT * (MAX_DIST / _MAX_EXACT) ** (k / (_NB - _MAX_EXACT)) - 1e-9))
    for k in range(1, _NB - _MAX_EXACT))
_FAR_BUCKET = _NB - 1
_FAR_DIST = _BUCKET_THRESHOLDS[-1]


def _dot(a, b):
    return jnp.dot(a, b, preferred_element_type=F32)


def _dot_nt(a, b):
    return lax.dot_general(a, b, (((1,), (1,)), ((), ())), preferred_element_type=F32)


def _rms_rows(x, w):
    ms = jnp.mean(x * x, axis=-1, keepdims=True)
    return x * lax.rsqrt(ms + EPS) * w


def _gelu_tanh(x):
    c = math.sqrt(2.0 / math.pi)
    return x * (0.5 * (1.0 + jnp.tanh(c * (x + 0.044715 * (x * x * x)))))


def _sigmoid(x):
    return 0.5 * jnp.tanh(0.5 * x) + 0.5


def _params(name, grid_rank):
    return pltpu.CompilerParams(dimension_semantics=("arbitrary",) * grid_rank,
                                vmem_limit_bytes=_VMEM_MIB[name] * MIB)


def _norm_matmul_kernel(x_ref, nw_ref, w_ref, *refs, segs, act, tm, norm_rows, emit_bf16):
    out_refs = refs[:len(segs)]
    n = pl.program_id(1)
    if emit_bf16:
        wb_ref, xn_ref = refs[len(segs):]
        wb_ref[...] = w_ref[...].astype(BF16)
        w_ref = wb_ref
    else:
        xn_ref = refs[len(segs)]

    def project(xn):
        acc = _dot(xn, w_ref[...])
        return _gelu_tanh(acc) if act == "gelu" else acc

    lo0, hi0, scale0, transposed0 = segs[0]
    assert lo0 == 0 and hi0 > 0 and not transposed0

    @pl.when(n == 0)
    def _():
        for r in range(0, tm, norm_rows):
            rows = slice(r, r + norm_rows)
            xn = _rms_rows(x_ref[rows, :], nw_ref[...]).astype(BF16)
            xn_ref[rows, :] = xn
            acc = project(xn)
            val = acc if scale0 == 1.0 else acc * scale0
            out_refs[0][rows, :] = val.astype(out_refs[0].dtype)

    @pl.when(n > 0)
    def _():
        acc = project(xn_ref[...])
        for (lo, hi, scale, transposed), o_ref in zip(segs, out_refs):
            def store(o_ref=o_ref, scale=scale, transposed=transposed):
                val = acc if scale == 1.0 else acc * scale
                if transposed:
                    val = val.T
                o_ref[...] = val.astype(o_ref.dtype)
            if len(segs) == 1:
                store()
            else:
                pl.when((n >= lo) & (n < hi))(store)


def _norm_matmul(x, nw, w, layer, seg_defs, *, act=None, tm, tn=512, emit_bf16=False):
    m, d = x.shape
    n_total = w.shape[2]
    assert m % tm == 0 and n_total % tn == 0 and (not emit_bf16 or m == tm)
    segs, out_shapes, out_specs = [], [], []
    for first, cols, dtype, scale, transposed in seg_defs:
        assert cols % tn == 0 and first % tn == 0
        lo, nb = first // tn, cols // tn
        segs.append((lo, lo + nb, scale, transposed))
        if transposed:
            out_shapes.append(jax.ShapeDtypeStruct((cols, m), dtype))
            out_specs.append(pl.BlockSpec(
                (tn, tm), lambda i, j, lo=lo, nb=nb: (jnp.clip(j - lo, 0, nb - 1), i)))
        else:
            out_shapes.append(jax.ShapeDtypeStruct((m, cols), dtype))
            out_specs.append(pl.BlockSpec(
                (tm, tn), lambda i, j, lo=lo, nb=nb: (i, jnp.clip(j - lo, 0, nb - 1))))
    if emit_bf16:
        out_shapes.append(jax.ShapeDtypeStruct((1, d, n_total), BF16))
        out_specs.append(pl.BlockSpec((None, d, tn), lambda i, j: (0, 0, j)))
    kern = functools.partial(_norm_matmul_kernel, segs=tuple(segs), act=act, tm=tm,
                             norm_rows=min(tm, NORM_ROWS), emit_bf16=emit_bf16)
    return pl.pallas_call(
        kern,
        out_shape=out_shapes,
        grid=(m // tm, n_total // tn),
        in_specs=[pl.BlockSpec((tm, d), lambda i, j: (i, 0)),
                  pl.BlockSpec((1, d), lambda i, j: (0, 0)),
                  pl.BlockSpec((None, d, tn), lambda i, j: (layer, 0, j))],
        out_specs=out_specs,
        scratch_shapes=[pltpu.VMEM((tm, d), BF16)],
        compiler_params=_params("norm_matmul", 2),
        name="norm_matmul",
    )(x, nw.reshape(1, d), w)


def _matmul_res_kernel(*refs, n_in, tn, emit_bf16, resident_w):
    a_refs = refs[:n_in]
    w_refs = refs[n_in:2 * n_in]
    res_ref = refs[2 * n_in]
    o_ref = refs[2 * n_in + 1]
    if resident_w:
        c0 = pl.multiple_of(pl.program_id(1) * tn, tn)
        w_refs = [w_ref.at[:, pl.ds(c0, tn)] for w_ref in w_refs]
    acc = res_ref[...]
    if emit_bf16:
        wb_ref = refs[2 * n_in + 2]
        for b, (a_ref, w_ref) in enumerate(zip(a_refs, w_refs)):
            wb_ref[b] = w_ref[...].astype(BF16)
            acc = acc + _dot(a_ref[...], wb_ref[b])
    else:
        for a_ref, w_ref in zip(a_refs, w_refs):
            acc = acc + _dot(a_ref[...], w_ref[...])
    o_ref[...] = acc


def _matmul_res(a_list, w, layer, res, *, tm, tn=1024, emit_bf16=False, resident_w=False):
    m, n_total = res.shape
    n_in = len(a_list)
    k_total = w.shape[1]
    kb = k_total // n_in
    assert all(a.shape == (m, kb) for a in a_list) and m % tm == 0 and n_total % tn == 0
    assert (not emit_bf16 or m == tm) and not (emit_bf16 and resident_w)
    in_specs = [pl.BlockSpec((tm, kb), lambda i, j: (i, 0)) for _ in a_list]
    if resident_w:
        in_specs += [pl.BlockSpec((None, kb, n_total), lambda i, j, b=b: (layer, b, 0),
                                  pipeline_mode=pl.Buffered(1)) for b in range(n_in)]
    else:
        in_specs += [pl.BlockSpec((None, kb, tn), lambda i, j, b=b: (layer, b, j))
                     for b in range(n_in)]
    in_specs += [pl.BlockSpec((tm, tn), lambda i, j: (i, j))]
    out_shape = [jax.ShapeDtypeStruct((m, n_total), F32)]
    out_specs = [pl.BlockSpec((tm, tn), lambda i, j: (i, j))]
    if emit_bf16:
        out_shape += [jax.ShapeDtypeStruct((n_in, kb, n_total), BF16)]
        out_specs += [pl.BlockSpec((n_in, kb, tn), lambda i, j: (0, 0, j))]
    res_out = pl.pallas_call(
        functools.partial(_matmul_res_kernel, n_in=n_in, tn=tn, emit_bf16=emit_bf16,
                          resident_w=resident_w),
        out_shape=out_shape,
        grid=(m // tm, n_total // tn),
        in_specs=in_specs,
        out_specs=out_specs,
        compiler_params=_params("matmul_res", 2),
        name="matmul_res",
    )(*a_list, *([w] * n_in), res)
    if emit_bf16:
        return res_out[0], res_out[1].reshape(1, k_total, n_total)
    return res_out[0]


def _ffn_kernel(x_ref, nw_ref, wu_ref, wd_ref, fw_ref, o_ref, *refs, tm, norm_rows, col_chunk,
                final, emit_bf16):
    if emit_bf16:
        wub_ref, wdb_ref, xn_ref = refs
        wub_ref[...] = wu_ref[...].astype(BF16)
        wdb_ref[...] = wd_ref[...].astype(BF16)
        wu_ref, wd_ref = wub_ref, wdb_ref
    else:
        (xn_ref,) = refs
    f = pl.program_id(1)
    d = o_ref.shape[1]

    def mlp(xn):
        h = jnp.maximum(_dot(xn, wu_ref[...]), 0.0)
        return (h * h).astype(BF16)

    @pl.when(f == 0)
    def _():
        for r in range(0, tm, norm_rows):
            rows = slice(r, r + norm_rows)
            x = x_ref[rows, :]
            xn = _rms_rows(x, nw_ref[...]).astype(BF16)
            xn_ref[rows, :] = xn
            h = mlp(xn)
            for c0 in range(0, d, col_chunk):
                cols = slice(c0, c0 + col_chunk)
                o_ref[rows, cols] = x[:, cols] + _dot(h, wd_ref[:, cols])

    last = pl.num_programs(1) - 1

    @pl.when((f > 0) & (f < last) if final else f > 0)
    def _():
        h = mlp(xn_ref[...])
        for c0 in range(0, d, col_chunk):
            o_ref[:, c0:c0 + col_chunk] += _dot(h, wd_ref[:, c0:c0 + col_chunk])

    if final:
        @pl.when(f == last)
        def _():
            for r in range(0, tm, norm_rows):
                rows = slice(r, r + norm_rows)
                h = mlp(xn_ref[rows, :])
                y = jnp.concatenate(
                    [o_ref[rows, c0:c0 + col_chunk] + _dot(h, wd_ref[:, c0:c0 + col_chunk])
                     for c0 in range(0, d, col_chunk)], axis=1)
                o_ref[rows, :] = _rms_rows(y, fw_ref[...])


def _ffn(x, nw, wu, wd, layer, fw, *, tm, tf=512, final=False, emit_bf16=False):
    m, d = x.shape
    dff = wu.shape[2]
    assert m % tm == 0 and dff % tf == 0 and (not emit_bf16 or m == tm)
    assert dff // tf >= 2
    kern = functools.partial(_ffn_kernel, tm=tm, norm_rows=min(tm, NORM_ROWS), col_chunk=512,
                             final=final, emit_bf16=emit_bf16)
    out_shape = [jax.ShapeDtypeStruct((m, d), F32)]
    out_specs = [pl.BlockSpec((tm, d), lambda i, f: (i, 0))]
    if emit_bf16:
        out_shape += [jax.ShapeDtypeStruct((1, d, dff), BF16),
                      jax.ShapeDtypeStruct((1, dff, d), BF16)]
        out_specs += [pl.BlockSpec((None, d, tf), lambda i, f: (0, 0, f)),
                      pl.BlockSpec((None, tf, d), lambda i, f: (0, f, 0))]
    res = pl.pallas_call(
        kern,
        out_shape=out_shape,
        grid=(m // tm, dff // tf),
        in_specs=[pl.BlockSpec((tm, d), lambda i, f: (i, 0)),
                  pl.BlockSpec((1, d), lambda i, f: (0, 0)),
                  pl.BlockSpec((None, d, tf), lambda i, f: (layer, 0, f)),
                  pl.BlockSpec((None, tf, d), lambda i, f: (layer, f, 0)),
                  pl.BlockSpec((1, d), lambda i, f: (0, 0))],
        out_specs=out_specs,
        scratch_shapes=[pltpu.VMEM((tm, d), BF16)],
        compiler_params=_params("ffn", 2),
        name="ffn",
    )(x, nw.reshape(1, d), wu, wd, fw.reshape(1, d))
    return res if emit_bf16 else res[0]


_XP_PAD = 8


def _scan_group(a, u, hb, row):
    for dist in (1, 2, 4):
        keep = row >= dist
        a_s = jnp.where(keep, pltpu.roll(a, dist, 0), 1.0)
        u_s = jnp.where(keep, pltpu.roll(u, dist, 0), 0.0)
        u = a * u_s + u
        a = a * a_s
    return a * hb + u


def _mixer_a_kernel(g_ref, xr_ref, cprev_ref, hprev_ref, cw_ref, cb_ref, wa_ref, ba_ref,
                    wx_ref, bx_ref, lam_ref, ya_ref, conv_ref, hlast_ref, xp_ref, h_ref,
                    *, tt, rc):
    t = pl.program_id(1)
    hist = CONV_W - 1

    @pl.when(t == 0)
    def _():
        xp_ref[_XP_PAD - hist:_XP_PAD, :] = cprev_ref[0]
        h_ref[...] = hprev_ref[0]

    @pl.when(t > 0)
    def _():
        xp_ref[_XP_PAD - hist:_XP_PAD, :] = xp_ref[_XP_PAD + tt - hist:_XP_PAD + tt, :]

    xp_ref[_XP_PAD:_XP_PAD + tt, :] = xr_ref[...]

    z = -lam_ref[...]
    softplus = jnp.maximum(z, 0.0) + jnp.log1p(jnp.exp(-jnp.abs(z)))
    row = lax.broadcasted_iota(jnp.int32, (8, D_RNN), 0)
    nblk = D_RNN // MXU_DIM
    hb = jnp.broadcast_to(h_ref[...], (8, D_RNN))

    for c in range(tt // rc):
        r0 = c * rc
        xc = cb_ref[...]
        for j in range(CONV_W):
            s = _XP_PAD - hist + j + r0
            xc = xc + xp_ref[s:s + rc, :] * cw_ref[j:j + 1, :]
        xcb = xc.astype(BF16)
        r_lin = jnp.concatenate(
            [_dot(xcb[:, MXU_DIM * b:MXU_DIM * (b + 1)], wa_ref[b]) for b in range(nblk)],
            axis=1)
        i_lin = jnp.concatenate(
            [_dot(xcb[:, MXU_DIM * b:MXU_DIM * (b + 1)], wx_ref[b]) for b in range(nblk)],
            axis=1)
        r = _sigmoid(r_lin + ba_ref[...])
        i = _sigmoid(i_lin + bx_ref[...])
        log_a = (-LRU_C) * r * softplus
        a = jnp.exp(log_a)
        th = jnp.tanh(log_a)
        u = jnp.sqrt((-2.0 * th) / (1.0 - th)) * (i * xc)
        hs = []
        for k in range(rc // 8):
            hrows = _scan_group(a[8 * k:8 * k + 8], u[8 * k:8 * k + 8], hb, row)
            hb = jnp.broadcast_to(hrows[7:8, :], (8, D_RNN))
            hs.append(hrows)
        hs = jnp.concatenate(hs, axis=0)
        g = g_ref[r0:r0 + rc, :]
        ya_ref[r0:r0 + rc, :] = (_gelu_tanh(g) * hs).astype(ya_ref.dtype)

    h_ref[...] = hb[0:1, :]
    conv_ref[0] = xr_ref[tt - hist:tt, :]
    hlast_ref[0] = hb[0:1, :]


def _mixer_a(gx, conv_prev, h_prev, p, *, batch, seq, tt):
    assert seq % tt == 0 and tt % 8 == 0
    rc = min(tt, 64)
    nt = seq // tt
    kern = functools.partial(_mixer_a_kernel, tt=tt, rc=rc)
    row_blk = lambda b, t: (b * nt + t, 0)
    vec = lambda b, t: (0, 0)
    return pl.pallas_call(
        kern,
        out_shape=[jax.ShapeDtypeStruct((batch * seq, D_RNN), BF16),
                   jax.ShapeDtypeStruct((batch, CONV_W - 1, D_RNN), F32),
                   jax.ShapeDtypeStruct((batch, 1, D_RNN), F32)],
        grid=(batch, nt),
        in_specs=[pl.BlockSpec((tt, D_RNN), row_blk),
                  pl.BlockSpec((tt, D_RNN), lambda b, t: (b * nt + t, 1)),
                  pl.BlockSpec((1, CONV_W - 1, D_RNN), lambda b, t: (b, 0, 0)),
                  pl.BlockSpec((1, 1, D_RNN), lambda b, t: (b, 0, 0)),
                  pl.BlockSpec((CONV_W, D_RNN), vec),
                  pl.BlockSpec((1, D_RNN), vec),
                  pl.BlockSpec((D_RNN // MXU_DIM, MXU_DIM, MXU_DIM), lambda b, t: (0, 0, 0)),
                  pl.BlockSpec((1, D_RNN), vec),
                  pl.BlockSpec((D_RNN // MXU_DIM, MXU_DIM, MXU_DIM), lambda b, t: (0, 0, 0)),
                  pl.BlockSpec((1, D_RNN), vec),
                  pl.BlockSpec((1, D_RNN), vec)],
        out_specs=[pl.BlockSpec((tt, D_RNN), row_blk),
                   pl.BlockSpec((1, CONV_W - 1, D_RNN), lambda b, t: (b, 0, 0)),
                   pl.BlockSpec((1, 1, D_RNN), lambda b, t: (b, 0, 0))],
        scratch_shapes=[pltpu.VMEM((_XP_PAD + tt, D_RNN), F32),
                        pltpu.VMEM((1, D_RNN), F32)],
        compiler_params=_params("mixer_a", 2),
        name="mixer_a",
    )(gx, gx, conv_prev, h_prev.reshape(batch, 1, D_RNN), p["conv_w"], p["conv_b"],
      p["wa_bd"], p["ba"], p["wx_bd"], p["bx"], p["lam"])


def _rel_bias(d, tab_ref, h):
    n = jnp.abs(d)
    large = jnp.full(d.shape, _MAX_EXACT, jnp.int32)
    for thr in _BUCKET_THRESHOLDS:
        large = large + (n >= thr).astype(jnp.int32)
    bucket = jnp.where(d > 0, _NB, 0) + jnp.where(n < _MAX_EXACT, n, large)
    val = jnp.full(d.shape, tab_ref[0, h], F32)
    for b in range(1, N_BUCKETS):
        val = jnp.where(bucket == b, tab_ref[b, h], val)
    return val


def _split_q(q):
    lane = lax.broadcasted_iota(jnp.int32, q.shape, 1)
    zero = jnp.zeros_like(q)
    return jnp.where(lane < B_DH, q, zero), jnp.where(lane >= B_DH, q, zero)


def _diff_lambda(lamv_ref, lam_init):
    lv = lamv_ref[...]
    s01 = jnp.sum(lv[0:1, :] * lv[1:2, :], axis=-1, keepdims=True)
    s23 = jnp.sum(lv[2:3, :] * lv[3:4, :], axis=-1, keepdims=True)
    return jnp.exp(s01) - jnp.exp(s23) + lam_init


def _diff_finish(o1, l1, o2, l2, lam, sub, lam_init):
    o = o1 / l1 - lam * (o2 / l2)
    return _rms_rows(o, sub) * (1.0 - lam_init)


def _attn_prompt_kernel(tab_ref, lamv_ref, subc_ref, q_ref, k_ref, vt_ref, o_ref,
                        b0_ref, b1_ref, qz_ref, s_ref, m_ref, l_ref, acc_ref, *, tq, tk, lam_init):
    h = pl.program_id(0)
    qi = pl.program_id(1)
    nq = tq // tk
    far_bias = tab_ref[_FAR_BUCKET, h]

    @pl.when(qi == 0)
    def _():
        ki_ = lax.broadcasted_iota(jnp.int32, (tk, tk), 0)
        qi_ = lax.broadcasted_iota(jnp.int32, (tk, tk), 1)
        d = ki_ - qi_
        visible = (ki_ // CHUNK) <= (qi_ // CHUNK)
        b0_ref[...] = jnp.where(visible, (_rel_bias(d, tab_ref, h) - far_bias) * LOG2E,
                                MASK_VALUE)
        b1_ref[...] = (_rel_bias(d - tk, tab_ref, h) - far_bias) * LOG2E

    qt = q_ref[...].astype(F32).T
    row = lax.broadcasted_iota(jnp.int32, qt.shape, 0)
    qz_ref[:, 0:tq] = jnp.where(row < B_DH, qt, 0.0).astype(BF16)
    qz_ref[:, tq:2 * tq] = jnp.where(row >= B_DH, qt, 0.0).astype(BF16)
    m_ref[...] = jnp.full(m_ref.shape, MASK_VALUE, F32)
    l_ref[...] = jnp.zeros(l_ref.shape, F32)
    acc_ref[...] = jnp.zeros(acc_ref.shape, F32)

    def scores(ki, br, c, slot):
        r = pl.multiple_of(ki * tk, tk)
        c0 = br * tq + c * tk
        s_ref[slot] = _dot(k_ref[pl.ds(r, tk), :], qz_ref[:, c0:c0 + tk])

    def softmax_pv(ki, br, c, slot, bias_ref):
        c0 = br * tq + c * tk
        cols = slice(c0, c0 + tk)
        r = pl.multiple_of(ki * tk, tk)
        s = s_ref[slot]
        if bias_ref is not None:
            s = s + bias_ref[...]
        m_old = m_ref[:, cols]
        m_new = jnp.maximum(m_old, jnp.max(s, axis=0, keepdims=True))
        alpha = jnp.exp2(m_old - m_new)
        e = jnp.exp2(s - m_new)
        l_ref[:, cols] = alpha * l_ref[:, cols] + jnp.sum(e, axis=0, keepdims=True)
        acc_ref[:, cols] = (alpha * acc_ref[:, cols]
                            + _dot(vt_ref[:, pl.ds(r, tk)], e.astype(BF16)))
        m_ref[:, cols] = m_new

    def run_tile(ki, items, after):
        assert len(items) % 2 == 0
        for idx, (br, c, bias_ref) in enumerate(items):
            slot = idx % 2
            if idx + 1 < len(items):
                scores(ki, items[idx + 1][0], items[idx + 1][1], 1 - slot)
            elif after is not None:
                scores(ki + 1, 0, after, 1 - slot)
            softmax_pv(ki, br, c, slot, bias_ref)

    def block_bias(c, j):
        return b0_ref if c == j else b1_ref if c == j + 1 else None

    scores(0, 0, 0, 0)

    def far_body(ki, carry):
        run_tile(ki, [(br, c, None) for br in range(2) for c in range(nq)], 0)
        return carry
    lax.fori_loop(0, jnp.maximum(nq * qi - 1, 0), far_body, 0)

    @pl.when(qi >= 1)
    def _():
        run_tile(nq * qi - 1, [(br, c, block_bias(c, -1)) for br in range(2)
                               for c in range(nq)], 0)

    for j in range(nq):
        run_tile(nq * qi + j, [(br, c, block_bias(c, j)) for br in range(2)
                               for c in range(j, nq)], j + 1 if j + 1 < nq else None)

    lam = _diff_lambda(lamv_ref, lam_init)
    o = (acc_ref[:, 0:tq] / l_ref[:, 0:tq]
         - lam * (acc_ref[:, tq:2 * tq] / l_ref[:, tq:2 * tq]))
    ms = jnp.mean(o * o, axis=0, keepdims=True)
    o = o * lax.rsqrt(ms + EPS) * subc_ref[...] * (1.0 - lam_init)
    o_ref[...] = o.T.astype(o_ref.dtype)


def _attn_prompt(q, k, vt, p, lam_init, *, tq=2048, tk=512):
    s = q.shape[0]
    assert s % tq == 0 and tq % tk == 0 and tk % CHUNK == 0 and tk + 1 >= _FAR_DIST
    kern = functools.partial(_attn_prompt_kernel, tq=tq, tk=tk, lam_init=lam_init)
    return pl.pallas_call(
        kern,
        out_shape=jax.ShapeDtypeStruct((s, B_W), BF16),
        grid=(B_HEADS, s // tq),
        in_specs=[pl.BlockSpec(memory_space=pltpu.SMEM),
                  pl.BlockSpec((4, B_DH), lambda h, i: (0, 0)),
                  pl.BlockSpec((B_E, 1), lambda h, i: (0, 0)),
                  pl.BlockSpec((tq, B_E), lambda h, i: (i, h)),
                  pl.BlockSpec((s, B_E), lambda h, i: (0, h)),
                  pl.BlockSpec((B_E, s), lambda h, i: (h, 0))],
        out_specs=pl.BlockSpec((tq, B_E), lambda h, i: (i, h)),
        scratch_shapes=[pltpu.VMEM((tk, tk), F32), pltpu.VMEM((tk, tk), F32),
                        pltpu.VMEM((B_E, 2 * tq), BF16),
                        pltpu.VMEM((2, tk, tk), F32),
                        pltpu.VMEM((1, 2 * tq), F32), pltpu.VMEM((1, 2 * tq), F32),
                        pltpu.VMEM((B_E, 2 * tq), F32)],
        compiler_params=_params("attn_prompt", 2),
        name="attn_prompt",
    )(p["bias_table"], p["lam_vecs"], p["subln_w"].reshape(B_E, 1), q, k, vt)


_SAMPLE_COL_CHUNK = 2048


def _attn_sample_kernel(tab_ref, lamv_ref, sub_ref, q_ref, kn_ref, vn_ref, kc_ref, vc_ref, o_ref,
                        bc_ref, bn_ref, s_ref, *, t_new, past, lam_init):
    rows_h = 2 * t_new
    n_keys = past * B_HEADS

    @pl.when(pl.program_id(0) == 0)
    def _():
        t_c = lax.broadcasted_iota(jnp.int32, (rows_h, n_keys), 0) % t_new
        col = lax.broadcasted_iota(jnp.int32, (rows_h, n_keys), 1)
        d_c = col // B_HEADS - past - t_c
        head_c = col % B_HEADS
        t_n = lax.broadcasted_iota(jnp.int32, (rows_h, B_HEADS * t_new), 0) % t_new
        col_n = lax.broadcasted_iota(jnp.int32, (rows_h, B_HEADS * t_new), 1)
        d_n = col_n % t_new - t_n
        head_n = col_n // t_new
        for h in range(B_HEADS):
            rows = slice(h * rows_h, (h + 1) * rows_h)
            bc_ref[rows, :] = jnp.where(head_c == h, LOG2E * _rel_bias(d_c, tab_ref, h),
                                        MASK_VALUE)
            bn_ref[rows, :] = jnp.where(head_n == h, LOG2E * _rel_bias(d_n, tab_ref, h),
                                        MASK_VALUE)

    q_parts, kn_parts, vn_parts = [], [], []
    for h in range(B_HEADS):
        cols = slice(h * B_E, (h + 1) * B_E)
        q_parts.extend(_split_q(q_ref[:, cols]))
        kn_parts.append(kn_ref[:, cols])
        vn_parts.append(vn_ref[:, cols])
    qblk = jnp.concatenate(q_parts, axis=0)
    kn = jnp.concatenate(kn_parts, axis=0).astype(BF16)
    vn = jnp.concatenate(vn_parts, axis=0).astype(BF16)

    s_n = _dot_nt(qblk, kn) + bn_ref[...]
    m = jnp.max(s_n, axis=-1, keepdims=True)
    for c0 in range(0, n_keys, _SAMPLE_COL_CHUNK):
        cc = slice(c0, c0 + _SAMPLE_COL_CHUNK)
        s = _dot_nt(qblk, kc_ref[cc, :].astype(BF16)) + bc_ref[:, cc]
        s_ref[:, cc] = s
        m = jnp.maximum(m, jnp.max(s, axis=-1, keepdims=True))

    e_n = jnp.exp2(s_n - m)
    l = jnp.sum(e_n, axis=-1, keepdims=True)
    o = _dot(e_n.astype(BF16), vn)
    for c0 in range(0, n_keys, _SAMPLE_COL_CHUNK):
        cc = slice(c0, c0 + _SAMPLE_COL_CHUNK)
        e = jnp.exp2(s_ref[:, cc] - m)
        l = l + jnp.sum(e, axis=-1, keepdims=True)
        o = o + _dot(e.astype(BF16), vc_ref[cc, :].astype(BF16))

    lam = _diff_lambda(lamv_ref, lam_init)
    for h in range(B_HEADS):
        r1 = slice(h * rows_h, h * rows_h + t_new)
        r2 = slice(h * rows_h + t_new, (h + 1) * rows_h)
        o_ref[:, h * B_E:(h + 1) * B_E] = _diff_finish(
            o[r1], l[r1], o[r2], l[r2], lam, sub_ref[...], lam_init).astype(o_ref.dtype)


def _attn_sample(q, k_new, v_new, k_cache, v_cache, layer, p, lam_init, *, batch, t_new):
    n_layers, _, past, _, _ = k_cache.shape
    n_keys = past * B_HEADS
    assert n_keys % _SAMPLE_COL_CHUNK == 0
    kern = functools.partial(_attn_sample_kernel, t_new=t_new, past=past, lam_init=lam_init)
    new_blk = pl.BlockSpec((t_new, B_W), lambda b: (b, 0))
    cache_blk = pl.BlockSpec((None, None, n_keys, B_E), lambda b: (layer, b, 0, 0))
    n_rows = B_HEADS * 2 * t_new
    return pl.pallas_call(
        kern,
        out_shape=jax.ShapeDtypeStruct((batch * t_new, B_W), BF16),
        grid=(batch,),
        in_specs=[pl.BlockSpec(memory_space=pltpu.SMEM),
                  pl.BlockSpec((4, B_DH), lambda b: (0, 0)),
                  pl.BlockSpec((1, B_E), lambda b: (0, 0)),
                  new_blk, new_blk, new_blk, cache_blk, cache_blk],
        out_specs=new_blk,
        scratch_shapes=[pltpu.VMEM((n_rows, n_keys), F32),
                        pltpu.VMEM((n_rows, B_HEADS * t_new), F32),
                        pltpu.VMEM((n_rows, n_keys), F32)],
        compiler_params=_params("attn_sample", 1),
        name="attn_sample",
    )(p["bias_table"], p["lam_vecs"], p["subln_w"], q, k_new, v_new,
      k_cache.reshape(n_layers, batch, n_keys, B_E), v_cache.reshape(n_layers, batch, n_keys, B_E))


def _gating_kernel(u_ref, v_ref, vnw_ref, ws_ref, bs_ref, *refs, chunk, n_chunks, emit_v):
    if emit_v:
        o_ref, vout_ref, wsm_ref = refs
    else:
        o_ref, wsm_ref = refs
        vout_ref = None

    @pl.when(pl.program_id(0) == 0)
    def _():
        ri = lax.broadcasted_iota(jnp.int32, (chunk, chunk), 0)
        ci = lax.broadcasted_iota(jnp.int32, (chunk, chunk), 1)
        for g in range(C_GROUPS):
            wsm_ref[g] = jnp.where(ci <= ri, ws_ref[g], 0.0).astype(BF16)

    for c in range(n_chunks):
        r0 = c * chunk
        vn = _rms_rows(v_ref[r0:r0 + chunk, :], vnw_ref[...])
        if emit_v:
            vout_ref[r0:r0 + chunk, :] = vn
        vnb = vn.astype(BF16)
        for g in range(C_GROUPS):
            cols = slice(g * C_GW, (g + 1) * C_GW)
            mix = _dot(wsm_ref[g], vnb[:, cols]) + bs_ref[:, g:g + 1]
            o_ref[r0:r0 + chunk, cols] = (u_ref[r0:r0 + chunk, cols] * mix).astype(o_ref.dtype)


def _gating(z, vnw, ws, bs_t, *, chunk, n_chunks, emit_v):
    m = z.shape[0]
    rt = chunk * n_chunks
    assert m % rt == 0
    kern = functools.partial(_gating_kernel, chunk=chunk, n_chunks=n_chunks, emit_v=emit_v)
    out_shape = [jax.ShapeDtypeStruct((m, D_C), BF16)]
    out_specs = [pl.BlockSpec((rt, D_C), lambda i: (i, 0))]
    if emit_v:
        out_shape.append(jax.ShapeDtypeStruct((m, D_C), F32))
        out_specs.append(pl.BlockSpec((rt, D_C), lambda i: (i, 0)))
    res = pl.pallas_call(
        kern,
        out_shape=out_shape,
        grid=(m // rt,),
        in_specs=[pl.BlockSpec((rt, D_C), lambda i: (i, 0)),
                  pl.BlockSpec((rt, D_C), lambda i: (i, 1)),
                  pl.BlockSpec((1, D_C), lambda i: (0, 0)),
                  pl.BlockSpec((C_GROUPS, chunk, chunk), lambda i: (0, 0, 0)),
                  pl.BlockSpec((chunk, C_GROUPS), lambda i: (0, 0))],
        out_specs=out_specs,
        scratch_shapes=[pltpu.VMEM((C_GROUPS, chunk, chunk), BF16)],
        compiler_params=_params("gating", 1),
        name="gating",
    )(z, z, vnw.reshape(1, D_C), ws, bs_t)
    return res if emit_v else (res[0], None)


def _block_diag(w):
    per = MXU_DIM // LRU_BW
    w4 = w.reshape(LRU_BLOCKS // per, per, LRU_BW, LRU_BW)
    eye = jnp.eye(per, dtype=w.dtype)
    bd = jnp.einsum("gaij,ab->gaibj", w4, eye)
    return bd.reshape(LRU_BLOCKS // per, MXU_DIM, MXU_DIM).astype(BF16)


def kernel(x_prompt, x_sample, state_conv, state_lru, cache_k, cache_v, w_even_in, w_even_out,
           conv_w, conv_b, lru_wa, lru_ba, lru_wx, lru_bx, lru_lambda, lam_vecs, subln_w,
           rel_bias_table, w_odd_in, w_odd_out, gmlp_vnorm_w, gmlp_ws, gmlp_bs, norm_mix_w,
           norm_ffn_w, norm_final_w, w_ff_up, w_ff_down):
    batch_p, seq_p, _ = x_prompt.shape
    batch_s, seq_s, _ = x_sample.shape
    assert batch_p == 1
    yp = x_prompt.reshape(batch_p * seq_p, D_MODEL)
    ys = x_sample.reshape(batch_s * seq_s, D_MODEL)
    tm_p = 1024
    tm_s = batch_s * seq_s
    scale = B_DH ** -0.5 * LOG2E

    q0, k0, v0 = 2 * D_RNN, 2 * D_RNN + B_W, 2 * D_RNN + 2 * B_W
    even_segs = [(0, 2 * D_RNN, F32, 1.0, False), (q0, B_W, BF16, scale, False),
                 (k0, B_W, F32, 1.0, False), (v0, B_W, F32, 1.0, False)]
    even_segs_p = even_segs + [(k0, B_W, BF16, 1.0, False), (v0, B_W, BF16, 1.0, True)]
    odd_segs = [(0, 2 * D_C, F32, 1.0, False)]
    zeros_conv = jnp.zeros((batch_p, CONV_W - 1, D_RNN), F32)
    zeros_h = jnp.zeros((batch_p, D_RNN), F32)

    p_conv, p_lru, p_k, p_v = [], [], [], []
    s_conv, s_lru, s_k, s_v, s_gv = [], [], [], [], []
    for l in range(DEPTH):
        if l % 2 == 0:
            e = l // 2
            lam_init = 0.8 - 0.6 * math.exp(-0.3 * l)
            pa = dict(conv_w=conv_w[e], conv_b=conv_b[e].reshape(1, D_RNN),
                      wa_bd=_block_diag(lru_wa[e]), ba=lru_ba[e].reshape(1, D_RNN),
                      wx_bd=_block_diag(lru_wx[e]), bx=lru_bx[e].reshape(1, D_RNN),
                      lam=lru_lambda[e].reshape(1, D_RNN))
            pb = dict(bias_table=rel_bias_table, lam_vecs=lam_vecs[e],
                      subln_w=subln_w[e].reshape(1, B_E))
            gx, q, k, v, w_in_b = _norm_matmul(ys, norm_mix_w[l], w_even_in, e, even_segs,
                                               tm=tm_s, emit_bf16=True)
            ya, c_new, h_new = _mixer_a(gx, state_conv[e], state_lru[e], pa, batch=batch_s,
                                        seq=seq_s, tt=seq_s)
            yb = _attn_sample(q, k, v, cache_k, cache_v, e, pb, lam_init, batch=batch_s,
                              t_new=seq_s)
            ys, w_out_b = _matmul_res([ya, yb], w_even_out, e, ys, tm=tm_s, emit_bf16=True)
            s_conv.append(c_new)
            s_lru.append(h_new.reshape(batch_s, D_RNN))
            s_k.append(k.reshape(batch_s, seq_s, B_HEADS, B_E))
            s_v.append(v.reshape(batch_s, seq_s, B_HEADS, B_E))
            gx, q, k, v, k_b, vt_b = _norm_matmul(yp, norm_mix_w[l], w_in_b, 0, even_segs_p,
                                                  tm=tm_p)
            ya, c_new, h_new = _mixer_a(gx, zeros_conv, zeros_h, pa, batch=batch_p, seq=seq_p,
                                        tt=512)
            yb = _attn_prompt(q, k_b, vt_b, pb, lam_init)
            yp = _matmul_res([ya, yb], w_out_b, 0, yp, tm=tm_p, resident_w=True)
            p_conv.append(c_new)
            p_lru.append(h_new.reshape(batch_p, D_RNN))
            p_k.append(k.reshape(batch_p, seq_p, B_HEADS, B_E))
            p_v.append(v.reshape(batch_p, seq_p, B_HEADS, B_E))
        else:
            o = l // 2
            z, w_in_b = _norm_matmul(ys, norm_mix_w[l], w_odd_in, o, odd_segs, act="gelu",
                                     tm=tm_s, emit_bf16=True)
            gated, gv = _gating(z, gmlp_vnorm_w[o], gmlp_ws[o][:, :seq_s, :seq_s],
                                gmlp_bs[o][:, :seq_s].T, chunk=seq_s, n_chunks=1, emit_v=True)
            ys, w_out_b = _matmul_res([gated], w_odd_out, o, ys, tm=tm_s, emit_bf16=True)
            s_gv.append(gv.reshape(batch_s, seq_s, D_C))
            (z,) = _norm_matmul(yp, norm_mix_w[l], w_in_b, 0, odd_segs, act="gelu", tm=tm_p,
                                tn=1024)
            gated, _ = _gating(z, gmlp_vnorm_w[o], gmlp_ws[o], gmlp_bs[o].T, chunk=GMLP_CHUNK,
                               n_chunks=4, emit_v=False)
            yp = _matmul_res([gated], w_out_b, 0, yp, tm=tm_p, resident_w=True)
        last = l == DEPTH - 1
        ys, wu_b, wd_b = _ffn(ys, norm_ffn_w[l], w_ff_up, w_ff_down, l, norm_final_w, tm=tm_s,
                              final=last, emit_bf16=True)
        yp = _ffn(yp, norm_ffn_w[l], wu_b, wd_b, 0, norm_final_w, tm=tm_p, final=last)

    return (yp.reshape(batch_p, seq_p, D_MODEL), ys.reshape(batch_s, seq_s, D_MODEL),
            jnp.stack(p_conv), jnp.stack(p_lru), jnp.stack(p_k), jnp.stack(p_v),
            jnp.stack(s_conv), jnp.stack(s_lru), jnp.stack(s_k), jnp.stack(s_v),
            jnp.stack(s_gv))
```

```python
import functools
import math

import jax
import jax.numpy as jnp
from jax import lax
from jax.experimental import pallas as pl
from jax.experimental.pallas import tpu as pltpu

D_MODEL = 2048
DEPTH = 4
CHUNK = 64
EPS = 1e-6
D_RNN = D_MODEL // 2
LRU_BLOCKS = 16
LRU_BW = D_RNN // LRU_BLOCKS
CONV_W = 4
LRU_C = 8.0
B_HEADS = 8
B_DH = 64
B_E = 2 * B_DH
B_W = B_HEADS * B_E
N_BUCKETS = 32
MAX_DIST = 512
GMLP_CHUNK = 128
D_C = D_MODEL
C_GROUPS = 16
C_GW = D_C // C_GROUPS
D_FF = 4 * D_MODEL

F32 = jnp.float32
BF16 = jnp.bfloat16
MASK_VALUE = -1e30
LOG2E = math.log2(math.e)
MIB = 1024 * 1024
MXU_DIM = 256
NORM_ROWS = 256
_VMEM_MIB = dict(norm_matmul=56, matmul_res=48, ffn=56, mixer_a=40, attn_prompt=48,
                 attn_sample=48, gating=40)

_NB = N_BUCKETS // 2
_MAX_EXACT = _NB // 2
_BUCKET_THRESHOLDS = tuple(
    int(math.ceil(_MAX_EXACT * (MAX_DIST / _MAX_EXACT) ** (k / (_NB - _MAX_EXACT)) - 1e-9))
    for k in range(1, _NB - _MAX_EXACT))
_BUCKET_LOWER_BOUNDS = tuple(range(1, _MAX_EXACT + 1)) + _BUCKET_THRESHOLDS
assert len(_BUCKET_LOWER_BOUNDS) == _NB - 1
_FAR_BUCKET = _NB - 1
_FAR_DIST = _BUCKET_THRESHOLDS[-1]


def _dot(a, b):
    return jnp.dot(a, b, preferred_element_type=F32)


def _dot_nt(a, b):
    return lax.dot_general(a, b, (((1,), (1,)), ((), ())), preferred_element_type=F32)


def _rms_rows(x, w):
    ms = jnp.mean(x * x, axis=-1, keepdims=True)
    return x * lax.rsqrt(ms + EPS) * w


def _gelu_tanh(x):
    c = math.sqrt(2.0 / math.pi)
    return x * (0.5 * (1.0 + jnp.tanh(c * (x + 0.044715 * (x * x * x)))))


def _sigmoid(x):
    return 0.5 * jnp.tanh(0.5 * x) + 0.5


def _params(name, grid_rank):
    return pltpu.CompilerParams(dimension_semantics=("arbitrary",) * grid_rank,
                                vmem_limit_bytes=_VMEM_MIB[name] * MIB)


def _norm_matmul_kernel(x_ref, nw_ref, w_ref, *refs, segs, act, tm, norm_rows, emit_bf16):
    out_refs = refs[:len(segs)]
    n = pl.program_id(1)
    if emit_bf16:
        wb_ref, xn_ref = refs[len(segs):]
        wb_ref[...] = w_ref[...].astype(BF16)
        w_ref = wb_ref
    else:
        xn_ref = refs[len(segs)]

    def project(xn):
        acc = _dot(xn, w_ref[...])
        return _gelu_tanh(acc) if act == "gelu" else acc

    lo0, hi0, scale0, transposed0 = segs[0]
    assert lo0 == 0 and hi0 > 0 and not transposed0

    @pl.when(n == 0)
    def _():
        for r in range(0, tm, norm_rows):
            rows = slice(r, r + norm_rows)
            xn = _rms_rows(x_ref[rows, :], nw_ref[...]).astype(BF16)
            xn_ref[rows, :] = xn
            acc = project(xn)
            val = acc if scale0 == 1.0 else acc * scale0
            out_refs[0][rows, :] = val.astype(out_refs[0].dtype)

    @pl.when(n > 0)
    def _():
        acc = project(xn_ref[...])
        for (lo, hi, scale, transposed), o_ref in zip(segs, out_refs):
            def store(o_ref=o_ref, scale=scale, transposed=transposed):
                val = acc if scale == 1.0 else acc * scale
                if transposed:
                    val = val.T
                o_ref[...] = val.astype(o_ref.dtype)
            if len(segs) == 1:
                store()
            else:
                pl.when((n >= lo) & (n < hi))(store)


def _norm_matmul(x, nw, w, layer, seg_defs, *, act=None, tm, tn=512, emit_bf16=False):
    m, d = x.shape
    n_total = w.shape[2]
    assert m % tm == 0 and n_total % tn == 0 and (not emit_bf16 or m == tm)
    segs, out_shapes, out_specs = [], [], []
    for first, cols, dtype, scale, transposed in seg_defs:
        assert cols % tn == 0 and first % tn == 0
        lo, nb = first // tn, cols // tn
        segs.append((lo, lo + nb, scale, transposed))
        if transposed:
            out_shapes.append(jax.ShapeDtypeStruct((cols, m), dtype))
            out_specs.append(pl.BlockSpec(
                (tn, tm), lambda i, j, lo=lo, nb=nb: (jnp.clip(j - lo, 0, nb - 1), i)))
        else:
            out_shapes.append(jax.ShapeDtypeStruct((m, cols), dtype))
            out_specs.append(pl.BlockSpec(
                (tm, tn), lambda i, j, lo=lo, nb=nb: (i, jnp.clip(j - lo, 0, nb - 1))))
    if emit_bf16:
        out_shapes.append(jax.ShapeDtypeStruct((1, d, n_total), BF16))
        out_specs.append(pl.BlockSpec((None, d, tn), lambda i, j: (0, 0, j)))
    kern = functools.partial(_norm_matmul_kernel, segs=tuple(segs), act=act, tm=tm,
                             norm_rows=min(tm, NORM_ROWS), emit_bf16=emit_bf16)
    return pl.pallas_call(
        kern,
        out_shape=out_shapes,
        grid=(m // tm, n_total // tn),
        in_specs=[pl.BlockSpec((tm, d), lambda i, j: (i, 0)),
                  pl.BlockSpec((1, d), lambda i, j: (0, 0)),
                  pl.BlockSpec((None, d, tn), lambda i, j: (layer, 0, j))],
        out_specs=out_specs,
        scratch_shapes=[pltpu.VMEM((tm, d), BF16)],
        compiler_params=_params("norm_matmul", 2),
        name="norm_matmul",
    )(x, nw.reshape(1, d), w)


def _matmul_res_kernel(*refs, n_in, tn, emit_bf16, resident_w):
    a_refs = refs[:n_in]
    w_refs = refs[n_in:2 * n_in]
    res_ref = refs[2 * n_in]
    o_ref = refs[2 * n_in + 1]
    if resident_w:
        c0 = pl.multiple_of(pl.program_id(1) * tn, tn)
        w_refs = [w_ref.at[:, pl.ds(c0, tn)] for w_ref in w_refs]
    acc = res_ref[...]
    if emit_bf16:
        wb_ref = refs[2 * n_in + 2]
        for b, (a_ref, w_ref) in enumerate(zip(a_refs, w_refs)):
            wb_ref[b] = w_ref[...].astype(BF16)
            acc = acc + _dot(a_ref[...], wb_ref[b])
    else:
        for a_ref, w_ref in zip(a_refs, w_refs):
            acc = acc + _dot(a_ref[...], w_ref[...])
    o_ref[...] = acc


def _matmul_res(a_list, w, layer, res, *, tm, tn=1024, emit_bf16=False, resident_w=False):
    m, n_total = res.shape
    n_in = len(a_list)
    k_total = w.shape[1]
    kb = k_total // n_in
    assert all(a.shape == (m, kb) for a in a_list) and m % tm == 0 and n_total % tn == 0
    assert (not emit_bf16 or m == tm) and not (emit_bf16 and resident_w)
    in_specs = [pl.BlockSpec((tm, kb), lambda i, j: (i, 0)) for _ in a_list]
    if resident_w:
        in_specs += [pl.BlockSpec((None, kb, n_total), lambda i, j, b=b: (layer, b, 0),
                                  pipeline_mode=pl.Buffered(1)) for b in range(n_in)]
    else:
        in_specs += [pl.BlockSpec((None, kb, tn), lambda i, j, b=b: (layer, b, j))
                     for b in range(n_in)]
    in_specs += [pl.BlockSpec((tm, tn), lambda i, j: (i, j))]
    out_shape = [jax.ShapeDtypeStruct((m, n_total), F32)]
    out_specs = [pl.BlockSpec((tm, tn), lambda i, j: (i, j))]
    if emit_bf16:
        out_shape += [jax.ShapeDtypeStruct((n_in, kb, n_total), BF16)]
        out_specs += [pl.BlockSpec((n_in, kb, tn), lambda i, j: (0, 0, j))]
    res_out = pl.pallas_call(
        functools.partial(_matmul_res_kernel, n_in=n_in, tn=tn, emit_bf16=emit_bf16,
                          resident_w=resident_w),
        out_shape=out_shape,
        grid=(m // tm, n_total // tn),
        in_specs=in_specs,
        out_specs=out_specs,
        compiler_params=_params("matmul_res", 2),
        name="matmul_res",
    )(*a_list, *([w] * n_in), res)
    if emit_bf16:
        return res_out[0], res_out[1].reshape(1, k_total, n_total)
    return res_out[0]


def _ffn_kernel(x_ref, nw_ref, wu_ref, wd_ref, fw_ref, o_ref, *refs, tm, norm_rows, col_chunk,
                final, emit_bf16):
    if emit_bf16:
        wub_ref, wdb_ref, xn_ref = refs
        wub_ref[...] = wu_ref[...].astype(BF16)
        wdb_ref[...] = wd_ref[...].astype(BF16)
        wu_ref, wd_ref = wub_ref, wdb_ref
    else:
        (xn_ref,) = refs
    f = pl.program_id(1)
    d = o_ref.shape[1]

    def mlp(xn):
        h = jnp.maximum(_dot(xn, wu_ref[...]), 0.0)
        return (h * h).astype(BF16)

    @pl.when(f == 0)
    def _():
        for r in range(0, tm, norm_rows):
            rows = slice(r, r + norm_rows)
            x = x_ref[rows, :]
            xn = _rms_rows(x, nw_ref[...]).astype(BF16)
            xn_ref[rows, :] = xn
            h = mlp(xn)
            for c0 in range(0, d, col_chunk):
                cols = slice(c0, c0 + col_chunk)
                o_ref[rows, cols] = x[:, cols] + _dot(h, wd_ref[:, cols])

    last = pl.num_programs(1) - 1

    @pl.when((f > 0) & (f < last) if final else f > 0)
    def _():
        h = mlp(xn_ref[...])
        for c0 in range(0, d, col_chunk):
            o_ref[:, c0:c0 + col_chunk] += _dot(h, wd_ref[:, c0:c0 + col_chunk])

    if final:
        @pl.when(f == last)
        def _():
            for r in range(0, tm, norm_rows):
                rows = slice(r, r + norm_rows)
                h = mlp(xn_ref[rows, :])
                y = jnp.concatenate(
                    [o_ref[rows, c0:c0 + col_chunk] + _dot(h, wd_ref[:, c0:c0 + col_chunk])
                     for c0 in range(0, d, col_chunk)], axis=1)
                o_ref[rows, :] = _rms_rows(y, fw_ref[...])


def _ffn(x, nw, wu, wd, layer, fw, *, tm, tf=512, final=False, emit_bf16=False):
    m, d = x.shape
    dff = wu.shape[2]
    assert m % tm == 0 and dff % tf == 0 and (not emit_bf16 or m == tm)
    assert dff // tf >= 2
    kern = functools.partial(_ffn_kernel, tm=tm, norm_rows=min(tm, NORM_ROWS), col_chunk=512,
                             final=final, emit_bf16=emit_bf16)
    out_shape = [jax.ShapeDtypeStruct((m, d), F32)]
    out_specs = [pl.BlockSpec((tm, d), lambda i, f: (i, 0))]
    if emit_bf16:
        out_shape += [jax.ShapeDtypeStruct((1, d, dff), BF16),
                      jax.ShapeDtypeStruct((1, dff, d), BF16)]
        out_specs += [pl.BlockSpec((None, d, tf), lambda i, f: (0, 0, f)),
                      pl.BlockSpec((None, tf, d), lambda i, f: (0, f, 0))]
    res = pl.pallas_call(
        kern,
        out_shape=out_shape,
        grid=(m // tm, dff // tf),
        in_specs=[pl.BlockSpec((tm, d), lambda i, f: (i, 0)),
                  pl.BlockSpec((1, d), lambda i, f: (0, 0)),
                  pl.BlockSpec((None, d, tf), lambda i, f: (layer, 0, f)),
                  pl.BlockSpec((None, tf, d), lambda i, f: (layer, f, 0)),
                  pl.BlockSpec((1, d), lambda i, f: (0, 0))],
        out_specs=out_specs,
        scratch_shapes=[pltpu.VMEM((tm, d), BF16)],
        compiler_params=_params("ffn", 2),
        name="ffn",
    )(x, nw.reshape(1, d), wu, wd, fw.reshape(1, d))
    return res if emit_bf16 else res[0]


_XP_PAD = 8


def _scan_group(a, u, hb, row):
    for dist in (1, 2, 4):
        keep = row >= dist
        a_s = jnp.where(keep, pltpu.roll(a, dist, 0), 1.0)
        u_s = jnp.where(keep, pltpu.roll(u, dist, 0), 0.0)
        u = a * u_s + u
        a = a * a_s
    return a * hb + u


def _mixer_a_kernel(g_ref, xr_ref, cprev_ref, hprev_ref, cw_ref, cb_ref, wa_ref, ba_ref,
                    wx_ref, bx_ref, lam_ref, ya_ref, conv_ref, hlast_ref, xp_ref, h_ref,
                    *, tt, rc):
    t = pl.program_id(1)
    hist = CONV_W - 1

    @pl.when(t == 0)
    def _():
        xp_ref[_XP_PAD - hist:_XP_PAD, :] = cprev_ref[0]
        h_ref[...] = hprev_ref[0]

    @pl.when(t > 0)
    def _():
        xp_ref[_XP_PAD - hist:_XP_PAD, :] = xp_ref[_XP_PAD + tt - hist:_XP_PAD + tt, :]

    xp_ref[_XP_PAD:_XP_PAD + tt, :] = xr_ref[...]

    z = -lam_ref[...]
    softplus = jnp.maximum(z, 0.0) + jnp.log1p(jnp.exp(-jnp.abs(z)))
    row = lax.broadcasted_iota(jnp.int32, (8, D_RNN), 0)
    nblk = D_RNN // MXU_DIM
    hb = jnp.broadcast_to(h_ref[...], (8, D_RNN))

    for c in range(tt // rc):
        r0 = c * rc
        xc = cb_ref[...]
        for j in range(CONV_W):
            s = _XP_PAD - hist + j + r0
            xc = xc + xp_ref[s:s + rc, :] * cw_ref[j:j + 1, :]
        xcb = xc.astype(BF16)
        r_lin = jnp.concatenate(
            [_dot(xcb[:, MXU_DIM * b:MXU_DIM * (b + 1)], wa_ref[b]) for b in range(nblk)],
            axis=1)
        i_lin = jnp.concatenate(
            [_dot(xcb[:, MXU_DIM * b:MXU_DIM * (b + 1)], wx_ref[b]) for b in range(nblk)],
            axis=1)
        r = _sigmoid(r_lin + ba_ref[...])
        i = _sigmoid(i_lin + bx_ref[...])
        log_a = (-LRU_C) * r * softplus
        a = jnp.exp(log_a)
        th = jnp.tanh(log_a)
        u = jnp.sqrt((-2.0 * th) / (1.0 - th)) * (i * xc)
        hs = []
        for k in range(rc // 8):
            hrows = _scan_group(a[8 * k:8 * k + 8], u[8 * k:8 * k + 8], hb, row)
            hb = jnp.broadcast_to(hrows[7:8, :], (8, D_RNN))
            hs.append(hrows)
        hs = jnp.concatenate(hs, axis=0)
        g = g_ref[r0:r0 + rc, :]
        ya_ref[r0:r0 + rc, :] = (_gelu_tanh(g) * hs).astype(ya_ref.dtype)

    h_ref[...] = hb[0:1, :]
    conv_ref[0] = xr_ref[tt - hist:tt, :]
    hlast_ref[0] = hb[0:1, :]


def _mixer_a(gx, conv_prev, h_prev, p, *, batch, seq, tt):
    assert seq % tt == 0 and tt % 8 == 0
    rc = min(tt, 64)
    nt = seq // tt
    kern = functools.partial(_mixer_a_kernel, tt=tt, rc=rc)
    row_blk = lambda b, t: (b * nt + t, 0)
    vec = lambda b, t: (0, 0)
    return pl.pallas_call(
        kern,
        out_shape=[jax.ShapeDtypeStruct((batch * seq, D_RNN), BF16),
                   jax.ShapeDtypeStruct((batch, CONV_W - 1, D_RNN), F32),
                   jax.ShapeDtypeStruct((batch, 1, D_RNN), F32)],
        grid=(batch, nt),
        in_specs=[pl.BlockSpec((tt, D_RNN), row_blk),
                  pl.BlockSpec((tt, D_RNN), lambda b, t: (b * nt + t, 1)),
                  pl.BlockSpec((1, CONV_W - 1, D_RNN), lambda b, t: (b, 0, 0)),
                  pl.BlockSpec((1, 1, D_RNN), lambda b, t: (b, 0, 0)),
                  pl.BlockSpec((CONV_W, D_RNN), vec),
                  pl.BlockSpec((1, D_RNN), vec),
                  pl.BlockSpec((D_RNN // MXU_DIM, MXU_DIM, MXU_DIM), lambda b, t: (0, 0, 0)),
                  pl.BlockSpec((1, D_RNN), vec),
                  pl.BlockSpec((D_RNN // MXU_DIM, MXU_DIM, MXU_DIM), lambda b, t: (0, 0, 0)),
                  pl.BlockSpec((1, D_RNN), vec),
                  pl.BlockSpec((1, D_RNN), vec)],
        out_specs=[pl.BlockSpec((tt, D_RNN), row_blk),
                   pl.BlockSpec((1, CONV_W - 1, D_RNN), lambda b, t: (b, 0, 0)),
                   pl.BlockSpec((1, 1, D_RNN), lambda b, t: (b, 0, 0))],
        scratch_shapes=[pltpu.VMEM((_XP_PAD + tt, D_RNN), F32),
                        pltpu.VMEM((1, D_RNN), F32)],
        compiler_params=_params("mixer_a", 2),
        name="mixer_a",
    )(gx, gx, conv_prev, h_prev.reshape(batch, 1, D_RNN), p["conv_w"], p["conv_b"],
      p["wa_bd"], p["ba"], p["wx_bd"], p["bx"], p["lam"])


def _rel_bias(d, tab_ref, h, *, may_be_positive=True):
    n = jnp.abs(d)

    def one_sign(first_bucket):
        val = jnp.full(d.shape, tab_ref[first_bucket, h], F32)
        for k, bound in enumerate(_BUCKET_LOWER_BOUNDS, start=1):
            val = jnp.where(n >= bound, tab_ref[first_bucket + k, h], val)
        return val

    behind = one_sign(0)
    if not may_be_positive:
        return behind
    return jnp.where(d > 0, one_sign(_NB), behind)


def _split_q(q):
    lane = lax.broadcasted_iota(jnp.int32, q.shape, 1)
    zero = jnp.zeros_like(q)
    return jnp.where(lane < B_DH, q, zero), jnp.where(lane >= B_DH, q, zero)


def _diff_lambda(lamv_ref, lam_init):
    lv = lamv_ref[...]
    s01 = jnp.sum(lv[0:1, :] * lv[1:2, :], axis=-1, keepdims=True)
    s23 = jnp.sum(lv[2:3, :] * lv[3:4, :], axis=-1, keepdims=True)
    return jnp.exp(s01) - jnp.exp(s23) + lam_init


def _diff_finish(o1, l1, o2, l2, lam, sub, lam_init):
    o = o1 / l1 - lam * (o2 / l2)
    return _rms_rows(o, sub) * (1.0 - lam_init)


def _attn_prompt_kernel(tab_ref, lamv_ref, subc_ref, q_ref, k_ref, vt_ref, o_ref,
                        b0_ref, b1_ref, qz_ref, s_ref, m_ref, l_ref, acc_ref, *, tq, tk, lam_init):
    h = pl.program_id(0)
    qi = pl.program_id(1)
    nq = tq // tk
    far_bias = tab_ref[_FAR_BUCKET, h]

    @pl.when(qi == 0)
    def _():
        ki_ = lax.broadcasted_iota(jnp.int32, (tk, tk), 0)
        qi_ = lax.broadcasted_iota(jnp.int32, (tk, tk), 1)
        d = ki_ - qi_
        visible = (ki_ // CHUNK) <= (qi_ // CHUNK)
        b0_ref[...] = jnp.where(visible, (_rel_bias(d, tab_ref, h) - far_bias) * LOG2E,
                                MASK_VALUE)
        b1_ref[...] = (_rel_bias(d - tk, tab_ref, h, may_be_positive=False) - far_bias) * LOG2E

    qt = q_ref[...].astype(F32).T
    row = lax.broadcasted_iota(jnp.int32, qt.shape, 0)
    qz_ref[:, 0:tq] = jnp.where(row < B_DH, qt, 0.0).astype(BF16)
    qz_ref[:, tq:2 * tq] = jnp.where(row >= B_DH, qt, 0.0).astype(BF16)
    m_ref[...] = jnp.full(m_ref.shape, MASK_VALUE, F32)
    l_ref[...] = jnp.zeros(l_ref.shape, F32)
    acc_ref[...] = jnp.zeros(acc_ref.shape, F32)

    def scores(ki, br, c, slot):
        r = pl.multiple_of(ki * tk, tk)
        c0 = br * tq + c * tk
        s_ref[slot] = _dot(k_ref[pl.ds(r, tk), :], qz_ref[:, c0:c0 + tk])

    def softmax_pv(ki, br, c, slot, bias_ref):
        c0 = br * tq + c * tk
        cols = slice(c0, c0 + tk)
        r = pl.multiple_of(ki * tk, tk)
        s = s_ref[slot]
        if bias_ref is not None:
            s = s + bias_ref[...]
        m_old = m_ref[:, cols]
        m_new = jnp.maximum(m_old, jnp.max(s, axis=0, keepdims=True))
        alpha = jnp.exp2(m_old - m_new)
        e = jnp.exp2(s - m_new)
        l_ref[:, cols] = alpha * l_ref[:, cols] + jnp.sum(e, axis=0, keepdims=True)
        acc_ref[:, cols] = (alpha * acc_ref[:, cols]
                            + _dot(vt_ref[:, pl.ds(r, tk)], e.astype(BF16)))
        m_ref[:, cols] = m_new

    def run_tile(ki, items, after):
        assert len(items) % 2 == 0
        for idx, (br, c, bias_ref) in enumerate(items):
            slot = idx % 2
            if idx + 1 < len(items):
                scores(ki, items[idx + 1][0], items[idx + 1][1], 1 - slot)
            elif after is not None:
                scores(ki + 1, 0, after, 1 - slot)
            softmax_pv(ki, br, c, slot, bias_ref)

    def block_bias(c, j):
        return b0_ref if c == j else b1_ref if c == j + 1 else None

    scores(0, 0, 0, 0)

    def far_body(ki, carry):
        run_tile(ki, [(br, c, None) for br in range(2) for c in range(nq)], 0)
        return carry
    lax.fori_loop(0, jnp.maximum(nq * qi - 1, 0), far_body, 0)

    @pl.when(qi >= 1)
    def _():
        run_tile(nq * qi - 1, [(br, c, block_bias(c, -1)) for br in range(2)
                               for c in range(nq)], 0)

    for j in range(nq):
        run_tile(nq * qi + j, [(br, c, block_bias(c, j)) for br in range(2)
                               for c in range(j, nq)], j + 1 if j + 1 < nq else None)

    lam = _diff_lambda(lamv_ref, lam_init)
    o = (acc_ref[:, 0:tq] / l_ref[:, 0:tq]
         - lam * (acc_ref[:, tq:2 * tq] / l_ref[:, tq:2 * tq]))
    ms = jnp.mean(o * o, axis=0, keepdims=True)
    o = o * lax.rsqrt(ms + EPS) * subc_ref[...] * (1.0 - lam_init)
    o_ref[...] = o.T.astype(o_ref.dtype)


def _attn_prompt(q, k, vt, p, lam_init, *, tq=2048, tk=512):
    s = q.shape[0]
    assert s % tq == 0 and tq % tk == 0 and tk % CHUNK == 0 and tk + 1 >= _FAR_DIST
    kern = functools.partial(_attn_prompt_kernel, tq=tq, tk=tk, lam_init=lam_init)
    return pl.pallas_call(
        kern,
        out_shape=jax.ShapeDtypeStruct((s, B_W), BF16),
        grid=(B_HEADS, s // tq),
        in_specs=[pl.BlockSpec(memory_space=pltpu.SMEM),
                  pl.BlockSpec((4, B_DH), lambda h, i: (0, 0)),
                  pl.BlockSpec((B_E, 1), lambda h, i: (0, 0)),
                  pl.BlockSpec((tq, B_E), lambda h, i: (i, h)),
                  pl.BlockSpec((s, B_E), lambda h, i: (0, h)),
                  pl.BlockSpec((B_E, s), lambda h, i: (h, 0))],
        out_specs=pl.BlockSpec((tq, B_E), lambda h, i: (i, h)),
        scratch_shapes=[pltpu.VMEM((tk, tk), F32), pltpu.VMEM((tk, tk), F32),
                        pltpu.VMEM((B_E, 2 * tq), BF16),
                        pltpu.VMEM((2, tk, tk), F32),
                        pltpu.VMEM((1, 2 * tq), F32), pltpu.VMEM((1, 2 * tq), F32),
                        pltpu.VMEM((B_E, 2 * tq), F32)],
        compiler_params=_params("attn_prompt", 2),
        name="attn_prompt",
    )(p["bias_table"], p["lam_vecs"], p["subln_w"].reshape(B_E, 1), q, k, vt)


_SAMPLE_COL_CHUNK = 2048


def _attn_sample_kernel(tab_ref, lamv_ref, sub_ref, q_ref, kn_ref, vn_ref, kc_ref, vc_ref, o_ref,
                        bc_ref, bn_ref, s_ref, *, t_new, past, lam_init):
    rows_h = 2 * t_new
    n_keys = past * B_HEADS

    @pl.when(pl.program_id(0) == 0)
    def _():
        t_c = lax.broadcasted_iota(jnp.int32, (t_new, n_keys), 0)
        col = lax.broadcasted_iota(jnp.int32, (t_new, n_keys), 1)
        d_c = col // B_HEADS - past - t_c
        head_c = col % B_HEADS
        t_n = lax.broadcasted_iota(jnp.int32, (t_new, B_HEADS * t_new), 0)
        col_n = lax.broadcasted_iota(jnp.int32, (t_new, B_HEADS * t_new), 1)
        d_n = col_n % t_new - t_n
        head_n = col_n // t_new
        for h in range(B_HEADS):
            tile_c = jnp.where(head_c == h,
                               LOG2E * _rel_bias(d_c, tab_ref, h, may_be_positive=False),
                               MASK_VALUE)
            tile_n = jnp.where(head_n == h, LOG2E * _rel_bias(d_n, tab_ref, h), MASK_VALUE)
            for br in range(2):
                rows = slice(h * rows_h + br * t_new, h * rows_h + (br + 1) * t_new)
                bc_ref[rows, :] = tile_c
                bn_ref[rows, :] = tile_n

    q_parts, kn_parts, vn_parts = [], [], []
    for h in range(B_HEADS):
        cols = slice(h * B_E, (h + 1) * B_E)
        q_parts.extend(_split_q(q_ref[:, cols]))
        kn_parts.append(kn_ref[:, cols])
        vn_parts.append(vn_ref[:, cols])
    qblk = jnp.concatenate(q_parts, axis=0)
    kn = jnp.concatenate(kn_parts, axis=0).astype(BF16)
    vn = jnp.concatenate(vn_parts, axis=0).astype(BF16)

    s_n = _dot_nt(qblk, kn) + bn_ref[...]
    m = jnp.max(s_n, axis=-1, keepdims=True)
    for c0 in range(0, n_keys, _SAMPLE_COL_CHUNK):
        cc = slice(c0, c0 + _SAMPLE_COL_CHUNK)
        s = _dot_nt(qblk, kc_ref[cc, :].astype(BF16)) + bc_ref[:, cc]
        s_ref[:, cc] = s
        m = jnp.maximum(m, jnp.max(s, axis=-1, keepdims=True))

    e_n = jnp.exp2(s_n - m)
    l = jnp.sum(e_n, axis=-1, keepdims=True)
    o = _dot(e_n.astype(BF16), vn)
    for c0 in range(0, n_keys, _SAMPLE_COL_CHUNK):
        cc = slice(c0, c0 + _SAMPLE_COL_CHUNK)
        e = jnp.exp2(s_ref[:, cc] - m)
        l = l + jnp.sum(e, axis=-1, keepdims=True)
        o = o + _dot(e.astype(BF16), vc_ref[cc, :].astype(BF16))

    lam = _diff_lambda(lamv_ref, lam_init)
    for h in range(B_HEADS):
        r1 = slice(h * rows_h, h * rows_h + t_new)
        r2 = slice(h * rows_h + t_new, (h + 1) * rows_h)
        o_ref[:, h * B_E:(h + 1) * B_E] = _diff_finish(
            o[r1], l[r1], o[r2], l[r2], lam, sub_ref[...], lam_init).astype(o_ref.dtype)


def _attn_sample(q, k_new, v_new, k_cache, v_cache, layer, p, lam_init, *, batch, t_new):
    n_layers, _, past, _, _ = k_cache.shape
    n_keys = past * B_HEADS
    assert n_keys % _SAMPLE_COL_CHUNK == 0
    kern = functools.partial(_attn_sample_kernel, t_new=t_new, past=past, lam_init=lam_init)
    new_blk = pl.BlockSpec((t_new, B_W), lambda b: (b, 0))
    cache_blk = pl.BlockSpec((None, None, n_keys, B_E), lambda b: (layer, b, 0, 0))
    n_rows = B_HEADS * 2 * t_new
    return pl.pallas_call(
        kern,
        out_shape=jax.ShapeDtypeStruct((batch * t_new, B_W), BF16),
        grid=(batch,),
        in_specs=[pl.BlockSpec(memory_space=pltpu.SMEM),
                  pl.BlockSpec((4, B_DH), lambda b: (0, 0)),
                  pl.BlockSpec((1, B_E), lambda b: (0, 0)),
                  new_blk, new_blk, new_blk, cache_blk, cache_blk],
        out_specs=new_blk,
        scratch_shapes=[pltpu.VMEM((n_rows, n_keys), F32),
                        pltpu.VMEM((n_rows, B_HEADS * t_new), F32),
                        pltpu.VMEM((n_rows, n_keys), F32)],
        compiler_params=_params("attn_sample", 1),
        name="attn_sample",
    )(p["bias_table"], p["lam_vecs"], p["subln_w"], q, k_new, v_new,
      k_cache.reshape(n_layers, batch, n_keys, B_E), v_cache.reshape(n_layers, batch, n_keys, B_E))


def _gating_kernel(u_ref, v_ref, vnw_ref, ws_ref, bs_ref, *refs, chunk, n_chunks, emit_v):
    if emit_v:
        o_ref, vout_ref, wsm_ref = refs
    else:
        o_ref, wsm_ref = refs
        vout_ref = None

    @pl.when(pl.program_id(0) == 0)
    def _():
        ri = lax.broadcasted_iota(jnp.int32, (chunk, chunk), 0)
        ci = lax.broadcasted_iota(jnp.int32, (chunk, chunk), 1)
        for g in range(C_GROUPS):
            wsm_ref[g] = jnp.where(ci <= ri, ws_ref[g], 0.0).astype(BF16)

    for c in range(n_chunks):
        r0 = c * chunk
        vn = _rms_rows(v_ref[r0:r0 + chunk, :], vnw_ref[...])
        if emit_v:
            vout_ref[r0:r0 + chunk, :] = vn
        vnb = vn.astype(BF16)
        for g in range(C_GROUPS):
            cols = slice(g * C_GW, (g + 1) * C_GW)
            mix = _dot(wsm_ref[g], vnb[:, cols]) + bs_ref[:, g:g + 1]
            o_ref[r0:r0 + chunk, cols] = (u_ref[r0:r0 + chunk, cols] * mix).astype(o_ref.dtype)


def _gating(z, vnw, ws, bs_t, *, chunk, n_chunks, emit_v):
    m = z.shape[0]
    rt = chunk * n_chunks
    assert m % rt == 0
    kern = functools.partial(_gating_kernel, chunk=chunk, n_chunks=n_chunks, emit_v=emit_v)
    out_shape = [jax.ShapeDtypeStruct((m, D_C), BF16)]
    out_specs = [pl.BlockSpec((rt, D_C), lambda i: (i, 0))]
    if emit_v:
        out_shape.append(jax.ShapeDtypeStruct((m, D_C), F32))
        out_specs.append(pl.BlockSpec((rt, D_C), lambda i: (i, 0)))
    res = pl.pallas_call(
        kern,
        out_shape=out_shape,
        grid=(m // rt,),
        in_specs=[pl.BlockSpec((rt, D_C), lambda i: (i, 0)),
                  pl.BlockSpec((rt, D_C), lambda i: (i, 1)),
                  pl.BlockSpec((1, D_C), lambda i: (0, 0)),
                  pl.BlockSpec((C_GROUPS, chunk, chunk), lambda i: (0, 0, 0)),
                  pl.BlockSpec((chunk, C_GROUPS), lambda i: (0, 0))],
        out_specs=out_specs,
        scratch_shapes=[pltpu.VMEM((C_GROUPS, chunk, chunk), BF16)],
        compiler_params=_params("gating", 1),
        name="gating",
    )(z, z, vnw.reshape(1, D_C), ws, bs_t)
    return res if emit_v else (res[0], None)


def _block_diag(w):
    per = MXU_DIM // LRU_BW
    w4 = w.reshape(LRU_BLOCKS // per, per, LRU_BW, LRU_BW)
    eye = jnp.eye(per, dtype=w.dtype)
    bd = jnp.einsum("gaij,ab->gaibj", w4, eye)
    return bd.reshape(LRU_BLOCKS // per, MXU_DIM, MXU_DIM).astype(BF16)


def kernel(x_prompt, x_sample, state_conv, state_lru, cache_k, cache_v, w_even_in, w_even_out,
           conv_w, conv_b, lru_wa, lru_ba, lru_wx, lru_bx, lru_lambda, lam_vecs, subln_w,
           rel_bias_table, w_odd_in, w_odd_out, gmlp_vnorm_w, gmlp_ws, gmlp_bs, norm_mix_w,
           norm_ffn_w, norm_final_w, w_ff_up, w_ff_down):
    batch_p, seq_p, _ = x_prompt.shape
    batch_s, seq_s, _ = x_sample.shape
    assert batch_p == 1
    yp = x_prompt.reshape(batch_p * seq_p, D_MODEL)
    ys = x_sample.reshape(batch_s * seq_s, D_MODEL)
    tm_p = 1024
    tm_s = batch_s * seq_s
    scale = B_DH ** -0.5 * LOG2E

    q0, k0, v0 = 2 * D_RNN, 2 * D_RNN + B_W, 2 * D_RNN + 2 * B_W
    even_segs = [(0, 2 * D_RNN, F32, 1.0, False), (q0, B_W, BF16, scale, False),
                 (k0, B_W, F32, 1.0, False), (v0, B_W, F32, 1.0, False)]
    even_segs_p = even_segs + [(k0, B_W, BF16, 1.0, False), (v0, B_W, BF16, 1.0, True)]
    odd_segs = [(0, 2 * D_C, F32, 1.0, False)]
    zeros_conv = jnp.zeros((batch_p, CONV_W - 1, D_RNN), F32)
    zeros_h = jnp.zeros((batch_p, D_RNN), F32)

    p_conv, p_lru, p_k, p_v = [], [], [], []
    s_conv, s_lru, s_k, s_v, s_gv = [], [], [], [], []
    for l in range(DEPTH):
        if l % 2 == 0:
            e = l // 2
            lam_init = 0.8 - 0.6 * math.exp(-0.3 * l)
            pa = dict(conv_w=conv_w[e], conv_b=conv_b[e].reshape(1, D_RNN),
                      wa_bd=_block_diag(lru_wa[e]), ba=lru_ba[e].reshape(1, D_RNN),
                      wx_bd=_block_diag(lru_wx[e]), bx=lru_bx[e].reshape(1, D_RNN),
                      lam=lru_lambda[e].reshape(1, D_RNN))
            pb = dict(bias_table=rel_bias_table, lam_vecs=lam_vecs[e],
                      subln_w=subln_w[e].reshape(1, B_E))
            gx, q, k, v, w_in_b = _norm_matmul(ys, norm_mix_w[l], w_even_in, e, even_segs,
                                               tm=tm_s, emit_bf16=True)
            ya, c_new, h_new = _mixer_a(gx, state_conv[e], state_lru[e], pa, batch=batch_s,
                                        seq=seq_s, tt=seq_s)
            yb = _attn_sample(q, k, v, cache_k, cache_v, e, pb, lam_init, batch=batch_s,
                              t_new=seq_s)
            ys, w_out_b = _matmul_res([ya, yb], w_even_out, e, ys, tm=tm_s, emit_bf16=True)
            s_conv.append(c_new)
            s_lru.append(h_new.reshape(batch_s, D_RNN))
            s_k.append(k.reshape(batch_s, seq_s, B_HEADS, B_E))
            s_v.append(v.reshape(batch_s, seq_s, B_HEADS, B_E))
            gx, q, k, v, k_b, vt_b = _norm_matmul(yp, norm_mix_w[l], w_in_b, 0, even_segs_p,
                                                  tm=tm_p)
            ya, c_new, h_new = _mixer_a(gx, zeros_conv, zeros_h, pa, batch=batch_p, seq=seq_p,
                                        tt=512)
            yb = _attn_prompt(q, k_b, vt_b, pb, lam_init)
            yp = _matmul_res([ya, yb], w_out_b, 0, yp, tm=tm_p, resident_w=True)
            p_conv.append(c_new)
            p_lru.append(h_new.reshape(batch_p, D_RNN))
            p_k.append(k.reshape(batch_p, seq_p, B_HEADS, B_E))
            p_v.append(v.reshape(batch_p, seq_p, B_HEADS, B_E))
        else:
            o = l // 2
            z, w_in_b = _norm_matmul(ys, norm_mix_w[l], w_odd_in, o, odd_segs, act="gelu",
                                     tm=tm_s, emit_bf16=True)
            gated, gv = _gating(z, gmlp_vnorm_w[o], gmlp_ws[o][:, :seq_s, :seq_s],
                                gmlp_bs[o][:, :seq_s].T, chunk=seq_s, n_chunks=1, emit_v=True)
            ys, w_out_b = _matmul_res([gated], w_odd_out, o, ys, tm=tm_s, emit_bf16=True)
            s_gv.append(gv.reshape(batch_s, seq_s, D_C))
            (z,) = _norm_matmul(yp, norm_mix_w[l], w_in_b, 0, odd_segs, act="gelu", tm=tm_p,
                                tn=1024)
            gated, _ = _gating(z, gmlp_vnorm_w[o], gmlp_ws[o], gmlp_bs[o].T, chunk=GMLP_CHUNK,
                               n_chunks=4, emit_v=False)
            yp = _matmul_res([gated], w_out_b, 0, yp, tm=tm_p, resident_w=True)
        last = l == DEPTH - 1
        ys, wu_b, wd_b = _ffn(ys, norm_ffn_w[l], w_ff_up, w_ff_down, l, norm_final_w, tm=tm_s,
                              final=last, emit_bf16=True)
        yp = _ffn(yp, norm_ffn_w[l], wu_b, wd_b, 0, norm_final_w, tm=tm_p, final=last)

    return (yp.reshape(batch_p, seq_p, D_MODEL), ys.reshape(batch_s, seq_s, D_MODEL),
            jnp.stack(p_conv), jnp.stack(p_lru), jnp.stack(p_k), jnp.stack(p_v),
            jnp.stack(s_conv), jnp.stack(s_lru), jnp.stack(s_k), jnp.stack(s_v),
            jnp.stack(s_gv))
```

```python
import functools
import math

import jax
import jax.numpy as jnp
from jax import lax
from jax.experimental import pallas as pl
from jax.experimental.pallas import tpu as pltpu

D_MODEL = 2048
DEPTH = 4
CHUNK = 64
EPS = 1e-6
D_RNN = D_MODEL // 2
LRU_BLOCKS = 16
LRU_BW = D_RNN // LRU_BLOCKS
CONV_W = 4
LRU_C = 8.0
B_HEADS = 8
B_DH = 64
B_E = 2 * B_DH
B_W = B_HEADS * B_E
N_BUCKETS = 32
MAX_DIST = 512
GMLP_CHUNK = 128
D_C = D_MODEL
C_GROUPS = 16
C_GW = D_C // C_GROUPS
D_FF = 4 * D_MODEL

F32 = jnp.float32
BF16 = jnp.bfloat16
MASK_VALUE = -1e30
LOG2E = math.log2(math.e)
MIB = 1024 * 1024
MXU_DIM = 256
NORM_ROWS = 256
_VMEM_MIB = dict(norm_matmul=56, matmul_res=48, ffn=56, mixer_a=40, attn_prompt=48,
                 attn_sample=48, gating=40, gated_out=48)

_NB = N_BUCKETS // 2
_MAX_EXACT = _NB // 2
_BUCKET_THRESHOLDS = tuple(
    int(math.ceil(_MAX_EXACT * (MAX_DIST / _MAX_EXACT) ** (k / (_NB - _MAX_EXACT)) - 1e-9))
    for k in range(1, _NB - _MAX_EXACT))
_BUCKET_LOWER_BOUNDS = tuple(range(1, _MAX_EXACT + 1)) + _BUCKET_THRESHOLDS
assert len(_BUCKET_LOWER_BOUNDS) == _NB - 1
_FAR_BUCKET = _NB - 1
_FAR_DIST = _BUCKET_THRESHOLDS[-1]


def _dot(a, b):
    return jnp.dot(a, b, preferred_element_type=F32)


def _dot_nt(a, b):
    return lax.dot_general(a, b, (((1,), (1,)), ((), ())), preferred_element_type=F32)


def _rms_rows(x, w):
    ms = jnp.mean(x * x, axis=-1, keepdims=True)
    return x * lax.rsqrt(ms + EPS) * w


def _gelu_tanh(x):
    c = math.sqrt(2.0 / math.pi)
    return x * (0.5 * (1.0 + jnp.tanh(c * (x + 0.044715 * (x * x * x)))))


def _sigmoid(x):
    return 0.5 * jnp.tanh(0.5 * x) + 0.5


def _params(name, grid_rank):
    return pltpu.CompilerParams(dimension_semantics=("arbitrary",) * grid_rank,
                                vmem_limit_bytes=_VMEM_MIB[name] * MIB)


def _norm_matmul_kernel(x_ref, nw_ref, w_ref, *refs, segs, act, tm, norm_rows, emit_bf16):
    out_refs = refs[:len(segs)]
    n = pl.program_id(1)
    if emit_bf16:
        wb_ref, xn_ref = refs[len(segs):]
        wb_ref[...] = w_ref[...].astype(BF16)
        w_ref = wb_ref
    else:
        xn_ref = refs[len(segs)]

    def project(xn):
        acc = _dot(xn, w_ref[...])
        return _gelu_tanh(acc) if act == "gelu" else acc

    lo0, hi0, scale0, transposed0 = segs[0]
    assert lo0 == 0 and hi0 > 0 and not transposed0

    @pl.when(n == 0)
    def _():
        for r in range(0, tm, norm_rows):
            rows = slice(r, r + norm_rows)
            xn = _rms_rows(x_ref[rows, :], nw_ref[...]).astype(BF16)
            xn_ref[rows, :] = xn
            acc = project(xn)
            val = acc if scale0 == 1.0 else acc * scale0
            out_refs[0][rows, :] = val.astype(out_refs[0].dtype)

    @pl.when(n > 0)
    def _():
        acc = project(xn_ref[...])
        for (lo, hi, scale, transposed), o_ref in zip(segs, out_refs):
            def store(o_ref=o_ref, scale=scale, transposed=transposed):
                val = acc if scale == 1.0 else acc * scale
                if transposed:
                    val = val.T
                o_ref[...] = val.astype(o_ref.dtype)
            if len(segs) == 1:
                store()
            else:
                pl.when((n >= lo) & (n < hi))(store)


def _norm_matmul(x, nw, w, layer, seg_defs, *, act=None, tm, tn=512, emit_bf16=False):
    m, d = x.shape
    n_total = w.shape[2]
    assert m % tm == 0 and n_total % tn == 0 and (not emit_bf16 or m == tm)
    segs, out_shapes, out_specs = [], [], []
    for first, cols, dtype, scale, transposed in seg_defs:
        assert cols % tn == 0 and first % tn == 0
        lo, nb = first // tn, cols // tn
        segs.append((lo, lo + nb, scale, transposed))
        if transposed:
            out_shapes.append(jax.ShapeDtypeStruct((cols, m), dtype))
            out_specs.append(pl.BlockSpec(
                (tn, tm), lambda i, j, lo=lo, nb=nb: (jnp.clip(j - lo, 0, nb - 1), i)))
        else:
            out_shapes.append(jax.ShapeDtypeStruct((m, cols), dtype))
            out_specs.append(pl.BlockSpec(
                (tm, tn), lambda i, j, lo=lo, nb=nb: (i, jnp.clip(j - lo, 0, nb - 1))))
    if emit_bf16:
        out_shapes.append(jax.ShapeDtypeStruct((1, d, n_total), BF16))
        out_specs.append(pl.BlockSpec((None, d, tn), lambda i, j: (0, 0, j)))
    kern = functools.partial(_norm_matmul_kernel, segs=tuple(segs), act=act, tm=tm,
                             norm_rows=min(tm, NORM_ROWS), emit_bf16=emit_bf16)
    return pl.pallas_call(
        kern,
        out_shape=out_shapes,
        grid=(m // tm, n_total // tn),
        in_specs=[pl.BlockSpec((tm, d), lambda i, j: (i, 0)),
                  pl.BlockSpec((1, d), lambda i, j: (0, 0)),
                  pl.BlockSpec((None, d, tn), lambda i, j: (layer, 0, j))],
        out_specs=out_specs,
        scratch_shapes=[pltpu.VMEM((tm, d), BF16)],
        compiler_params=_params("norm_matmul", 2),
        name="norm_matmul",
    )(x, nw.reshape(1, d), w)


def _matmul_res_kernel(*refs, n_in, tn, emit_bf16, resident_w):
    a_refs = refs[:n_in]
    w_refs = refs[n_in:2 * n_in]
    res_ref = refs[2 * n_in]
    o_ref = refs[2 * n_in + 1]
    if resident_w:
        c0 = pl.multiple_of(pl.program_id(1) * tn, tn)
        w_refs = [w_ref.at[:, pl.ds(c0, tn)] for w_ref in w_refs]
    acc = res_ref[...]
    if emit_bf16:
        wb_ref = refs[2 * n_in + 2]
        for b, (a_ref, w_ref) in enumerate(zip(a_refs, w_refs)):
            wb_ref[b] = w_ref[...].astype(BF16)
            acc = acc + _dot(a_ref[...], wb_ref[b])
    else:
        for a_ref, w_ref in zip(a_refs, w_refs):
            acc = acc + _dot(a_ref[...], w_ref[...])
    o_ref[...] = acc


def _matmul_res(a_list, w, layer, res, *, tm, tn=1024, emit_bf16=False, resident_w=False):
    m, n_total = res.shape
    n_in = len(a_list)
    k_total = w.shape[1]
    kb = k_total // n_in
    assert all(a.shape == (m, kb) for a in a_list) and m % tm == 0 and n_total % tn == 0
    assert (not emit_bf16 or m == tm) and not (emit_bf16 and resident_w)
    in_specs = [pl.BlockSpec((tm, kb), lambda i, j: (i, 0)) for _ in a_list]
    if resident_w:
        in_specs += [pl.BlockSpec((None, kb, n_total), lambda i, j, b=b: (layer, b, 0),
                                  pipeline_mode=pl.Buffered(1)) for b in range(n_in)]
    else:
        in_specs += [pl.BlockSpec((None, kb, tn), lambda i, j, b=b: (layer, b, j))
                     for b in range(n_in)]
    in_specs += [pl.BlockSpec((tm, tn), lambda i, j: (i, j))]
    out_shape = [jax.ShapeDtypeStruct((m, n_total), F32)]
    out_specs = [pl.BlockSpec((tm, tn), lambda i, j: (i, j))]
    if emit_bf16:
        out_shape += [jax.ShapeDtypeStruct((n_in, kb, n_total), BF16)]
        out_specs += [pl.BlockSpec((n_in, kb, tn), lambda i, j: (0, 0, j))]
    res_out = pl.pallas_call(
        functools.partial(_matmul_res_kernel, n_in=n_in, tn=tn, emit_bf16=emit_bf16,
                          resident_w=resident_w),
        out_shape=out_shape,
        grid=(m // tm, n_total // tn),
        in_specs=in_specs,
        out_specs=out_specs,
        compiler_params=_params("matmul_res", 2),
        name="matmul_res",
    )(*a_list, *([w] * n_in), res)
    if emit_bf16:
        return res_out[0], res_out[1].reshape(1, k_total, n_total)
    return res_out[0]


def _ffn_kernel(x_ref, nw_ref, wu_ref, wd_ref, fw_ref, o_ref, *refs, tm, norm_rows, col_chunk,
                final, emit_bf16):
    if emit_bf16:
        wub_ref, wdb_ref, xn_ref = refs
        wub_ref[...] = wu_ref[...].astype(BF16)
        wdb_ref[...] = wd_ref[...].astype(BF16)
        wu_ref, wd_ref = wub_ref, wdb_ref
    else:
        (xn_ref,) = refs
    f = pl.program_id(1)
    d = o_ref.shape[1]

    def mlp(xn):
        h = jnp.maximum(_dot(xn, wu_ref[...]), 0.0)
        return (h * h).astype(BF16)

    @pl.when(f == 0)
    def _():
        for r in range(0, tm, norm_rows):
            rows = slice(r, r + norm_rows)
            x = x_ref[rows, :]
            xn = _rms_rows(x, nw_ref[...]).astype(BF16)
            xn_ref[rows, :] = xn
            h = mlp(xn)
            for c0 in range(0, d, col_chunk):
                cols = slice(c0, c0 + col_chunk)
                o_ref[rows, cols] = x[:, cols] + _dot(h, wd_ref[:, cols])

    last = pl.num_programs(1) - 1

    @pl.when((f > 0) & (f < last) if final else f > 0)
    def _():
        h = mlp(xn_ref[...])
        for c0 in range(0, d, col_chunk):
            o_ref[:, c0:c0 + col_chunk] += _dot(h, wd_ref[:, c0:c0 + col_chunk])

    if final:
        @pl.when(f == last)
        def _():
            for r in range(0, tm, norm_rows):
                rows = slice(r, r + norm_rows)
                h = mlp(xn_ref[rows, :])
                y = jnp.concatenate(
                    [o_ref[rows, c0:c0 + col_chunk] + _dot(h, wd_ref[:, c0:c0 + col_chunk])
                     for c0 in range(0, d, col_chunk)], axis=1)
                o_ref[rows, :] = _rms_rows(y, fw_ref[...])


def _ffn(x, nw, wu, wd, layer, fw, *, tm, tf=512, final=False, emit_bf16=False):
    m, d = x.shape
    dff = wu.shape[2]
    assert m % tm == 0 and dff % tf == 0 and (not emit_bf16 or m == tm)
    assert dff // tf >= 2
    kern = functools.partial(_ffn_kernel, tm=tm, norm_rows=min(tm, NORM_ROWS), col_chunk=512,
                             final=final, emit_bf16=emit_bf16)
    out_shape = [jax.ShapeDtypeStruct((m, d), F32)]
    out_specs = [pl.BlockSpec((tm, d), lambda i, f: (i, 0))]
    if emit_bf16:
        out_shape += [jax.ShapeDtypeStruct((1, d, dff), BF16),
                      jax.ShapeDtypeStruct((1, dff, d), BF16)]
        out_specs += [pl.BlockSpec((None, d, tf), lambda i, f: (0, 0, f)),
                      pl.BlockSpec((None, tf, d), lambda i, f: (0, f, 0))]
    res = pl.pallas_call(
        kern,
        out_shape=out_shape,
        grid=(m // tm, dff // tf),
        in_specs=[pl.BlockSpec((tm, d), lambda i, f: (i, 0)),
                  pl.BlockSpec((1, d), lambda i, f: (0, 0)),
                  pl.BlockSpec((None, d, tf), lambda i, f: (layer, 0, f)),
                  pl.BlockSpec((None, tf, d), lambda i, f: (layer, f, 0)),
                  pl.BlockSpec((1, d), lambda i, f: (0, 0))],
        out_specs=out_specs,
        scratch_shapes=[pltpu.VMEM((tm, d), BF16)],
        compiler_params=_params("ffn", 2),
        name="ffn",
    )(x, nw.reshape(1, d), wu, wd, fw.reshape(1, d))
    return res if emit_bf16 else res[0]


_XP_PAD = 8


def _scan_group(a, u, hb, row):
    for dist in (1, 2, 4):
        keep = row >= dist
        a_s = jnp.where(keep, pltpu.roll(a, dist, 0), 1.0)
        u_s = jnp.where(keep, pltpu.roll(u, dist, 0), 0.0)
        u = a * u_s + u
        a = a * a_s
    return a * hb + u


def _mixer_a_kernel(g_ref, xr_ref, cprev_ref, hprev_ref, cw_ref, cb_ref, wa_ref, ba_ref,
                    wx_ref, bx_ref, lam_ref, ya_ref, conv_ref, hlast_ref, xp_ref, h_ref,
                    *, tt, rc):
    t = pl.program_id(1)
    hist = CONV_W - 1

    @pl.when(t == 0)
    def _():
        xp_ref[_XP_PAD - hist:_XP_PAD, :] = cprev_ref[0]
        h_ref[...] = hprev_ref[0]

    @pl.when(t > 0)
    def _():
        xp_ref[_XP_PAD - hist:_XP_PAD, :] = xp_ref[_XP_PAD + tt - hist:_XP_PAD + tt, :]

    xp_ref[_XP_PAD:_XP_PAD + tt, :] = xr_ref[...]

    z = -lam_ref[...]
    softplus = jnp.maximum(z, 0.0) + jnp.log1p(jnp.exp(-jnp.abs(z)))
    row = lax.broadcasted_iota(jnp.int32, (8, D_RNN), 0)
    nblk = D_RNN // MXU_DIM
    hb = jnp.broadcast_to(h_ref[...], (8, D_RNN))

    for c in range(tt // rc):
        r0 = c * rc
        xc = cb_ref[...]
        for j in range(CONV_W):
            s = _XP_PAD - hist + j + r0
            xc = xc + xp_ref[s:s + rc, :] * cw_ref[j:j + 1, :]
        xcb = xc.astype(BF16)
        r_lin = jnp.concatenate(
            [_dot(xcb[:, MXU_DIM * b:MXU_DIM * (b + 1)], wa_ref[b]) for b in range(nblk)],
            axis=1)
        i_lin = jnp.concatenate(
            [_dot(xcb[:, MXU_DIM * b:MXU_DIM * (b + 1)], wx_ref[b]) for b in range(nblk)],
            axis=1)
        r = _sigmoid(r_lin + ba_ref[...])
        i = _sigmoid(i_lin + bx_ref[...])
        log_a = (-LRU_C) * r * softplus
        a = jnp.exp(log_a)
        th = jnp.tanh(log_a)
        u = jnp.sqrt((-2.0 * th) / (1.0 - th)) * (i * xc)
        hs = []
        for k in range(rc // 8):
            hrows = _scan_group(a[8 * k:8 * k + 8], u[8 * k:8 * k + 8], hb, row)
            hb = jnp.broadcast_to(hrows[7:8, :], (8, D_RNN))
            hs.append(hrows)
        hs = jnp.concatenate(hs, axis=0)
        g = g_ref[r0:r0 + rc, :]
        ya_ref[r0:r0 + rc, :] = (_gelu_tanh(g) * hs).astype(ya_ref.dtype)

    h_ref[...] = hb[0:1, :]
    conv_ref[0] = xr_ref[tt - hist:tt, :]
    hlast_ref[0] = hb[0:1, :]


def _mixer_a(gx, conv_prev, h_prev, p, *, batch, seq, tt):
    assert seq % tt == 0 and tt % 8 == 0
    rc = min(tt, 64)
    nt = seq // tt
    kern = functools.partial(_mixer_a_kernel, tt=tt, rc=rc)
    row_blk = lambda b, t: (b * nt + t, 0)
    vec = lambda b, t: (0, 0)
    return pl.pallas_call(
        kern,
        out_shape=[jax.ShapeDtypeStruct((batch * seq, D_RNN), BF16),
                   jax.ShapeDtypeStruct((batch, CONV_W - 1, D_RNN), F32),
                   jax.ShapeDtypeStruct((batch, 1, D_RNN), F32)],
        grid=(batch, nt),
        in_specs=[pl.BlockSpec((tt, D_RNN), row_blk),
                  pl.BlockSpec((tt, D_RNN), lambda b, t: (b * nt + t, 1)),
                  pl.BlockSpec((1, CONV_W - 1, D_RNN), lambda b, t: (b, 0, 0)),
                  pl.BlockSpec((1, 1, D_RNN), lambda b, t: (b, 0, 0)),
                  pl.BlockSpec((CONV_W, D_RNN), vec),
                  pl.BlockSpec((1, D_RNN), vec),
                  pl.BlockSpec((D_RNN // MXU_DIM, MXU_DIM, MXU_DIM), lambda b, t: (0, 0, 0)),
                  pl.BlockSpec((1, D_RNN), vec),
                  pl.BlockSpec((D_RNN // MXU_DIM, MXU_DIM, MXU_DIM), lambda b, t: (0, 0, 0)),
                  pl.BlockSpec((1, D_RNN), vec),
                  pl.BlockSpec((1, D_RNN), vec)],
        out_specs=[pl.BlockSpec((tt, D_RNN), row_blk),
                   pl.BlockSpec((1, CONV_W - 1, D_RNN), lambda b, t: (b, 0, 0)),
                   pl.BlockSpec((1, 1, D_RNN), lambda b, t: (b, 0, 0))],
        scratch_shapes=[pltpu.VMEM((_XP_PAD + tt, D_RNN), F32),
                        pltpu.VMEM((1, D_RNN), F32)],
        compiler_params=_params("mixer_a", 2),
        name="mixer_a",
    )(gx, gx, conv_prev, h_prev.reshape(batch, 1, D_RNN), p["conv_w"], p["conv_b"],
      p["wa_bd"], p["ba"], p["wx_bd"], p["bx"], p["lam"])


def _rel_bias(d, tab_ref, h, *, may_be_positive=True):
    n = jnp.abs(d)

    def one_sign(first_bucket):
        val = jnp.full(d.shape, tab_ref[first_bucket, h], F32)
        for k, bound in enumerate(_BUCKET_LOWER_BOUNDS, start=1):
            val = jnp.where(n >= bound, tab_ref[first_bucket + k, h], val)
        return val

    behind = one_sign(0)
    if not may_be_positive:
        return behind
    return jnp.where(d > 0, one_sign(_NB), behind)


def _split_q(q):
    lane = lax.broadcasted_iota(jnp.int32, q.shape, 1)
    zero = jnp.zeros_like(q)
    return jnp.where(lane < B_DH, q, zero), jnp.where(lane >= B_DH, q, zero)


def _diff_lambda(lamv_ref, lam_init):
    lv = lamv_ref[...]
    s01 = jnp.sum(lv[0:1, :] * lv[1:2, :], axis=-1, keepdims=True)
    s23 = jnp.sum(lv[2:3, :] * lv[3:4, :], axis=-1, keepdims=True)
    return jnp.exp(s01) - jnp.exp(s23) + lam_init


def _diff_finish(o1, l1, o2, l2, lam, sub, lam_init):
    o = o1 / l1 - lam * (o2 / l2)
    return _rms_rows(o, sub) * (1.0 - lam_init)


def _attn_prompt_kernel(tab_ref, lamv_ref, subc_ref, q_ref, k_ref, vt_ref, o_ref,
                        b0_ref, b1_ref, qz_ref, s_ref, m_ref, l_ref, acc_ref, *, tq, tk, lam_init):
    h = pl.program_id(0)
    qi = pl.program_id(1)
    nq = tq // tk
    far_bias = tab_ref[_FAR_BUCKET, h]

    @pl.when(qi == 0)
    def _():
        ki_ = lax.broadcasted_iota(jnp.int32, (tk, tk), 0)
        qi_ = lax.broadcasted_iota(jnp.int32, (tk, tk), 1)
        d = ki_ - qi_
        visible = (ki_ // CHUNK) <= (qi_ // CHUNK)
        b0_ref[...] = jnp.where(visible, (_rel_bias(d, tab_ref, h) - far_bias) * LOG2E,
                                MASK_VALUE)
        b1_ref[...] = (_rel_bias(d - tk, tab_ref, h, may_be_positive=False) - far_bias) * LOG2E

    qt = q_ref[...].astype(F32).T
    row = lax.broadcasted_iota(jnp.int32, qt.shape, 0)
    qz_ref[:, 0:tq] = jnp.where(row < B_DH, qt, 0.0).astype(BF16)
    qz_ref[:, tq:2 * tq] = jnp.where(row >= B_DH, qt, 0.0).astype(BF16)
    m_ref[...] = jnp.full(m_ref.shape, MASK_VALUE, F32)
    l_ref[...] = jnp.zeros(l_ref.shape, F32)
    acc_ref[...] = jnp.zeros(acc_ref.shape, F32)

    def scores(ki, br, c, slot):
        r = pl.multiple_of(ki * tk, tk)
        c0 = br * tq + c * tk
        s_ref[slot] = _dot(k_ref[pl.ds(r, tk), :], qz_ref[:, c0:c0 + tk])

    def softmax_pv(ki, br, c, slot, bias_ref):
        c0 = br * tq + c * tk
        cols = slice(c0, c0 + tk)
        r = pl.multiple_of(ki * tk, tk)
        s = s_ref[slot]
        if bias_ref is not None:
            s = s + bias_ref[...]
        m_old = m_ref[:, cols]
        m_new = jnp.maximum(m_old, jnp.max(s, axis=0, keepdims=True))
        alpha = jnp.exp2(m_old - m_new)
        e = jnp.exp2(s - m_new)
        l_ref[:, cols] = alpha * l_ref[:, cols] + jnp.sum(e, axis=0, keepdims=True)
        acc_ref[:, cols] = (alpha * acc_ref[:, cols]
                            + _dot(vt_ref[:, pl.ds(r, tk)], e.astype(BF16)))
        m_ref[:, cols] = m_new

    def run_tile(ki, items, after):
        assert len(items) % 2 == 0
        for idx, (br, c, bias_ref) in enumerate(items):
            slot = idx % 2
            if idx + 1 < len(items):
                scores(ki, items[idx + 1][0], items[idx + 1][1], 1 - slot)
            elif after is not None:
                scores(ki + 1, 0, after, 1 - slot)
            softmax_pv(ki, br, c, slot, bias_ref)

    def block_bias(c, j):
        return b0_ref if c == j else b1_ref if c == j + 1 else None

    scores(0, 0, 0, 0)

    def far_body(ki, carry):
        run_tile(ki, [(br, c, None) for br in range(2) for c in range(nq)], 0)
        return carry
    lax.fori_loop(0, jnp.maximum(nq * qi - 1, 0), far_body, 0)

    @pl.when(qi >= 1)
    def _():
        run_tile(nq * qi - 1, [(br, c, block_bias(c, -1)) for br in range(2)
                               for c in range(nq)], 0)

    for j in range(nq):
        run_tile(nq * qi + j, [(br, c, block_bias(c, j)) for br in range(2)
                               for c in range(j, nq)], j + 1 if j + 1 < nq else None)

    lam = _diff_lambda(lamv_ref, lam_init)
    o = (acc_ref[:, 0:tq] / l_ref[:, 0:tq]
         - lam * (acc_ref[:, tq:2 * tq] / l_ref[:, tq:2 * tq]))
    ms = jnp.mean(o * o, axis=0, keepdims=True)
    o = o * lax.rsqrt(ms + EPS) * subc_ref[...] * (1.0 - lam_init)
    o_ref[...] = o.T.astype(o_ref.dtype)


def _attn_prompt(q, k, vt, p, lam_init, *, tq=2048, tk=512):
    s = q.shape[0]
    assert s % tq == 0 and tq % tk == 0 and tk % CHUNK == 0 and tk + 1 >= _FAR_DIST
    kern = functools.partial(_attn_prompt_kernel, tq=tq, tk=tk, lam_init=lam_init)
    return pl.pallas_call(
        kern,
        out_shape=jax.ShapeDtypeStruct((s, B_W), BF16),
        grid=(B_HEADS, s // tq),
        in_specs=[pl.BlockSpec(memory_space=pltpu.SMEM),
                  pl.BlockSpec((4, B_DH), lambda h, i: (0, 0)),
                  pl.BlockSpec((B_E, 1), lambda h, i: (0, 0)),
                  pl.BlockSpec((tq, B_E), lambda h, i: (i, h)),
                  pl.BlockSpec((s, B_E), lambda h, i: (0, h)),
                  pl.BlockSpec((B_E, s), lambda h, i: (h, 0))],
        out_specs=pl.BlockSpec((tq, B_E), lambda h, i: (i, h)),
        scratch_shapes=[pltpu.VMEM((tk, tk), F32), pltpu.VMEM((tk, tk), F32),
                        pltpu.VMEM((B_E, 2 * tq), BF16),
                        pltpu.VMEM((2, tk, tk), F32),
                        pltpu.VMEM((1, 2 * tq), F32), pltpu.VMEM((1, 2 * tq), F32),
                        pltpu.VMEM((B_E, 2 * tq), F32)],
        compiler_params=_params("attn_prompt", 2),
        name="attn_prompt",
    )(p["bias_table"], p["lam_vecs"], p["subln_w"].reshape(B_E, 1), q, k, vt)


_SAMPLE_COL_CHUNK = 2048


def _attn_sample_kernel(tab_ref, lamv_ref, sub_ref, q_ref, kn_ref, vn_ref, kc_ref, vc_ref, o_ref,
                        bc_ref, bn_ref, s_ref, *, t_new, past, lam_init):
    rows_h = 2 * t_new
    n_keys = past * B_HEADS

    @pl.when(pl.program_id(0) == 0)
    def _():
        t_c = lax.broadcasted_iota(jnp.int32, (t_new, n_keys), 0)
        col = lax.broadcasted_iota(jnp.int32, (t_new, n_keys), 1)
        d_c = col // B_HEADS - past - t_c
        head_c = col % B_HEADS
        t_n = lax.broadcasted_iota(jnp.int32, (t_new, B_HEADS * t_new), 0)
        col_n = lax.broadcasted_iota(jnp.int32, (t_new, B_HEADS * t_new), 1)
        d_n = col_n % t_new - t_n
        head_n = col_n // t_new
        for h in range(B_HEADS):
            tile_c = jnp.where(head_c == h,
                               LOG2E * _rel_bias(d_c, tab_ref, h, may_be_positive=False),
                               MASK_VALUE)
            tile_n = jnp.where(head_n == h, LOG2E * _rel_bias(d_n, tab_ref, h), MASK_VALUE)
            for br in range(2):
                rows = slice(h * rows_h + br * t_new, h * rows_h + (br + 1) * t_new)
                bc_ref[rows, :] = tile_c
                bn_ref[rows, :] = tile_n

    q_parts, kn_parts, vn_parts = [], [], []
    for h in range(B_HEADS):
        cols = slice(h * B_E, (h + 1) * B_E)
        q_parts.extend(_split_q(q_ref[:, cols]))
        kn_parts.append(kn_ref[:, cols])
        vn_parts.append(vn_ref[:, cols])
    qblk = jnp.concatenate(q_parts, axis=0)
    kn = jnp.concatenate(kn_parts, axis=0).astype(BF16)
    vn = jnp.concatenate(vn_parts, axis=0).astype(BF16)

    s_n = _dot_nt(qblk, kn) + bn_ref[...]
    m = jnp.max(s_n, axis=-1, keepdims=True)
    for c0 in range(0, n_keys, _SAMPLE_COL_CHUNK):
        cc = slice(c0, c0 + _SAMPLE_COL_CHUNK)
        s = _dot_nt(qblk, kc_ref[cc, :].astype(BF16)) + bc_ref[:, cc]
        s_ref[:, cc] = s
        m = jnp.maximum(m, jnp.max(s, axis=-1, keepdims=True))

    e_n = jnp.exp2(s_n - m)
    l = jnp.sum(e_n, axis=-1, keepdims=True)
    o = _dot(e_n.astype(BF16), vn)
    for c0 in range(0, n_keys, _SAMPLE_COL_CHUNK):
        cc = slice(c0, c0 + _SAMPLE_COL_CHUNK)
        e = jnp.exp2(s_ref[:, cc] - m)
        l = l + jnp.sum(e, axis=-1, keepdims=True)
        o = o + _dot(e.astype(BF16), vc_ref[cc, :].astype(BF16))

    lam = _diff_lambda(lamv_ref, lam_init)
    for h in range(B_HEADS):
        r1 = slice(h * rows_h, h * rows_h + t_new)
        r2 = slice(h * rows_h + t_new, (h + 1) * rows_h)
        o_ref[:, h * B_E:(h + 1) * B_E] = _diff_finish(
            o[r1], l[r1], o[r2], l[r2], lam, sub_ref[...], lam_init).astype(o_ref.dtype)


def _attn_sample(q, k_new, v_new, k_cache, v_cache, layer, p, lam_init, *, batch, t_new):
    n_layers, _, past, _, _ = k_cache.shape
    n_keys = past * B_HEADS
    assert n_keys % _SAMPLE_COL_CHUNK == 0
    kern = functools.partial(_attn_sample_kernel, t_new=t_new, past=past, lam_init=lam_init)
    new_blk = pl.BlockSpec((t_new, B_W), lambda b: (b, 0))
    cache_blk = pl.BlockSpec((None, None, n_keys, B_E), lambda b: (layer, b, 0, 0))
    n_rows = B_HEADS * 2 * t_new
    return pl.pallas_call(
        kern,
        out_shape=jax.ShapeDtypeStruct((batch * t_new, B_W), BF16),
        grid=(batch,),
        in_specs=[pl.BlockSpec(memory_space=pltpu.SMEM),
                  pl.BlockSpec((4, B_DH), lambda b: (0, 0)),
                  pl.BlockSpec((1, B_E), lambda b: (0, 0)),
                  new_blk, new_blk, new_blk, cache_blk, cache_blk],
        out_specs=new_blk,
        scratch_shapes=[pltpu.VMEM((n_rows, n_keys), F32),
                        pltpu.VMEM((n_rows, B_HEADS * t_new), F32),
                        pltpu.VMEM((n_rows, n_keys), F32)],
        compiler_params=_params("attn_sample", 1),
        name="attn_sample",
    )(p["bias_table"], p["lam_vecs"], p["subln_w"], q, k_new, v_new,
      k_cache.reshape(n_layers, batch, n_keys, B_E), v_cache.reshape(n_layers, batch, n_keys, B_E))


def _gating_kernel(u_ref, v_ref, vnw_ref, ws_ref, bs_ref, *refs, chunk, n_chunks, emit_v):
    if emit_v:
        o_ref, vout_ref, wsm_ref = refs
    else:
        o_ref, wsm_ref = refs
        vout_ref = None

    @pl.when(pl.program_id(0) == 0)
    def _():
        ri = lax.broadcasted_iota(jnp.int32, (chunk, chunk), 0)
        ci = lax.broadcasted_iota(jnp.int32, (chunk, chunk), 1)
        for g in range(C_GROUPS):
            wsm_ref[g] = jnp.where(ci <= ri, ws_ref[g], 0.0).astype(BF16)

    for c in range(n_chunks):
        r0 = c * chunk
        vn = _rms_rows(v_ref[r0:r0 + chunk, :], vnw_ref[...])
        if emit_v:
            vout_ref[r0:r0 + chunk, :] = vn
        vnb = vn.astype(BF16)
        for g in range(C_GROUPS):
            cols = slice(g * C_GW, (g + 1) * C_GW)
            mix = _dot(wsm_ref[g], vnb[:, cols]) + bs_ref[:, g:g + 1]
            o_ref[r0:r0 + chunk, cols] = (u_ref[r0:r0 + chunk, cols] * mix).astype(o_ref.dtype)


def _gating(z, vnw, ws, bs_t, *, chunk, n_chunks, emit_v):
    m = z.shape[0]
    rt = chunk * n_chunks
    assert m % rt == 0
    kern = functools.partial(_gating_kernel, chunk=chunk, n_chunks=n_chunks, emit_v=emit_v)
    out_shape = [jax.ShapeDtypeStruct((m, D_C), BF16)]
    out_specs = [pl.BlockSpec((rt, D_C), lambda i: (i, 0))]
    if emit_v:
        out_shape.append(jax.ShapeDtypeStruct((m, D_C), F32))
        out_specs.append(pl.BlockSpec((rt, D_C), lambda i: (i, 0)))
    res = pl.pallas_call(
        kern,
        out_shape=out_shape,
        grid=(m // rt,),
        in_specs=[pl.BlockSpec((rt, D_C), lambda i: (i, 0)),
                  pl.BlockSpec((rt, D_C), lambda i: (i, 1)),
                  pl.BlockSpec((1, D_C), lambda i: (0, 0)),
                  pl.BlockSpec((C_GROUPS, chunk, chunk), lambda i: (0, 0, 0)),
                  pl.BlockSpec((chunk, C_GROUPS), lambda i: (0, 0))],
        out_specs=out_specs,
        scratch_shapes=[pltpu.VMEM((C_GROUPS, chunk, chunk), BF16)],
        compiler_params=_params("gating", 1),
        name="gating",
    )(z, z, vnw.reshape(1, D_C), ws, bs_t)
    return res if emit_v else (res[0], None)


def _gated_out_kernel(u_ref, v_ref, vnw_ref, ws_ref, bs_ref, w_ref, res_ref, o_ref, wsm_ref,
                      g_ref, *, chunk, n_chunks, tn):
    i = pl.program_id(0)
    j = pl.program_id(1)

    @pl.when((i == 0) & (j == 0))
    def _():
        ri = lax.broadcasted_iota(jnp.int32, (chunk, chunk), 0)
        ci = lax.broadcasted_iota(jnp.int32, (chunk, chunk), 1)
        for g in range(C_GROUPS):
            wsm_ref[g] = jnp.where(ci <= ri, ws_ref[g], 0.0).astype(BF16)

    @pl.when(j == 0)
    def _():
        for c in range(n_chunks):
            r0 = c * chunk
            vnb = _rms_rows(v_ref[r0:r0 + chunk, :], vnw_ref[...]).astype(BF16)
            for g in range(C_GROUPS):
                cols = slice(g * C_GW, (g + 1) * C_GW)
                mix = _dot(wsm_ref[g], vnb[:, cols]) + bs_ref[:, g:g + 1]
                g_ref[r0:r0 + chunk, cols] = (u_ref[r0:r0 + chunk, cols] * mix).astype(BF16)

    c0 = pl.multiple_of(j * tn, tn)
    o_ref[...] = res_ref[...] + _dot(g_ref[...], w_ref[:, pl.ds(c0, tn)])


def _gated_out(z, vnw, ws, bs_t, w, res, *, chunk, n_chunks, tn=1024):
    m, n_total = res.shape
    rt = chunk * n_chunks
    assert m % rt == 0 and n_total % tn == 0
    kern = functools.partial(_gated_out_kernel, chunk=chunk, n_chunks=n_chunks, tn=tn)
    return pl.pallas_call(
        kern,
        out_shape=jax.ShapeDtypeStruct((m, n_total), F32),
        grid=(m // rt, n_total // tn),
        in_specs=[pl.BlockSpec((rt, D_C), lambda i, j: (i, 0)),
                  pl.BlockSpec((rt, D_C), lambda i, j: (i, 1)),
                  pl.BlockSpec((1, D_C), lambda i, j: (0, 0)),
                  pl.BlockSpec((C_GROUPS, chunk, chunk), lambda i, j: (0, 0, 0)),
                  pl.BlockSpec((chunk, C_GROUPS), lambda i, j: (0, 0)),
                  pl.BlockSpec((None, D_C, n_total), lambda i, j: (0, 0, 0),
                               pipeline_mode=pl.Buffered(1)),
                  pl.BlockSpec((rt, tn), lambda i, j: (i, j))],
        out_specs=pl.BlockSpec((rt, tn), lambda i, j: (i, j)),
        scratch_shapes=[pltpu.VMEM((C_GROUPS, chunk, chunk), BF16),
                        pltpu.VMEM((rt, D_C), BF16)],
        compiler_params=_params("gated_out", 2),
        name="gated_out",
    )(z, z, vnw.reshape(1, D_C), ws, bs_t, w, res)


def _block_diag(w):
    per = MXU_DIM // LRU_BW
    w4 = w.reshape(LRU_BLOCKS // per, per, LRU_BW, LRU_BW)
    eye = jnp.eye(per, dtype=w.dtype)
    bd = jnp.einsum("gaij,ab->gaibj", w4, eye)
    return bd.reshape(LRU_BLOCKS // per, MXU_DIM, MXU_DIM).astype(BF16)


def kernel(x_prompt, x_sample, state_conv, state_lru, cache_k, cache_v, w_even_in, w_even_out,
           conv_w, conv_b, lru_wa, lru_ba, lru_wx, lru_bx, lru_lambda, lam_vecs, subln_w,
           rel_bias_table, w_odd_in, w_odd_out, gmlp_vnorm_w, gmlp_ws, gmlp_bs, norm_mix_w,
           norm_ffn_w, norm_final_w, w_ff_up, w_ff_down):
    batch_p, seq_p, _ = x_prompt.shape
    batch_s, seq_s, _ = x_sample.shape
    assert batch_p == 1
    yp = x_prompt.reshape(batch_p * seq_p, D_MODEL)
    ys = x_sample.reshape(batch_s * seq_s, D_MODEL)
    tm_p = 1024
    tm_s = batch_s * seq_s
    scale = B_DH ** -0.5 * LOG2E

    q0, k0, v0 = 2 * D_RNN, 2 * D_RNN + B_W, 2 * D_RNN + 2 * B_W
    even_segs = [(0, 2 * D_RNN, F32, 1.0, False), (q0, B_W, BF16, scale, False),
                 (k0, B_W, F32, 1.0, False), (v0, B_W, F32, 1.0, False)]
    even_segs_p = even_segs + [(k0, B_W, BF16, 1.0, False), (v0, B_W, BF16, 1.0, True)]
    odd_segs = [(0, 2 * D_C, F32, 1.0, False)]
    zeros_conv = jnp.zeros((batch_p, CONV_W - 1, D_RNN), F32)
    zeros_h = jnp.zeros((batch_p, D_RNN), F32)

    p_conv, p_lru, p_k, p_v = [], [], [], []
    s_conv, s_lru, s_k, s_v, s_gv = [], [], [], [], []
    for l in range(DEPTH):
        if l % 2 == 0:
            e = l // 2
            lam_init = 0.8 - 0.6 * math.exp(-0.3 * l)
            pa = dict(conv_w=conv_w[e], conv_b=conv_b[e].reshape(1, D_RNN),
                      wa_bd=_block_diag(lru_wa[e]), ba=lru_ba[e].reshape(1, D_RNN),
                      wx_bd=_block_diag(lru_wx[e]), bx=lru_bx[e].reshape(1, D_RNN),
                      lam=lru_lambda[e].reshape(1, D_RNN))
            pb = dict(bias_table=rel_bias_table, lam_vecs=lam_vecs[e],
                      subln_w=subln_w[e].reshape(1, B_E))
            gx, q, k, v, w_in_b = _norm_matmul(ys, norm_mix_w[l], w_even_in, e, even_segs,
                                               tm=tm_s, emit_bf16=True)
            ya, c_new, h_new = _mixer_a(gx, state_conv[e], state_lru[e], pa, batch=batch_s,
                                        seq=seq_s, tt=seq_s)
            yb = _attn_sample(q, k, v, cache_k, cache_v, e, pb, lam_init, batch=batch_s,
                              t_new=seq_s)
            ys, w_out_b = _matmul_res([ya, yb], w_even_out, e, ys, tm=tm_s, emit_bf16=True)
            s_conv.append(c_new)
            s_lru.append(h_new.reshape(batch_s, D_RNN))
            s_k.append(k.reshape(batch_s, seq_s, B_HEADS, B_E))
            s_v.append(v.reshape(batch_s, seq_s, B_HEADS, B_E))
            gx, q, k, v, k_b, vt_b = _norm_matmul(yp, norm_mix_w[l], w_in_b, 0, even_segs_p,
                                                  tm=tm_p)
            ya, c_new, h_new = _mixer_a(gx, zeros_conv, zeros_h, pa, batch=batch_p, seq=seq_p,
                                        tt=512)
            yb = _attn_prompt(q, k_b, vt_b, pb, lam_init)
            yp = _matmul_res([ya, yb], w_out_b, 0, yp, tm=tm_p, resident_w=True)
            p_conv.append(c_new)
            p_lru.append(h_new.reshape(batch_p, D_RNN))
            p_k.append(k.reshape(batch_p, seq_p, B_HEADS, B_E))
            p_v.append(v.reshape(batch_p, seq_p, B_HEADS, B_E))
        else:
            o = l // 2
            z, w_in_b = _norm_matmul(ys, norm_mix_w[l], w_odd_in, o, odd_segs, act="gelu",
                                     tm=tm_s, emit_bf16=True)
            gated, gv = _gating(z, gmlp_vnorm_w[o], gmlp_ws[o][:, :seq_s, :seq_s],
                                gmlp_bs[o][:, :seq_s].T, chunk=seq_s, n_chunks=1, emit_v=True)
            ys, w_out_b = _matmul_res([gated], w_odd_out, o, ys, tm=tm_s, emit_bf16=True)
            s_gv.append(gv.reshape(batch_s, seq_s, D_C))
            (z,) = _norm_matmul(yp, norm_mix_w[l], w_in_b, 0, odd_segs, act="gelu", tm=tm_p,
                                tn=1024)
            yp = _gated_out(z, gmlp_vnorm_w[o], gmlp_ws[o], gmlp_bs[o].T, w_out_b, yp,
                            chunk=GMLP_CHUNK, n_chunks=4)
        last = l == DEPTH - 1
        ys, wu_b, wd_b = _ffn(ys, norm_ffn_w[l], w_ff_up, w_ff_down, l, norm_final_w, tm=tm_s,
                              final=last, emit_bf16=True)
        yp = _ffn(yp, norm_ffn_w[l], wu_b, wd_b, 0, norm_final_w, tm=tm_p, final=last)

    return (yp.reshape(batch_p, seq_p, D_MODEL), ys.reshape(batch_s, seq_s, D_MODEL),
            jnp.stack(p_conv), jnp.stack(p_lru), jnp.stack(p_k), jnp.stack(p_v),
            jnp.stack(s_conv), jnp.stack(s_lru), jnp.stack(s_k), jnp.stack(s_v),
            jnp.stack(s_gv))
```
